```python
import math
import jax, jax.numpy as jnp
from jax import lax
import numpy as np


D_MODEL = 1024
BATCH = 8
SEQ = 2048
DEPTH = 1
DEC_BATCH = 128
DEC_SEQ = 1
PAST_LEN = 2048
PAGE_SIZE = 128

H_DIFF = 4
DH_DIFF = 64
H_DSA = 8
DH_DSA = 64
H_IDX = 8
DH_IDX = 64
TOPK_MAX = 256
N_GROUPS = 4
EXPERTS_PER_GROUP = 8
N_EXPERTS = N_GROUPS * EXPERTS_PER_GROUP
D_EXPERT = 512
TOP_K_EXPERTS = 2
ROPE_THETA = 10000.0
EPS = 1e-6
Q_BLOCK = 128
DSA_Q_BLOCK = 64

W_DIFF = H_DIFF * 2 * DH_DIFF
W_DSA = H_DSA * DH_DSA
W_IDX_Q = H_IDX * DH_IDX
SPLIT_SIZES = (W_DIFF, W_DIFF, W_DIFF, W_DSA, W_DSA, W_DSA, W_IDX_Q, DH_IDX, H_IDX, D_MODEL, D_MODEL)
P_TOTAL = 3 * W_DIFF + 3 * W_DSA + W_IDX_Q + DH_IDX + H_IDX + 2 * D_MODEL

kernel_name = 'diffattn_dsa_hiermoe_step'


def rmsnorm(x, g):
    xf = x.astype(jnp.float32)
    y = xf * lax.rsqrt(jnp.mean(xf * xf, axis=-1, keepdims=True) + EPS)
    return (y * g.astype(jnp.float32)).astype(x.dtype)


def rope(x, pos):
    half = x.shape[-1] // 2
    inv_freq = jnp.power(ROPE_THETA, -jnp.arange(half, dtype=jnp.float32) / half)
    ang = pos.astype(jnp.float32)[:, None] * inv_freq[None, :]
    cos = jnp.cos(ang)[:, None, :]
    sin = jnp.sin(ang)[:, None, :]
    xf = x.astype(jnp.float32)
    x1, x2 = xf[..., :half], xf[..., half:]
    return jnp.concatenate([x1 * cos - x2 * sin, x2 * cos + x1 * sin], axis=-1).astype(x.dtype)


def to_blocks(a, block):
    b, t = a.shape[:2]
    return jnp.moveaxis(a.reshape((b, t // block, block) + a.shape[2:]), 1, 0)


def from_blocks(a):
    a = jnp.moveaxis(a, 0, 1)
    return a.reshape((a.shape[0], a.shape[1] * a.shape[2]) + a.shape[3:])


def gather_pages(cache, page_table):
    g = cache[page_table]
    return g.reshape((g.shape[0], g.shape[1] * g.shape[2]) + g.shape[3:])


def diff_attention(q, k, v, q_pos, lam):
    s = jnp.einsum('bqhcd,bkhcd->bhcqk', q, k).astype(jnp.float32) * (DH_DIFF ** -0.5)
    causal = jnp.arange(k.shape[1])[None, :] <= q_pos[:, None]
    p = jax.nn.softmax(jnp.where(causal, s, -jnp.inf), axis=-1)
    a = p[:, :, 0] - lam * p[:, :, 1]
    return jnp.einsum('bhqk,bkhe->bqhe', a.astype(v.dtype), v)


def dsa_attention(q, k, v, q_idx, k_idx, w_idx, q_pos, k_sel):
    causal = jnp.arange(k.shape[1])[None, :] <= q_pos[:, None]
    dots = jnp.einsum('bqhd,bkd->bqhk', q_idx, k_idx).astype(jnp.float32) * (DH_IDX ** -0.5)
    score = jnp.einsum('bqhk,bqh->bqk', jax.nn.relu(dots), w_idx.astype(jnp.float32))
    score = jnp.where(causal[None], score, -jnp.inf)
    _, idx = lax.top_k(score, k_sel)
    valid = idx <= q_pos[None, :, None]
    k_g = jax.vmap(lambda rows, ii: rows[ii])(k, idx)
    v_g = jax.vmap(lambda rows, ii: rows[ii])(v, idx)
    s = jnp.einsum('bqhd,bqkhd->bhqk', q, k_g).astype(jnp.float32) * (DH_DSA ** -0.5)
    p = jax.nn.softmax(jnp.where(valid[:, None], s, -jnp.inf), axis=-1)
    return jnp.einsum('bhqk,bqkhd->bqhd', p.astype(v.dtype), v_g)


def token_mixing(x, pos, past, lam_init, k_sel, q_block, dsa_block,
                 g_mix, w_in, g_q_diff, g_k_diff, lam_q1, lam_k1, lam_q2, lam_k2, g_subln,
                 g_q_dsa, g_k_dsa, w_o_diff, w_o_dsa, w_out):
    b, t, _ = x.shape
    h = rmsnorm(x, g_mix)
    proj = h @ w_in
    points = np.cumsum(SPLIT_SIZES)[:-1].tolist()
    (q_d, k_d, v_d, q_s, k_s, v_s, q_i, k_i, w_i, gt_d, gt_s) = jnp.split(proj, points, axis=-1)
    q_d = rope(rmsnorm(q_d.reshape(b, t, 2 * H_DIFF, DH_DIFF), g_q_diff), pos).reshape(b, t, H_DIFF, 2, DH_DIFF)
    k_d = rope(rmsnorm(k_d.reshape(b, t, 2 * H_DIFF, DH_DIFF), g_k_diff), pos).reshape(b, t, H_DIFF, 2, DH_DIFF)
    v_d = v_d.reshape(b, t, H_DIFF, 2 * DH_DIFF)
    q_s = rope(rmsnorm(q_s.reshape(b, t, H_DSA, DH_DSA), g_q_dsa), pos)
    k_s = rope(rmsnorm(k_s.reshape(b, t, H_DSA, DH_DSA), g_k_dsa), pos)
    v_s = v_s.reshape(b, t, H_DSA, DH_DSA)
    q_i = rope(q_i.reshape(b, t, H_IDX, DH_IDX), pos)
    k_i = rope(k_i[:, :, None, :], pos)[:, :, 0]
    w_i = w_i * (H_IDX ** -0.5)
    rows = (k_d, v_d, k_s, v_s, k_i)
    if past is None:
        kd_all, vd_all, ks_all, vs_all, ki_all = rows
    else:
        kd_all = jnp.concatenate([past[0], k_d.astype(past[0].dtype)], axis=1)
        vd_all = jnp.concatenate([past[1], v_d.astype(past[1].dtype)], axis=1)
        ks_all = jnp.concatenate([past[2], k_s.astype(past[2].dtype)], axis=1)
        vs_all = jnp.concatenate([past[3], v_s.astype(past[3].dtype)], axis=1)
        ki_all = jnp.concatenate([past[4], k_i.astype(past[4].dtype)], axis=1)
    f32 = jnp.float32
    lam = (jnp.exp(jnp.sum(lam_q1.astype(f32) * lam_k1.astype(f32)))
           - jnp.exp(jnp.sum(lam_q2.astype(f32) * lam_k2.astype(f32))) + lam_init)
    if q_block is None:
        o_d = diff_attention(q_d, kd_all, vd_all, pos, lam)
        o_s = dsa_attention(q_s, ks_all, vs_all, q_i, ki_all, w_i, pos, k_sel)
    else:
        o_d = from_blocks(lax.map(
            lambda a: diff_attention(a[0], kd_all, vd_all, a[1], lam),
            (to_blocks(q_d, q_block), pos.reshape(-1, q_block))))
        o_s = from_blocks(lax.map(
            lambda a: dsa_attention(a[0], ks_all, vs_all, a[1], ki_all, a[2], a[3], k_sel),
            (to_blocks(q_s, dsa_block), to_blocks(q_i, dsa_block), to_blocks(w_i, dsa_block),
             pos.reshape(-1, dsa_block))))
    o_d = rmsnorm(o_d, g_subln) * (1.0 - lam_init)
    y_d = o_d.reshape(b, t, W_DIFF) @ w_o_diff
    y_s = o_s.reshape(b, t, W_DSA) @ w_o_dsa
    merged = jax.nn.sigmoid(gt_d) * y_d + jax.nn.sigmoid(gt_s) * y_s
    return x + merged @ w_out, rows


def hier_moe(x, g_ffn, w_group, w_router, w_gate, w_up, w_down):
    b, t, d = x.shape
    h = rmsnorm(x, g_ffn).reshape(b * t, d)
    g_logits = jnp.dot(h, w_group).astype(jnp.float32)
    g_prob = jax.nn.softmax(g_logits, axis=-1)
    g_sel = jnp.argmax(g_logits, axis=-1)
    g_w = jnp.take_along_axis(g_prob, g_sel[:, None], axis=-1)
    e_logits = jnp.dot(h, w_router).astype(jnp.float32).reshape(-1, N_GROUPS, EXPERTS_PER_GROUP)
    e_in_group = jnp.take_along_axis(e_logits, g_sel[:, None, None], axis=1)[:, 0]
    top_val, top_idx = lax.top_k(e_in_group, TOP_K_EXPERTS)
    top_w = jax.nn.softmax(top_val, axis=-1) * g_w
    expert_id = g_sel[:, None] * EXPERTS_PER_GROUP + top_idx
    combine = jnp.einsum('nk,nke->en', top_w,
                         jax.nn.one_hot(expert_id, N_EXPERTS, dtype=jnp.float32)).astype(h.dtype)

    def add_expert(acc, params):
        wg, wu, wd, c = params
        act = jax.nn.silu(h @ wg) * (h @ wu)
        return acc + (act @ wd) * c[:, None], None

    out, _ = lax.scan(add_expert, jnp.zeros_like(h), (w_gate, w_up, w_down, combine))
    return out.reshape(b, t, d)


def setup_inputs(seed: int = 0) -> dict:
    key = jax.random.key(seed)
    ks = jax.random.split(key, 32)
    n_pages = PAST_LEN // PAGE_SIZE
    n_used = DEC_BATCH * n_pages
    n_pool = n_used + max(1, n_used // 4)
    page_table = jax.random.permutation(ks[0], n_pool)[:n_used].reshape(DEC_BATCH, n_pages).astype(jnp.int32)

    def normal(k, shape, scale=1.0):
        return scale * jax.random.normal(k, shape, jnp.float32)

    def gain(k, shape):
        return 1.0 + 0.02 * jax.random.normal(k, shape, jnp.float32)

    return {
        'x_prompt': normal(ks[1], (BATCH, SEQ, D_MODEL)),
        'x_sample': normal(ks[2], (DEC_BATCH, DEC_SEQ, D_MODEL)),
        'cache_diff_k': normal(ks[3], (DEPTH, n_pool, PAGE_SIZE, H_DIFF, 2, DH_DIFF)),
        'cache_diff_v': normal(ks[4], (DEPTH, n_pool, PAGE_SIZE, H_DIFF, 2 * DH_DIFF)),
        'cache_dsa_k': normal(ks[5], (DEPTH, n_pool, PAGE_SIZE, H_DSA, DH_DSA)),
        'cache_dsa_v': normal(ks[6], (DEPTH, n_pool, PAGE_SIZE, H_DSA, DH_DSA)),
        'cache_idx_k': normal(ks[7], (DEPTH, n_pool, PAGE_SIZE, DH_IDX)),
        'page_table': page_table,
        'g_mix': gain(ks[8], (DEPTH, D_MODEL)),
        'w_in': normal(ks[9], (DEPTH, D_MODEL, P_TOTAL), D_MODEL ** -0.5),
        'g_q_diff': gain(ks[10], (DEPTH, DH_DIFF)),
        'g_k_diff': gain(ks[11], (DEPTH, DH_DIFF)),
        'lam_q1': normal(ks[12], (DEPTH, DH_DIFF), 0.1),
        'lam_k1': normal(ks[13], (DEPTH, DH_DIFF), 0.1),
        'lam_q2': normal(ks[14], (DEPTH, DH_DIFF), 0.1),
        'lam_k2': normal(ks[15], (DEPTH, DH_DIFF), 0.1),
        'g_subln': gain(ks[16], (DEPTH, 2 * DH_DIFF)),
        'g_q_dsa': gain(ks[17], (DEPTH, DH_DSA)),
        'g_k_dsa': gain(ks[18], (DEPTH, DH_DSA)),
        'w_o_diff': normal(ks[19], (DEPTH, W_DIFF, D_MODEL), W_DIFF ** -0.5),
        'w_o_dsa': normal(ks[20], (DEPTH, W_DSA, D_MODEL), W_DSA ** -0.5),
        'w_out': normal(ks[21], (DEPTH, D_MODEL, D_MODEL), D_MODEL ** -0.5),
        'g_ffn': gain(ks[22], (DEPTH, D_MODEL)),
        'w_group': normal(ks[23], (DEPTH, D_MODEL, N_GROUPS), D_MODEL ** -0.5),
        'w_router': normal(ks[24], (DEPTH, D_MODEL, N_EXPERTS), D_MODEL ** -0.5),
        'w_gate': normal(ks[25], (DEPTH, N_EXPERTS, D_MODEL, D_EXPERT), D_MODEL ** -0.5),
        'w_up': normal(ks[26], (DEPTH, N_EXPERTS, D_MODEL, D_EXPERT), D_MODEL ** -0.5),
        'w_down': normal(ks[27], (DEPTH, N_EXPERTS, D_EXPERT, D_MODEL), D_EXPERT ** -0.5),
    }


def reference(x_prompt, x_sample, cache_diff_k, cache_diff_v, cache_dsa_k, cache_dsa_v, cache_idx_k,
              page_table, g_mix, w_in, g_q_diff, g_k_diff, lam_q1, lam_k1, lam_q2, lam_k2, g_subln,
              g_q_dsa, g_k_dsa, w_o_diff, w_o_dsa, w_out, g_ffn, w_group, w_router, w_gate, w_up, w_down):
    seq = x_prompt.shape[1]
    dec_seq = x_sample.shape[1]
    past_len = page_table.shape[1] * cache_diff_k.shape[2]
    pos_p = jnp.arange(seq, dtype=jnp.int32)
    pos_s = past_len + jnp.arange(dec_seq, dtype=jnp.int32)
    k_sel_p = min(TOPK_MAX, seq // 4)
    k_sel_s = min(TOPK_MAX, (past_len + dec_seq) // 4)
    y_p, y_s = x_prompt, x_sample
    rows_p, rows_s = [], []
    for l in range(DEPTH):
        lam_init = 0.8 - 0.6 * math.exp(-0.3 * l)
        mix_params = (g_mix[l], w_in[l], g_q_diff[l], g_k_diff[l], lam_q1[l], lam_k1[l], lam_q2[l], lam_k2[l],
                      g_subln[l], g_q_dsa[l], g_k_dsa[l], w_o_diff[l], w_o_dsa[l], w_out[l])
        moe_params = (g_ffn[l], w_group[l], w_router[l], w_gate[l], w_up[l], w_down[l])
        past = (gather_pages(cache_diff_k[l], page_table), gather_pages(cache_diff_v[l], page_table),
                gather_pages(cache_dsa_k[l], page_table), gather_pages(cache_dsa_v[l], page_table),
                gather_pages(cache_idx_k[l], page_table))
        y_p, r_p = token_mixing(y_p, pos_p, None, lam_init, k_sel_p, Q_BLOCK, DSA_Q_BLOCK, *mix_params)
        y_p = y_p + hier_moe(y_p, *moe_params)
        y_s, r_s = token_mixing(y_s, pos_s, past, lam_init, k_sel_s, None, None, *mix_params)
        y_s = y_s + hier_moe(y_s, *moe_params)
        rows_p.append(r_p)
        rows_s.append(r_s)
    new_diff_k_p = jnp.stack([r[0] for r in rows_p])
    new_diff_v_p = jnp.stack([r[1] for r in rows_p])
    new_dsa_k_p = jnp.stack([r[2] for r in rows_p])
    new_dsa_v_p = jnp.stack([r[3] for r in rows_p])
    new_idx_k_p = jnp.stack([r[4] for r in rows_p])
    new_diff_k_s = jnp.stack([r[0] for r in rows_s])
    new_diff_v_s = jnp.stack([r[1] for r in rows_s])
    new_dsa_k_s = jnp.stack([r[2] for r in rows_s])
    new_dsa_v_s = jnp.stack([r[3] for r in rows_s])
    new_idx_k_s = jnp.stack([r[4] for r in rows_s])
    return (y_p, y_s, new_diff_k_p, new_diff_v_p, new_dsa_k_p, new_dsa_v_p, new_idx_k_p,
            new_diff_k_s, new_diff_v_s, new_dsa_k_s, new_dsa_v_s, new_idx_k_s)
```

```python
import functools
import math

import jax
import jax.numpy as jnp
from jax import lax
from jax.experimental import pallas as pl
from jax.experimental.pallas import tpu as pltpu

F32 = jnp.float32
BF16 = jnp.bfloat16
I32 = jnp.int32

EPS = 1e-6
ROPE_THETA = 10000.0
TOPK_MAX = 256
TOP_K_EXPERTS = 2
LANES = 128
HEAD_DIM = 64
MASK_NEG = -1e30
INT_MIN = -(2 ** 31)
VMEM_LIMIT = 56 * 1024 * 1024


def _params(*sem):
    return pltpu.CompilerParams(dimension_semantics=sem, vmem_limit_bytes=VMEM_LIMIT)


def _dot(a, b):
    return jnp.dot(a, b, preferred_element_type=F32)


def _dot_nt(a, b):
    return lax.dot_general(a, b, (((1,), (1,)), ((), ())), preferred_element_type=F32)


def _split_bf16(a):
    hi = a.astype(BF16)
    lo = (a - hi.astype(F32)).astype(BF16)
    return hi, lo


def _inproj_kernel(x_ref, cos_ref, sin_ref, gmix_ref, wqkv_ref, wsm_ref, wgate_ref,
                   gqd_ref, gkd_ref, gqs_ref, gks_ref, bd_ref,
                   qd_ref, kdf_ref, kdb_ref, vdf_ref, vdb_ref,
                   qs_ref, ksf_ref, ksb_ref, vsf_ref, vsb_ref,
                   qi_ref, kif_ref, ki2_ref, wi_ref, gate_ref, *, w_diff, w_dsa, w_idx, h_idx, qk_scale):
    x = x_ref[...]
    tm = x.shape[0]
    ms = jnp.mean(x * x, axis=-1, keepdims=True)
    h = ((x * lax.rsqrt(ms + EPS)) * gmix_ref[...]).astype(BF16)

    cos = cos_ref[...]
    sin = sin_ref[...]
    lane = lax.broadcasted_iota(I32, (tm, LANES), 1)
    low_half = (lane & (HEAD_DIM // 2)) == 0

    def rope(y):
        partner = jnp.where(low_half, pltpu.roll(y, LANES - HEAD_DIM // 2, 1), pltpu.roll(y, HEAD_DIM // 2, 1))
        return y * cos + partner * sin

    bd = bd_ref[...]

    def head_norm(p, g_ref):
        hi, lo = _split_bf16(p * p)
        width = bd.shape[0]
        parts = []
        for c in range(0, p.shape[1], width):
            parts.append(_dot(hi[:, c:c + width], bd) + _dot(lo[:, c:c + width], bd))
        ss = jnp.concatenate(parts, axis=1) if len(parts) > 1 else parts[0]
        return (p * lax.rsqrt(ss * (1.0 / HEAD_DIM) + EPS)) * g_ref[...]

    def seg(c0, width):
        return _dot(h, wqkv_ref[:, c0:c0 + width])

    def rope_all(y, scale=None):
        outs = []
        for c in range(0, y.shape[1], LANES):
            r = rope(y[:, c:c + LANES])
            outs.append(r if scale is None else r * scale)
        return jnp.concatenate(outs, axis=1) if len(outs) > 1 else outs[0]

    c0 = 0
    qd = rope_all(head_norm(seg(c0, w_diff), gqd_ref), qk_scale)
    qd_ref[...] = qd.astype(BF16)
    c0 += w_diff
    kd = rope_all(head_norm(seg(c0, w_diff), gkd_ref))
    kdf_ref[...] = kd
    kdb_ref[...] = kd.astype(BF16)
    c0 += w_diff
    vd = seg(c0, w_diff)
    vdf_ref[...] = vd
    vdb_ref[...] = vd.astype(BF16)
    c0 += w_diff
    qs = rope_all(head_norm(seg(c0, w_dsa), gqs_ref), qk_scale)
    qs_ref[...] = qs.astype(BF16)
    c0 += w_dsa
    ks = rope_all(head_norm(seg(c0, w_dsa), gks_ref))
    ksf_ref[...] = ks
    ksb_ref[...] = ks.astype(BF16)
    c0 += w_dsa
    vs = seg(c0, w_dsa)
    vsf_ref[...] = vs
    vsb_ref[...] = vs.astype(BF16)
    c0 += w_dsa
    qi = rope_all(seg(c0, w_idx), qk_scale)
    qi_ref[...] = qi.astype(BF16)

    sm = _dot(h, wsm_ref[...])
    ki = rope(sm)
    kif_ref[...] = ki[:, :HEAD_DIM]
    ki2_ref[...] = jnp.where(lane < HEAD_DIM, ki, pltpu.roll(ki, HEAD_DIM, 1)).astype(BF16)
    wi_ref[...] = sm * (h_idx ** -0.5)

    g = _dot(h, wgate_ref[...])
    gate_ref[...] = (1.0 / (1.0 + jnp.exp(-g))).astype(BF16)


def _inproj(x, cos_t, sin_t, pos_blocks, tm, prm):
    n, d = x.shape
    w_diff, w_dsa, w_idx, h_idx = prm["w_diff"], prm["w_dsa"], prm["w_idx"], prm["h_idx"]
    grid = (n // tm,)
    row = lambda i: (i, 0)
    const = lambda i: (0, 0)
    pos = lambda i: (i % pos_blocks, 0)
    full = lambda a: pl.BlockSpec(a.shape, const)
    in_specs = [pl.BlockSpec((tm, d), row), pl.BlockSpec((tm, LANES), pos), pl.BlockSpec((tm, LANES), pos),
                full(prm["g_mix"]), full(prm["w_qkv"]), full(prm["w_small"]), full(prm["w_gate"]),
                full(prm["g_qd"]), full(prm["g_kd"]), full(prm["g_qs"]), full(prm["g_ks"]), full(prm["bd"])]
    wg = prm["w_gate"].shape[1]
    outs = [
        (w_diff, BF16), (w_diff, F32), (w_diff, BF16), (w_diff, F32), (w_diff, BF16),
        (w_dsa, BF16), (w_dsa, F32), (w_dsa, BF16), (w_dsa, F32), (w_dsa, BF16),
        (w_idx, BF16), (HEAD_DIM, F32), (LANES, BF16), (LANES, F32), (wg, BF16)]
    out_shape = [jax.ShapeDtypeStruct((n, w), dt) for w, dt in outs]
    out_specs = [pl.BlockSpec((tm, w), row) for w, _ in outs]
    kern = functools.partial(_inproj_kernel, w_diff=w_diff, w_dsa=w_dsa, w_idx=w_idx, h_idx=h_idx,
                             qk_scale=HEAD_DIM ** -0.5)
    return pl.pallas_call(kern, grid=grid, in_specs=in_specs, out_specs=out_specs, out_shape=out_shape,
                          compiler_params=_params("parallel"), name="inproj")(
        x, cos_t, sin_t, prm["g_mix"], prm["w_qkv"], prm["w_small"], prm["w_gate"],
        prm["g_qd"], prm["g_kd"], prm["g_qs"], prm["g_ks"], prm["bd"])


def _stack_halves(q):
    lane = lax.broadcasted_iota(I32, q.shape, 1)
    zero = jnp.zeros_like(q)
    return jnp.concatenate([jnp.where(lane < HEAD_DIM, q, zero), jnp.where(lane >= HEAD_DIM, q, zero)], axis=0)


def _online_update(s, v, m_sc, l_sc, acc_sc):
    m_prev = m_sc[...]
    m_new = jnp.maximum(m_prev, jnp.max(s, axis=-1, keepdims=True))
    alpha = jnp.exp(m_prev - m_new)
    p = jnp.exp(s - m_new)
    l_sc[...] = alpha * l_sc[...] + jnp.sum(p, axis=-1, keepdims=True)
    acc_sc[...] = alpha * acc_sc[...] + _dot(p.astype(BF16), v)
    m_sc[...] = m_new


def _lambda(lam_ref, lam_init):
    a = jnp.sum(lam_ref[0:1, :] * lam_ref[1:2, :], axis=-1, keepdims=True)
    b = jnp.sum(lam_ref[2:3, :] * lam_ref[3:4, :], axis=-1, keepdims=True)
    return jnp.exp(a) - jnp.exp(b) + lam_init


def _diff_attn_kernel(lam_ref, gsub_ref, q_ref, k_ref, v_ref, o_ref, m_sc, l_sc, acc_sc, *, tq, lam_init):
    qi = pl.program_id(2)
    q2 = _stack_halves(q_ref[...])
    m_sc[...] = jnp.full(m_sc.shape, -jnp.inf, F32)
    l_sc[...] = jnp.zeros(l_sc.shape, F32)
    acc_sc[...] = jnp.zeros(acc_sc.shape, F32)

    def step(j, masked):
        k = k_ref[pl.ds(pl.multiple_of(j * tq, tq), tq), :]
        v = v_ref[pl.ds(pl.multiple_of(j * tq, tq), tq), :]
        s = _dot_nt(q2, k)
        if masked:
            r = lax.broadcasted_iota(I32, s.shape, 0)
            r = jnp.where(r >= tq, r - tq, r)
            c = lax.broadcasted_iota(I32, s.shape, 1)
            s = jnp.where(c <= r, s, -jnp.inf)
        _online_update(s, v, m_sc, l_sc, acc_sc)

    def body(j, carry):
        step(j, False)
        return carry

    lax.fori_loop(0, qi, body, 0)
    step(qi, True)

    a = acc_sc[...] * (1.0 / l_sc[...])
    o = a[:tq] - _lambda(lam_ref, lam_init) * a[tq:]
    ms = jnp.mean(o * o, axis=-1, keepdims=True)
    on = ((o * lax.rsqrt(ms + EPS)) * gsub_ref[...]) * (1.0 - lam_init)
    o_ref[...] = on.astype(BF16)


def _diff_attn(q, k, v, lam_vecs, g_sub, lam_init, tq):
    b, t, w = q.shape
    nh = w // LANES
    grid = (b, nh, t // tq)
    qmap = lambda bi, hi, qi: (bi, qi, hi)
    kvmap = lambda bi, hi, qi: (bi, 0, hi)
    const = lambda bi, hi, qi: (0, 0)
    kern = functools.partial(_diff_attn_kernel, tq=tq, lam_init=lam_init)
    return pl.pallas_call(
        kern, grid=grid,
        in_specs=[pl.BlockSpec(lam_vecs.shape, const), pl.BlockSpec(g_sub.shape, const),
                  pl.BlockSpec((None, tq, LANES), qmap),
                  pl.BlockSpec((None, t, LANES), kvmap), pl.BlockSpec((None, t, LANES), kvmap)],
        out_specs=pl.BlockSpec((None, tq, LANES), qmap),
        out_shape=jax.ShapeDtypeStruct((b, t, w), BF16),
        scratch_shapes=[pltpu.VMEM((2 * tq, 1), F32), pltpu.VMEM((2 * tq, 1), F32), pltpu.VMEM((2 * tq, LANES), F32)],
        compiler_params=_params("parallel", "parallel", "arbitrary"), name="diff_attn")(lam_vecs, g_sub, q, k, v)


def _score_keys(score):
    score = jnp.where(score == 0.0, 0.0, score)
    b = lax.bitcast_convert_type(score, I32)
    return b ^ (lax.shift_right_arithmetic(b, 31) & 0x7FFFFFFF)


def _lane_fold(a):
    out = a[:, :LANES]
    for c in range(LANES, a.shape[1], LANES):
        out = out + a[:, c:c + LANES]
    return out


def _select_bias(key_sc, bias_sc, n_c, k_sel, row_pos, col0):
    _, r, kc = key_sc.shape

    def count(pred):
        def body(c, a):
            return a + _lane_fold(jnp.where(pred(key_sc[c]), 1.0, 0.0))
        a = lax.fori_loop(0, n_c, body, jnp.zeros((r, LANES), F32))
        return jnp.sum(a, axis=-1, keepdims=True)

    def bit_body(i, t):
        cand = t + lax.shift_left(jnp.int32(1), 31 - i)
        cnt = count(lambda ik: ik >= cand)
        return jnp.where(cnt >= k_sel, cand, t)

    t = lax.fori_loop(0, 32, bit_body, jnp.full((r, 1), INT_MIN, I32))
    need = k_sel - count(lambda ik: ik > t)

    ri = lax.broadcasted_iota(I32, (kc, kc), 0)
    ci = lax.broadcasted_iota(I32, (kc, kc), 1)
    tri = jnp.where(ri <= ci, 1.0, 0.0).astype(BF16)
    col = lax.broadcasted_iota(I32, (r, kc), 1)

    def bias_body(c, run):
        ik = key_sc[c]
        eq = ik == t
        pre = _dot(jnp.where(eq, 1.0, 0.0).astype(BF16), tri)
        sel = (ik > t) | (eq & ((pre + run) <= need))
        causal = (col + (col0 + c * kc)) <= row_pos
        bias_sc[c] = jnp.where(sel & causal, 0.0, MASK_NEG)
        return run + pre[:, kc - 1:kc]

    lax.fori_loop(0, n_c, bias_body, jnp.zeros((r, 1), F32))


def _dsa_attn_kernel(qi_ref, ki2_ref, wi_ref, qs_ref, ks_ref, vs_ref, o_ref,
                     qstack_sc, key_sc, bias_sc, m_sc, l_sc, acc_sc, *, tq, kc, k_sel, h_idx):
    qb = pl.program_id(1)
    q0 = qb * tq
    n_c = q0 // kc + 1
    row_pos = q0 + lax.broadcasted_iota(I32, (tq, 1), 0)

    for hh in range(h_idx):
        pair = qi_ref[:, (hh // 2) * LANES:(hh // 2 + 1) * LANES]
        lane = lax.broadcasted_iota(I32, pair.shape, 1)
        keep = (lane < HEAD_DIM) if hh % 2 == 0 else (lane >= HEAD_DIM)
        qstack_sc[hh * tq:(hh + 1) * tq, :] = jnp.where(keep, pair, jnp.zeros_like(pair))
    wi = wi_ref[...]

    def score_body(c, carry):
        kk = ki2_ref[pl.ds(pl.multiple_of(c * kc, kc), kc), :]
        d = _dot_nt(qstack_sc[...], kk)
        acc = jnp.zeros((tq, kc), F32)
        for hh in range(h_idx):
            acc = acc + jnp.maximum(d[hh * tq:(hh + 1) * tq], 0.0) * wi[:, HEAD_DIM + hh:HEAD_DIM + hh + 1]
        col = c * kc + lax.broadcasted_iota(I32, (tq, kc), 1)
        key_sc[c] = _score_keys(jnp.where(col <= row_pos, acc, -jnp.inf))
        return carry

    lax.fori_loop(0, n_c, score_body, 0)
    _select_bias(key_sc, bias_sc, n_c, k_sel, row_pos, 0)

    for hp in range(qs_ref.shape[1] // LANES):
        q2 = _stack_halves(qs_ref[:, hp * LANES:(hp + 1) * LANES])
        m_sc[...] = jnp.full(m_sc.shape, MASK_NEG, F32)
        l_sc[...] = jnp.zeros(l_sc.shape, F32)
        acc_sc[...] = jnp.zeros(acc_sc.shape, F32)

        def attn_body(c, carry):
            rows = pl.ds(pl.multiple_of(c * kc, kc), kc)
            k = ks_ref[rows, hp * LANES:(hp + 1) * LANES]
            v = vs_ref[rows, hp * LANES:(hp + 1) * LANES]
            bias = bias_sc[c]
            s = _dot_nt(q2, k) + jnp.concatenate([bias, bias], axis=0)
            _online_update(s, v, m_sc, l_sc, acc_sc)
            return carry

        lax.fori_loop(0, n_c, attn_body, 0)
        a = acc_sc[...] * (1.0 / l_sc[...])
        lane = lax.broadcasted_iota(I32, (tq, LANES), 1)
        o_ref[:, hp * LANES:(hp + 1) * LANES] = jnp.where(lane < HEAD_DIM, a[:tq], a[tq:]).astype(BF16)


def _dsa_attn(qi, ki2, wi, qs, ks, vs, k_sel, h_idx, tq, kc):
    b, t, w = qs.shape
    nc = t // kc
    grid = (b, t // tq)
    qmap = lambda bi, qb: (bi, qb, 0)
    kvmap = lambda bi, qb: (bi, 0, 0)
    kern = functools.partial(_dsa_attn_kernel, tq=tq, kc=kc, k_sel=k_sel, h_idx=h_idx)
    return pl.pallas_call(
        kern, grid=grid,
        in_specs=[pl.BlockSpec((None, tq, qi.shape[2]), qmap), pl.BlockSpec((None, t, LANES), kvmap),
                  pl.BlockSpec((None, tq, LANES), qmap), pl.BlockSpec((None, tq, w), qmap),
                  pl.BlockSpec((None, t, w), kvmap), pl.BlockSpec((None, t, w), kvmap)],
        out_specs=pl.BlockSpec((None, tq, w), qmap),
        out_shape=jax.ShapeDtypeStruct((b, t, w), BF16),
        scratch_shapes=[pltpu.VMEM((h_idx * tq, LANES), BF16), pltpu.VMEM((nc, tq, kc), I32),
                        pltpu.VMEM((nc, tq, kc), F32), pltpu.VMEM((2 * tq, 1), F32), pltpu.VMEM((2 * tq, 1), F32),
                        pltpu.VMEM((2 * tq, LANES), F32)],
        compiler_params=_params("parallel", "arbitrary"), name="dsa_attn")(qi, ki2, wi, qs, ks, vs)


def _page_specs(n_pages, block):
    def spec(j):
        return pl.BlockSpec((None,) + block, lambda b, pt, j=j: (pt[b, j],) + (0,) * len(block))
    return [spec(j) for j in range(n_pages)]


def _idx_score_kernel(pt_ref, q_ref, w_ref, knew_ref, *refs, n_pages, page, pad):
    page_refs, o_ref = refs[:n_pages], refs[n_pages]
    q = q_ref[...]
    w = w_ref[...]
    for j in range(n_pages):
        d = _dot_nt(q, page_refs[j][...].astype(BF16))
        o_ref[:, j * page:(j + 1) * page] = jnp.sum(jnp.maximum(d, 0.0) * w, axis=0, keepdims=True)
    d_new = jnp.sum(q.astype(F32) * knew_ref[...].astype(F32), axis=-1, keepdims=True)
    s_new = jnp.sum(jnp.maximum(d_new, 0.0) * w, axis=0, keepdims=True)
    lane = lax.broadcasted_iota(I32, (1, pad), 1)
    o_ref[:, n_pages * page:] = jnp.where(lane == 0, s_new, -jnp.inf)


def _idx_scores(page_table, q_i, w_i, k_new, cache_idx, pad):
    db, n_pages = page_table.shape
    _, page, dh = cache_idx.shape
    h_idx = q_i.shape[1]
    tk = n_pages * page + pad
    seq = lambda b, pt: (b, 0, 0)
    kern = functools.partial(_idx_score_kernel, n_pages=n_pages, page=page, pad=pad)
    grid_spec = pltpu.PrefetchScalarGridSpec(
        num_scalar_prefetch=1, grid=(db,),
        in_specs=[pl.BlockSpec((None, h_idx, dh), seq), pl.BlockSpec((None, h_idx, 1), seq),
                  pl.BlockSpec((None, 1, dh), seq)] + _page_specs(n_pages, (page, dh)),
        out_specs=pl.BlockSpec((None, 1, tk), seq))
    out = pl.pallas_call(kern, grid_spec=grid_spec, out_shape=jax.ShapeDtypeStruct((db, 1, tk), F32),
                         compiler_params=_params("parallel"), name="idx_scores")(
        page_table, q_i, w_i, k_new, *([cache_idx] * n_pages))
    return out.reshape(db, tk)


def _select_kernel(s_ref, o_ref, key_sc, bias_sc, *, kc, k_sel, q_pos):
    r = s_ref.shape[0]
    nc = s_ref.shape[1] // kc
    for c in range(nc):
        key_sc[c] = _score_keys(s_ref[:, c * kc:(c + 1) * kc])
    row_pos = jnp.full((r, 1), q_pos, I32)
    _select_bias(key_sc, bias_sc, nc, k_sel, row_pos, 0)
    for c in range(nc):
        o_ref[:, c * kc:(c + 1) * kc] = bias_sc[c]


def _select(scores, k_sel, q_pos, kc):
    r, tk = scores.shape
    nc = tk // kc
    kern = functools.partial(_select_kernel, kc=kc, k_sel=k_sel, q_pos=q_pos)
    return pl.pallas_call(
        kern, grid=(1,), in_specs=[pl.BlockSpec((r, tk), lambda i: (0, 0))],
        out_specs=pl.BlockSpec((r, tk), lambda i: (0, 0)), out_shape=jax.ShapeDtypeStruct((r, tk), F32),
        scratch_shapes=[pltpu.VMEM((nc, r, kc), I32), pltpu.VMEM((nc, r, kc), F32)],
        compiler_params=_params("arbitrary"), name="select")(scores)


def _block_diag_rows(q, width):
    h = q.shape[0]
    t = jnp.concatenate([q] * (width // HEAD_DIM), axis=1)
    row = lax.broadcasted_iota(I32, t.shape, 0)
    col = lax.broadcasted_iota(I32, t.shape, 1)
    return jnp.where((col >= row * HEAD_DIM) & (col < (row + 1) * HEAD_DIM), t, jnp.zeros_like(t))


def _decode_attn_kernel(pt_ref, lam_ref, gsub_ref, qd_ref, kdn_ref, vdn_ref, qs_ref, ksn_ref, vsn_ref, bias_ref,
                        *refs, n_pages, page, lam_init):
    kd_pages = refs[:n_pages]
    vd_pages = refs[n_pages:2 * n_pages]
    ks_pages = refs[2 * n_pages:3 * n_pages]
    vs_pages = refs[3 * n_pages:4 * n_pages]
    od_ref, os_ref = refs[4 * n_pages], refs[4 * n_pages + 1]

    def attend(q_ref, knew_ref, vnew_ref, k_pages, v_pages, bias):
        q = q_ref[...]
        width = k_pages[0].shape[1]
        qbd = _block_diag_rows(q, width)
        s = [_dot_nt(qbd, k_pages[j][...].astype(BF16)) for j in range(n_pages)]
        s_new = jnp.sum(q.astype(F32) * knew_ref[...].astype(F32), axis=-1, keepdims=True)
        if bias is not None:
            s = [s[j] + bias[:, j * page:(j + 1) * page] for j in range(n_pages)]
            s_new = s_new + bias[:, n_pages * page:n_pages * page + 1]
        m = s_new
        for j in range(n_pages):
            m = jnp.maximum(m, jnp.max(s[j], axis=-1, keepdims=True))
        p_new = jnp.exp(s_new - m)
        l = p_new
        acc = p_new * vnew_ref[...].astype(F32)
        for j in range(n_pages):
            p = jnp.exp(s[j] - m)
            l = l + jnp.sum(p, axis=-1, keepdims=True)
            acc = acc + _dot(p.astype(BF16), v_pages[j][...].astype(BF16))
        return acc * (1.0 / l)

    a = attend(qd_ref, kdn_ref, vdn_ref, kd_pages, vd_pages, None)
    row = lax.broadcasted_iota(I32, a.shape, 0)
    col = lax.broadcasted_iota(I32, a.shape, 1)
    own = (col // LANES) == (row // 2)
    coef = jnp.where(row % 2 == 0, 1.0, -_lambda(lam_ref, lam_init))
    o = jnp.sum(jnp.where(own, a * coef, 0.0), axis=0, keepdims=True)
    parts = []
    for c in range(0, o.shape[1], LANES):
        oc = o[:, c:c + LANES]
        ms = jnp.mean(oc * oc, axis=-1, keepdims=True)
        parts.append(((oc * lax.rsqrt(ms + EPS)) * gsub_ref[...]) * (1.0 - lam_init))
    od_ref[...] = jnp.concatenate(parts, axis=1).astype(BF16)

    a = attend(qs_ref, ksn_ref, vsn_ref, ks_pages, vs_pages, bias_ref[...])
    row = lax.broadcasted_iota(I32, a.shape, 0)
    col = lax.broadcasted_iota(I32, a.shape, 1)
    os_ref[...] = jnp.sum(jnp.where((col // HEAD_DIM) == row, a, 0.0), axis=0, keepdims=True).astype(BF16)


def _decode_attn(page_table, lam_vecs, g_sub, qd, kdn, vdn, qs, ksn, vsn, bias, caches, lam_init):
    db, n_pages = page_table.shape
    c_kd, c_vd, c_ks, c_vs = caches
    page, w_diff = c_kd.shape[1:]
    w_dsa = c_ks.shape[2]
    seq = lambda b, pt: (b, 0, 0)
    const = lambda b, pt: (0, 0)
    in_specs = [pl.BlockSpec(lam_vecs.shape, const), pl.BlockSpec(g_sub.shape, const),
                pl.BlockSpec((None,) + qd.shape[1:], seq), pl.BlockSpec((None,) + kdn.shape[1:], seq),
                pl.BlockSpec((None,) + vdn.shape[1:], seq),
                pl.BlockSpec((None,) + qs.shape[1:], seq), pl.BlockSpec((None,) + ksn.shape[1:], seq),
                pl.BlockSpec((None,) + vsn.shape[1:], seq), pl.BlockSpec((None,) + bias.shape[1:], seq)]
    for cache in caches:
        in_specs += _page_specs(n_pages, tuple(cache.shape[1:]))
    kern = functools.partial(_decode_attn_kernel, n_pages=n_pages, page=page, lam_init=lam_init)
    grid_spec = pltpu.PrefetchScalarGridSpec(
        num_scalar_prefetch=1, grid=(db,), in_specs=in_specs,
        out_specs=[pl.BlockSpec((None, 1, w_diff), seq), pl.BlockSpec((None, 1, w_dsa), seq)])
    pages = []
    for cache in caches:
        pages += [cache] * n_pages
    od, os_ = pl.pallas_call(
        kern, grid_spec=grid_spec,
        out_shape=[jax.ShapeDtypeStruct((db, 1, w_diff), BF16), jax.ShapeDtypeStruct((db, 1, w_dsa), BF16)],
        compiler_params=_params("parallel"), name="decode_attn")(
        page_table, lam_vecs, g_sub, qd, kdn, vdn, qs, ksn, vsn, bias, *pages)
    return od.reshape(db, w_diff), os_.reshape(db, w_dsa)


def _outproj_kernel(x_ref, od_ref, os_ref, gate_ref, wod_ref, wos_ref, wout_ref, gffn_ref, wrh_ref, wrl_ref,
                    x1_ref, h2_ref, comb_ref, *, n_groups, n_experts):
    d = x_ref.shape[1]
    yd = _dot(od_ref[...], wod_ref[...])
    ys = _dot(os_ref[...], wos_ref[...])
    merged = gate_ref[:, :d].astype(F32) * yd + gate_ref[:, d:].astype(F32) * ys
    x1 = x_ref[...] + _dot(merged.astype(BF16), wout_ref[...])
    x1_ref[...] = x1

    ms = jnp.mean(x1 * x1, axis=-1, keepdims=True)
    h2 = (x1 * lax.rsqrt(ms + EPS)) * gffn_ref[...]
    h2_ref[...] = h2.astype(BF16)

    hi, lo = _split_bf16(h2)
    lg = _dot(hi, wrh_ref[...]) + (_dot(hi, wrl_ref[...]) + _dot(lo, wrh_ref[...]))
    lane = lax.broadcasted_iota(I32, lg.shape, 1)
    lanef = lane.astype(F32)
    big = float(LANES)
    epg = n_experts // n_groups

    gmask = (lane >= n_experts) & (lane < n_experts + n_groups)
    gl = jnp.where(gmask, lg, -jnp.inf)
    gmax = jnp.max(gl, axis=-1, keepdims=True)
    gsel = jnp.min(jnp.where(gl == gmax, lanef, big), axis=-1, keepdims=True) - n_experts
    g_w = 1.0 / jnp.sum(jnp.exp(gl - gmax), axis=-1, keepdims=True)

    lo_lane = gsel * epg
    emask = (lanef >= lo_lane) & (lanef < lo_lane + epg)
    el = jnp.where(emask, lg, -jnp.inf)
    v1 = jnp.max(el, axis=-1, keepdims=True)
    i1 = jnp.min(jnp.where(el == v1, lanef, big), axis=-1, keepdims=True)
    el2 = jnp.where(lanef == i1, -jnp.inf, el)
    v2 = jnp.max(el2, axis=-1, keepdims=True)
    i2 = jnp.min(jnp.where(el2 == v2, lanef, big), axis=-1, keepdims=True)
    e21 = jnp.exp(v2 - v1)
    den = 1.0 / (1.0 + e21)
    comb_ref[...] = (jnp.where(lanef == i1, den * g_w, 0.0) + jnp.where(lanef == i2, (e21 * den) * g_w, 0.0))


def _outproj(x, od, os_, gates, prm, tm):
    n, d = x.shape
    row = lambda i: (i, 0)
    const = lambda i: (0, 0)
    full = lambda a: pl.BlockSpec(a.shape, const)
    kern = functools.partial(_outproj_kernel, n_groups=prm["n_groups"], n_experts=prm["n_experts"])
    return pl.pallas_call(
        kern, grid=(n // tm,),
        in_specs=[pl.BlockSpec((tm, d), row), pl.BlockSpec((tm, od.shape[1]), row), pl.BlockSpec((tm, os_.shape[1]), row),
                  pl.BlockSpec((tm, gates.shape[1]), row), full(prm["w_od"]), full(prm["w_os"]), full(prm["w_out"]),
                  full(prm["g_ffn"]), full(prm["w_rt_hi"]), full(prm["w_rt_lo"])],
        out_specs=[pl.BlockSpec((tm, d), row), pl.BlockSpec((tm, d), row), pl.BlockSpec((tm, LANES), row)],
        out_shape=[jax.ShapeDtypeStruct((n, d), F32), jax.ShapeDtypeStruct((n, d), BF16),
                   jax.ShapeDtypeStruct((n, LANES), F32)],
        compiler_params=_params("parallel"), name="outproj")(
        x, od, os_, gates, prm["w_od"], prm["w_os"], prm["w_out"], prm["g_ffn"], prm["w_rt_hi"], prm["w_rt_lo"])


def _moe_kernel(x1_ref, h2_ref, comb_ref, wg_ref, wu_ref, wd_ref, y_ref, acc_sc):
    e = pl.program_id(1)

    @pl.when(e == 0)
    def _():
        acc_sc[...] = jnp.zeros(acc_sc.shape, F32)

    h = h2_ref[...]
    a = _dot(h, wg_ref[...])
    u = _dot(h, wu_ref[...])
    act = (a * (1.0 / (1.0 + jnp.exp(-a)))) * u
    o = _dot(act.astype(BF16), wd_ref[...])
    comb = comb_ref[...]
    lane = lax.broadcasted_iota(I32, comb.shape, 1)
    c = jnp.sum(jnp.where(lane == e, comb, 0.0), axis=-1, keepdims=True)
    acc_sc[...] += o * c

    @pl.when(e == pl.num_programs(1) - 1)
    def _():
        y_ref[...] = x1_ref[...] + acc_sc[...]


def _moe(x1, h2, comb, w_gate, w_up, w_down, tm):
    n, d = x1.shape
    ne, _, de = w_gate.shape
    row = lambda i, e: (i, 0)
    return pl.pallas_call(
        _moe_kernel, grid=(n // tm, ne),
        in_specs=[pl.BlockSpec((tm, d), row), pl.BlockSpec((tm, d), row), pl.BlockSpec((tm, LANES), row),
                  pl.BlockSpec((None, d, de), lambda i, e: (e, 0, 0)), pl.BlockSpec((None, d, de), lambda i, e: (e, 0, 0)),
                  pl.BlockSpec((None, de, d), lambda i, e: (e, 0, 0))],
        out_specs=pl.BlockSpec((tm, d), row), out_shape=jax.ShapeDtypeStruct((n, d), F32),
        scratch_shapes=[pltpu.VMEM((tm, d), F32)],
        compiler_params=_params("parallel", "arbitrary"), name="moe")(x1, h2, comb, w_gate, w_up, w_down)


def _rope_tables(pos):
    half = HEAD_DIM // 2
    inv_freq = jnp.power(ROPE_THETA, -jnp.arange(half, dtype=F32) / half)
    ang = pos.astype(F32)[:, None] * inv_freq[None, :]
    cos, sin = jnp.cos(ang), jnp.sin(ang)
    reps = LANES // HEAD_DIM
    return jnp.tile(cos, (1, 2 * reps)), jnp.tile(jnp.concatenate([-sin, sin], axis=1), (1, reps))


def _tile_rows(n, target):
    t = min(n, target)
    while n % t:
        t //= 2
    return t


def kernel(x_prompt, x_sample, cache_diff_k, cache_diff_v, cache_dsa_k, cache_dsa_v, cache_idx_k, page_table, g_mix, w_in, g_q_diff, g_k_diff, lam_q1, lam_k1, lam_q2, lam_k2, g_subln, g_q_dsa, g_k_dsa, w_o_diff, w_o_dsa, w_out, g_ffn, w_group, w_router, w_gate, w_up, w_down):
    bsz, seq, d_model = x_prompt.shape
    db, dec_seq, _ = x_sample.shape
    depth, n_pool, page, h_diff, _, dh = cache_diff_k.shape
    h_dsa = cache_dsa_k.shape[3]
    n_pages = page_table.shape[1]
    past_len = n_pages * page
    n_groups = w_group.shape[2]
    n_experts = w_router.shape[2]
    w_diff = h_diff * 2 * dh
    w_dsa = h_dsa * dh
    h_idx = w_in.shape[2] - (3 * w_diff + 3 * w_dsa + dh + 2 * d_model)
    h_idx = h_idx // (dh + 1)
    w_idx = h_idx * dh
    assert dh == HEAD_DIM and cache_dsa_k.shape[4] == HEAD_DIM and cache_idx_k.shape[3] == HEAD_DIM
    assert dec_seq == 1 and depth == 1
    assert 3 * w_diff + 3 * w_dsa + w_idx + dh + h_idx + 2 * d_model == w_in.shape[2]
    k_sel_p = min(TOPK_MAX, seq // 4)
    k_sel_s = min(TOPK_MAX, (past_len + dec_seq) // 4)

    y_p = x_prompt.reshape(bsz * seq, d_model)
    y_s = x_sample.reshape(db * dec_seq, d_model)
    outs_p, outs_s = [], []
    for l in range(depth):
        lam_init = 0.8 - 0.6 * math.exp(-0.3 * l)
        n_qkv = 3 * w_diff + 3 * w_dsa + w_idx
        w_l = w_in[l]
        rt = jnp.concatenate([w_router[l], w_group[l]], axis=1)
        rt = jnp.pad(rt, ((0, 0), (0, LANES - rt.shape[1])))
        rt_hi = rt.astype(BF16)
        blk = lax.broadcasted_iota(I32, (2 * LANES, 2 * LANES), 0) // HEAD_DIM
        blk_c = lax.broadcasted_iota(I32, (2 * LANES, 2 * LANES), 1) // HEAD_DIM
        prm = dict(
            w_diff=w_diff, w_dsa=w_dsa, w_idx=w_idx, h_idx=h_idx, n_groups=n_groups, n_experts=n_experts,
            g_mix=g_mix[l][None, :],
            w_qkv=w_l[:, :n_qkv].astype(BF16),
            w_small=jnp.pad(w_l[:, n_qkv:n_qkv + dh + h_idx], ((0, 0), (0, LANES - dh - h_idx))).astype(BF16),
            w_gate=w_l[:, n_qkv + dh + h_idx:].astype(BF16),
            g_qd=jnp.tile(g_q_diff[l], w_diff // dh)[None, :], g_kd=jnp.tile(g_k_diff[l], w_diff // dh)[None, :],
            g_qs=jnp.tile(g_q_dsa[l], w_dsa // dh)[None, :], g_ks=jnp.tile(g_k_dsa[l], w_dsa // dh)[None, :],
            bd=(blk == blk_c).astype(BF16),
            w_od=w_o_diff[l].astype(BF16), w_os=w_o_dsa[l].astype(BF16), w_out=w_out[l].astype(BF16),
            g_ffn=g_ffn[l][None, :], w_rt_hi=rt_hi, w_rt_lo=(rt - rt_hi.astype(F32)).astype(BF16))
        lam_vecs = jnp.stack([lam_q1[l], lam_k1[l], lam_q2[l], lam_k2[l]])
        g_sub = g_subln[l][None, :]
        wg_b, wu_b, wd_b = w_gate[l].astype(BF16), w_up[l].astype(BF16), w_down[l].astype(BF16)

        tm = _tile_rows(seq, 256)
        cos_p, sin_p = _rope_tables(jnp.arange(seq, dtype=I32))
        (qd, kdf, kdb, vdf, vdb, qs, ksf, ksb, vsf, vsb, qi, kif, ki2, wi, gates) = _inproj(
            y_p, cos_p, sin_p, seq // tm, tm, prm)
        r3 = lambda a: a.reshape(bsz, seq, a.shape[1])
        tq_d = _tile_rows(seq, 256)
        od = _diff_attn(r3(qd), r3(kdb), r3(vdb), lam_vecs, g_sub, lam_init, tq_d)
        tq_s = _tile_rows(seq, 128)
        kc = _tile_rows(seq, 256)
        assert kc >= k_sel_p
        os_ = _dsa_attn(r3(qi), r3(ki2), r3(wi), r3(qs), r3(ksb), r3(vsb), k_sel_p, h_idx, tq_s, kc)
        x1, h2, comb = _outproj(y_p, od.reshape(bsz * seq, w_diff), os_.reshape(bsz * seq, w_dsa), gates, prm, tm)
        y_p = _moe(x1, h2, comb, wg_b, wu_b, wd_b, _tile_rows(bsz * seq, 1024))
        outs_p.append((kdf.reshape(bsz, seq, h_diff, 2, dh), vdf.reshape(bsz, seq, h_diff, 2 * dh),
                       ksf.reshape(bsz, seq, h_dsa, dh), vsf.reshape(bsz, seq, h_dsa, dh), kif.reshape(bsz, seq, dh)))

        n_s = db * dec_seq
        cos_s, sin_s = _rope_tables(jnp.full((n_s,), past_len, I32))
        (qd, kdf, kdb, vdf, vdb, qs, ksf, ksb, vsf, vsb, qi, kif, ki2, wi, gates) = _inproj(
            y_s, cos_s, sin_s, 1, n_s, prm)
        pad = 2 * LANES
        scores = _idx_scores(page_table, qi.reshape(db, h_idx, dh), wi[:, dh:dh + h_idx].reshape(db, h_idx, 1),
                             ki2[:, :dh].reshape(db, 1, dh), cache_idx_k[l], pad)
        assert 2 * LANES >= k_sel_s
        bias = _select(scores, k_sel_s, past_len, 2 * LANES)
        caches = (cache_diff_k[l].reshape(n_pool, page, w_diff), cache_diff_v[l].reshape(n_pool, page, w_diff),
                  cache_dsa_k[l].reshape(n_pool, page, w_dsa), cache_dsa_v[l].reshape(n_pool, page, w_dsa))
        od, os_ = _decode_attn(page_table, lam_vecs, g_sub,
                               qd.reshape(db, 2 * h_diff, dh), kdb.reshape(db, 2 * h_diff, dh), vdb.reshape(db, 1, w_diff),
                               qs.reshape(db, h_dsa, dh), ksb.reshape(db, h_dsa, dh), vsb.reshape(db, 1, w_dsa),
                               bias.reshape(db, 1, bias.shape[1]), caches, lam_init)
        x1, h2, comb = _outproj(y_s, od, os_, gates, prm, n_s)
        y_s = _moe(x1, h2, comb, wg_b, wu_b, wd_b, n_s)
        outs_s.append((kdf.reshape(db, dec_seq, h_diff, 2, dh), vdf.reshape(db, dec_seq, h_diff, 2 * dh),
                       ksf.reshape(db, dec_seq, h_dsa, dh), vsf.reshape(db, dec_seq, h_dsa, dh),
                       kif.reshape(db, dec_seq, dh)))

    stack = lambda rows, i: jnp.stack([r[i] for r in rows])
    return ((y_p.reshape(bsz, seq, d_model), y_s.reshape(db, dec_seq, d_model))
            + tuple(stack(outs_p, i) for i in range(5)) + tuple(stack(outs_s, i) for i in range(5)))
```

```python
import functools
import math

import jax
import jax.numpy as jnp
from jax import lax
from jax.experimental import pallas as pl
from jax.experimental.pallas import tpu as pltpu

F32 = jnp.float32
BF16 = jnp.bfloat16
I32 = jnp.int32

EPS = 1e-6
ROPE_THETA = 10000.0
TOPK_MAX = 256
LANES = 128
HEAD_DIM = 64
MASK_NEG = -1e30
INT_MIN = -(2 ** 31)
KEY_NEG_INF = -2139095041
VMEM_LIMIT = 56 * 1024 * 1024
ROW_TILE = 256
KEY_CHUNK = 512
DSA_Q_TILE = 128
LOG2E = 1.4426950408889634


def _params(*sem):
    return pltpu.CompilerParams(dimension_semantics=sem, vmem_limit_bytes=VMEM_LIMIT)


def _dot(a, b):
    return jnp.dot(a, b, preferred_element_type=F32)


def _dot_nt(a, b):
    return lax.dot_general(a, b, (((1,), (1,)), ((), ())), preferred_element_type=F32)


def _split_bf16(a):
    hi = a.astype(BF16)
    lo = (a - hi.astype(F32)).astype(BF16)
    return hi, lo


_INPROJ_OUTS = ("qd", "kdT_f", "kdT_b", "vd_f", "vd_b", "qs", "ksT_f", "ksT_b", "vsT_f", "vs_b",
                "qi", "kiT_f", "ki2T_b", "wi", "gates")
_INPROJ_NATURAL = ("kd_b", "ks_b", "ki_b")


def _inproj_kernel(x_ref, cos_ref, sin_ref, gmix_ref, wqkv_ref, wsm_ref, wgate_ref,
                   gqd_ref, gkd_ref, gqs_ref, gks_ref, bd_ref, *out_refs,
                   names, w_diff, w_dsa, w_idx, h_idx, qk_scale):
    o = dict(zip(names, out_refs))
    x = x_ref[...]
    tm = x.shape[0]
    ms = jnp.mean(x * x, axis=-1, keepdims=True)
    h = ((x * lax.rsqrt(ms + EPS)) * gmix_ref[...]).astype(BF16)

    cos = cos_ref[...]
    sin = sin_ref[...]
    lane = lax.broadcasted_iota(I32, (tm, LANES), 1)
    low_half = (lane & (HEAD_DIM // 2)) == 0

    def rope(y):
        partner = jnp.where(low_half, pltpu.roll(y, LANES - HEAD_DIM // 2, 1), pltpu.roll(y, HEAD_DIM // 2, 1))
        return y * cos + partner * sin

    bd = bd_ref[...]

    def head_norm(p, g_ref):
        hi, lo = _split_bf16(p * p)
        width = bd.shape[0]
        parts = []
        for c in range(0, p.shape[1], width):
            parts.append(_dot(hi[:, c:c + width], bd) + _dot(lo[:, c:c + width], bd))
        ss = jnp.concatenate(parts, axis=1) if len(parts) > 1 else parts[0]
        return (p * lax.rsqrt(ss * (1.0 / HEAD_DIM) + EPS)) * g_ref[...]

    def seg(c0, width):
        return _dot(h, wqkv_ref[:, c0:c0 + width])

    def rope_all(y, scale=None):
        outs = []
        for c in range(0, y.shape[1], LANES):
            r = rope(y[:, c:c + LANES])
            outs.append(r if scale is None else r * scale)
        return jnp.concatenate(outs, axis=1) if len(outs) > 1 else outs[0]

    c0 = 0
    o["qd"][...] = rope_all(head_norm(seg(c0, w_diff), gqd_ref), qk_scale * LOG2E).astype(BF16)
    c0 += w_diff
    kd = rope_all(head_norm(seg(c0, w_diff), gkd_ref))
    kd_t = kd.T
    o["kdT_f"][...] = kd_t
    o["kdT_b"][...] = kd_t.astype(BF16)
    if "kd_b" in o:
        o["kd_b"][...] = kd.astype(BF16)
    c0 += w_diff
    vd = seg(c0, w_diff)
    o["vd_f"][...] = vd
    o["vd_b"][...] = vd.astype(BF16)
    c0 += w_diff
    o["qs"][...] = rope_all(head_norm(seg(c0, w_dsa), gqs_ref), qk_scale * LOG2E).astype(BF16)
    c0 += w_dsa
    ks = rope_all(head_norm(seg(c0, w_dsa), gks_ref))
    ks_t = ks.T
    o["ksT_f"][...] = ks_t
    o["ksT_b"][...] = ks_t.astype(BF16)
    if "ks_b" in o:
        o["ks_b"][...] = ks.astype(BF16)
    c0 += w_dsa
    vs = seg(c0, w_dsa)
    o["vsT_f"][...] = vs.T
    o["vs_b"][...] = vs.astype(BF16)
    c0 += w_dsa
    o["qi"][...] = rope_all(seg(c0, w_idx), qk_scale).astype(BF16)

    sm = _dot(h, wsm_ref[...])
    ki = rope(sm)
    ki2 = jnp.where(lane < HEAD_DIM, ki, pltpu.roll(ki, HEAD_DIM, 1))
    ki2_t = ki2.T
    o["kiT_f"][...] = ki2_t[:HEAD_DIM]
    o["ki2T_b"][...] = ki2_t.astype(BF16)
    if "ki_b" in o:
        o["ki_b"][...] = ki2.astype(BF16)
    o["wi"][...] = sm * (h_idx ** -0.5)

    g = _dot(h, wgate_ref[...])
    o["gates"][...] = (1.0 / (1.0 + jnp.exp(-g))).astype(BF16)


def _inproj(x, cos_t, sin_t, bsz, tm, kc, prm, natural_k):
    n, d = x.shape
    t = n // bsz
    tiles = t // tm
    per_chunk = kc // tm
    w_diff, w_dsa, w_idx, h_idx = prm["w_diff"], prm["w_dsa"], prm["w_idx"], prm["h_idx"]
    wg = prm["w_gate"].shape[1]
    row = lambda i: (i, 0)
    const = lambda i: (0, 0)
    pos = lambda i: (i % tiles, 0)
    full = lambda a: pl.BlockSpec(a.shape, const)
    in_specs = [pl.BlockSpec((tm, d), row), pl.BlockSpec((tm, LANES), pos), pl.BlockSpec((tm, LANES), pos),
                full(prm["g_mix"]), full(prm["w_qkv"]), full(prm["w_small"]), full(prm["w_gate"]),
                full(prm["g_qd"]), full(prm["g_kd"]), full(prm["g_qs"]), full(prm["g_ks"]), full(prm["bd"])]

    def nat(w, dt):
        return jax.ShapeDtypeStruct((n, w), dt), pl.BlockSpec((tm, w), row)

    def tr_f(w):
        return (jax.ShapeDtypeStruct((bsz, w, t), F32),
                pl.BlockSpec((None, w, tm), lambda i: (i // tiles, 0, i % tiles)))

    def tr_b(w):
        return (jax.ShapeDtypeStruct((bsz, t // kc, w, kc), BF16),
                pl.BlockSpec((None, None, w, tm),
                             lambda i: (i // tiles, (i % tiles) // per_chunk, 0, (i % tiles) % per_chunk)))

    table = dict(qd=nat(w_diff, BF16), kdT_f=tr_f(w_diff), kdT_b=tr_b(w_diff), vd_f=nat(w_diff, F32),
                 vd_b=nat(w_diff, BF16), qs=nat(w_dsa, BF16), ksT_f=tr_f(w_dsa), ksT_b=tr_b(w_dsa),
                 vsT_f=tr_f(w_dsa), vs_b=nat(w_dsa, BF16), qi=nat(w_idx, BF16), kiT_f=tr_f(HEAD_DIM),
                 ki2T_b=tr_b(LANES), wi=nat(LANES, F32), gates=nat(wg, BF16),
                 kd_b=nat(w_diff, BF16), ks_b=nat(w_dsa, BF16), ki_b=nat(LANES, BF16))
    names = _INPROJ_OUTS + (_INPROJ_NATURAL if natural_k else ())
    kern = functools.partial(_inproj_kernel, names=names, w_diff=w_diff, w_dsa=w_dsa, w_idx=w_idx, h_idx=h_idx,
                             qk_scale=HEAD_DIM ** -0.5)
    outs = pl.pallas_call(kern, grid=(n // tm,), in_specs=in_specs,
                          out_specs=[table[k][1] for k in names], out_shape=[table[k][0] for k in names],
                          compiler_params=_params("parallel"), name="inproj")(
        x, cos_t, sin_t, prm["g_mix"], prm["w_qkv"], prm["w_small"], prm["w_gate"],
        prm["g_qd"], prm["g_kd"], prm["g_qs"], prm["g_ks"], prm["bd"])
    return dict(zip(names, outs))


def _stack_halves(q):
    lane = lax.broadcasted_iota(I32, q.shape, 1)
    zero = jnp.zeros_like(q)
    return jnp.concatenate([jnp.where(lane < HEAD_DIM, q, zero), jnp.where(lane >= HEAD_DIM, q, zero)], axis=0)


def _lane_fold(a):
    out = a[:, :LANES]
    for c in range(LANES, a.shape[1], LANES):
        out = out + a[:, c:c + LANES]
    return out


def _online_update(s_ref, v, m_sc, l_sc, acc_sc, fin=None):
    s = s_ref[...]
    if fin is not None:
        s = fin(s, 0)
    m_prev = m_sc[...]
    m_new = jnp.maximum(m_prev, jnp.max(s, axis=-1, keepdims=True))
    alpha = jnp.exp2(m_prev - m_new)
    p = jnp.exp2(s - jnp.tile(m_new, (1, s.shape[1] // LANES)))
    l_sc[...] = alpha * l_sc[...] + _lane_fold(p)
    acc_sc[...] = alpha * acc_sc[...] + _dot(p.astype(BF16), v)
    m_sc[...] = m_new


def _finish(l_sc, acc_sc):
    return acc_sc[...] * (1.0 / jnp.sum(l_sc[...], axis=-1, keepdims=True))


def _attend_chunks(n, score, value, s_a, s_b, m_sc, l_sc, acc_sc, last=None):
    s_a[...] = score(0)

    def pair(p, carry):
        j = 2 * p
        s_b[...] = score(j + 1)
        _online_update(s_a, value(j), m_sc, l_sc, acc_sc)
        s_a[...] = score(j + 2)
        _online_update(s_b, value(j + 1), m_sc, l_sc, acc_sc)
        return carry

    lax.fori_loop(0, (n - 1) // 2, pair, 0)

    @pl.when(n % 2 == 1)
    def _():
        _online_update(s_a, value(n - 1), m_sc, l_sc, acc_sc, last)

    @pl.when(n % 2 == 0)
    def _():
        s_b[...] = score(n - 1)
        _online_update(s_a, value(n - 2), m_sc, l_sc, acc_sc)
        _online_update(s_b, value(n - 1), m_sc, l_sc, acc_sc, last)


def _lambda(lam_ref, lam_init):
    a = jnp.sum(lam_ref[0:1, :] * lam_ref[1:2, :], axis=-1, keepdims=True)
    b = jnp.sum(lam_ref[2:3, :] * lam_ref[3:4, :], axis=-1, keepdims=True)
    return jnp.exp(a) - jnp.exp(b) + lam_init


def _attn_scratch(rows, kc):
    return [pltpu.VMEM((rows, LANES), F32)] * 3 + [pltpu.VMEM((rows, kc), F32)] * 2


def _diff_attn_kernel(lam_ref, gsub_ref, q_ref, kt_ref, v_ref, o_ref, m_sc, l_sc, acc_sc, s_a, s_b, *,
                      tq, kc, lam_init):
    q0 = pl.program_id(2) * tq
    n_c = q0 // kc + 1
    q2 = _stack_halves(q_ref[...])
    m_sc[...] = jnp.full(m_sc.shape, -jnp.inf, F32)
    l_sc[...] = jnp.zeros(l_sc.shape, F32)
    acc_sc[...] = jnp.zeros(acc_sc.shape, F32)

    def causal(s, row0):
        r = lax.broadcasted_iota(I32, s.shape, 0) + row0
        r = jnp.where(r >= tq, r - tq, r) + q0
        c = lax.broadcasted_iota(I32, s.shape, 1) + (n_c - 1) * kc
        return jnp.where(c <= r, s, -jnp.inf)

    _attend_chunks(n_c, lambda j: _dot(q2, kt_ref[j]),
                   lambda j: v_ref[pl.ds(pl.multiple_of(j * kc, kc), kc), :],
                   s_a, s_b, m_sc, l_sc, acc_sc, last=causal)

    a = _finish(l_sc, acc_sc)
    o = a[:tq] - _lambda(lam_ref, lam_init) * a[tq:]
    ms = jnp.mean(o * o, axis=-1, keepdims=True)
    on = ((o * lax.rsqrt(ms + EPS)) * gsub_ref[...]) * (1.0 - lam_init)
    o_ref[...] = on.astype(BF16)


def _diff_attn(q, kt, v, lam_vecs, g_sub, lam_init, tq):
    b, t, w = q.shape
    nc, kc = kt.shape[1], kt.shape[3]
    assert kc % tq == 0
    grid = (b, w // LANES, t // tq)
    qmap = lambda bi, hi, qi: (bi, qi, hi)
    const = lambda bi, hi, qi: (0, 0)
    kern = functools.partial(_diff_attn_kernel, tq=tq, kc=kc, lam_init=lam_init)
    return pl.pallas_call(
        kern, grid=grid,
        in_specs=[pl.BlockSpec(lam_vecs.shape, const), pl.BlockSpec(g_sub.shape, const),
                  pl.BlockSpec((None, tq, LANES), qmap),
                  pl.BlockSpec((None, nc, LANES, kc), lambda bi, hi, qi: (bi, 0, hi, 0)),
                  pl.BlockSpec((None, t, LANES), lambda bi, hi, qi: (bi, 0, hi))],
        out_specs=pl.BlockSpec((None, tq, LANES), qmap),
        out_shape=jax.ShapeDtypeStruct((b, t, w), BF16),
        scratch_shapes=_attn_scratch(2 * tq, kc),
        compiler_params=_params("parallel", "parallel", "arbitrary"), name="diff_attn")(lam_vecs, g_sub, q, kt, v)


def _key_to_f32(k):
    b = k ^ (lax.shift_right_arithmetic(k, 31) & 0x7FFFFFFF)
    return lax.bitcast_convert_type(b, F32)


def _select_bias(score_sc, bias_sc, n_c, k_sel, row_pos, col0):
    _, r, kc = score_sc.shape

    def count(pred):
        def body(c, a):
            return a + _lane_fold(jnp.where(pred(score_sc[c]), 1.0, 0.0))
        a = lax.fori_loop(0, n_c, body, jnp.zeros((r, LANES), F32))
        return jnp.sum(a, axis=-1, keepdims=True)

    def bit_body(i, t):
        cand = t + lax.shift_left(jnp.int32(1), 31 - i)
        cand_f = _key_to_f32(cand)
        cnt = count(lambda sc: sc >= cand_f)
        return jnp.where(cnt >= k_sel, cand, t)

    t = lax.fori_loop(0, 32, bit_body, jnp.full((r, 1), INT_MIN, I32))
    thr = _key_to_f32(jnp.maximum(t, KEY_NEG_INF))
    need = k_sel - count(lambda sc: sc > thr)

    ri = lax.broadcasted_iota(I32, (kc, kc), 0)
    ci = lax.broadcasted_iota(I32, (kc, kc), 1)
    tri = jnp.where(ri <= ci, 1.0, 0.0).astype(BF16)
    col = lax.broadcasted_iota(I32, (r, kc), 1)

    def bias_body(c, run):
        sc = score_sc[c]
        eq = sc == thr
        pre = _dot(jnp.where(eq, 1.0, 0.0).astype(BF16), tri)
        sel = (sc > thr) | (eq & ((pre + run) <= need))
        causal = (col + (col0 + c * kc)) <= row_pos
        bias_sc[c] = jnp.where(sel & causal, 0.0, MASK_NEG)
        return run + pre[:, kc - 1:kc]

    lax.fori_loop(0, n_c, bias_body, jnp.zeros((r, 1), F32))


def _dsa_attn_kernel(qi_ref, ki2t_ref, wi_ref, qs_ref, kst_ref, vs_ref, o_ref,
                     qstack_sc, score_sc, bias_sc, m_sc, l_sc, acc_sc, s_a, s_b, *, tq, kc, k_sel, h_idx):
    qb = pl.program_id(1)
    q0 = qb * tq
    n_c = q0 // kc + 1
    row_pos = q0 + lax.broadcasted_iota(I32, (tq, 1), 0)

    for hh in range(h_idx):
        pair = qi_ref[:, (hh // 2) * LANES:(hh // 2 + 1) * LANES]
        lane = lax.broadcasted_iota(I32, pair.shape, 1)
        keep = (lane < HEAD_DIM) if hh % 2 == 0 else (lane >= HEAD_DIM)
        qstack_sc[hh * tq:(hh + 1) * tq, :] = jnp.where(keep, pair, jnp.zeros_like(pair))
    wi = wi_ref[...]

    def score_body(c, carry):
        d = _dot(qstack_sc[...], ki2t_ref[c])
        acc = jnp.zeros((tq, kc), F32)
        for hh in range(h_idx):
            acc = acc + jnp.maximum(d[hh * tq:(hh + 1) * tq], 0.0) * wi[:, HEAD_DIM + hh:HEAD_DIM + hh + 1]
        col = c * kc + lax.broadcasted_iota(I32, (tq, kc), 1)
        score_sc[c] = jnp.where(col <= row_pos, acc, -jnp.inf)
        return carry

    lax.fori_loop(0, n_c, score_body, 0)
    _select_bias(score_sc, bias_sc, n_c, k_sel, row_pos, 0)

    for hp in range(qs_ref.shape[1] // LANES):
        q2 = _stack_halves(qs_ref[:, hp * LANES:(hp + 1) * LANES])
        m_sc[...] = jnp.full(m_sc.shape, MASK_NEG, F32)
        l_sc[...] = jnp.zeros(l_sc.shape, F32)
        acc_sc[...] = jnp.zeros(acc_sc.shape, F32)

        def score(c, q2=q2, hp=hp):
            bias = bias_sc[c]
            return _dot(q2, kst_ref[c, hp * LANES:(hp + 1) * LANES, :]) + jnp.concatenate([bias, bias], axis=0)

        def value(c, hp=hp):
            return vs_ref[pl.ds(pl.multiple_of(c * kc, kc), kc), hp * LANES:(hp + 1) * LANES]

        _attend_chunks(n_c, score, value, s_a, s_b, m_sc, l_sc, acc_sc)
        a = _finish(l_sc, acc_sc)
        lane = lax.broadcasted_iota(I32, (tq, LANES), 1)
        o_ref[:, hp * LANES:(hp + 1) * LANES] = jnp.where(lane < HEAD_DIM, a[:tq], a[tq:]).astype(BF16)


def _dsa_attn(qi, ki2t, wi, qs, kst, vs, k_sel, h_idx, tq):
    b, t, w = qs.shape
    nc, kc = kst.shape[1], kst.shape[3]
    assert kc % tq == 0 and kc >= k_sel
    qmap = lambda bi, qb: (bi, qb, 0)
    kern = functools.partial(_dsa_attn_kernel, tq=tq, kc=kc, k_sel=k_sel, h_idx=h_idx)
    return pl.pallas_call(
        kern, grid=(b, t // tq),
        in_specs=[pl.BlockSpec((None, tq, qi.shape[2]), qmap),
                  pl.BlockSpec((None, nc, LANES, kc), lambda bi, qb: (bi, 0, 0, 0)),
                  pl.BlockSpec((None, tq, LANES), qmap), pl.BlockSpec((None, tq, w), qmap),
                  pl.BlockSpec((None, nc, w, kc), lambda bi, qb: (bi, 0, 0, 0)),
                  pl.BlockSpec((None, t, w), lambda bi, qb: (bi, 0, 0))],
        out_specs=pl.BlockSpec((None, tq, w), qmap),
        out_shape=jax.ShapeDtypeStruct((b, t, w), BF16),
        scratch_shapes=[pltpu.VMEM((h_idx * tq, LANES), BF16), pltpu.VMEM((nc, tq, kc), F32),
                        pltpu.VMEM((nc, tq, kc), F32)] + _attn_scratch(2 * tq, kc),
        compiler_params=_params("parallel", "arbitrary"), name="dsa_attn")(qi, ki2t, wi, qs, kst, vs)


def _page_specs(n_pages, block):
    def spec(j):
        return pl.BlockSpec((None,) + block, lambda b, pt, j=j: (pt[b, j],) + (0,) * len(block))
    return [spec(j) for j in range(n_pages)]


def _idx_score_kernel(pt_ref, q_ref, w_ref, knew_ref, *refs, n_pages, page, pad):
    page_refs, o_ref = refs[:n_pages], refs[n_pages]
    q = q_ref[...]
    w = w_ref[...]
    for j in range(n_pages):
        d = _dot(q, page_refs[j][...].astype(BF16))
        o_ref[:, j * page:(j + 1) * page] = jnp.sum(jnp.maximum(d, 0.0) * w, axis=0, keepdims=True)
    d_new = jnp.sum(q.astype(F32) * knew_ref[...].astype(F32), axis=-1, keepdims=True)
    s_new = jnp.sum(jnp.maximum(d_new, 0.0) * w, axis=0, keepdims=True)
    lane = lax.broadcasted_iota(I32, (1, pad), 1)
    o_ref[:, n_pages * page:] = jnp.where(lane == 0, s_new, -jnp.inf)


def _idx_scores(page_table, q_i, w_i, k_new, cache_t, pad):
    db, n_pages = page_table.shape
    _, dh, page = cache_t.shape
    h_idx = q_i.shape[1]
    tk = n_pages * page + pad
    seq = lambda b, pt: (b, 0, 0)
    kern = functools.partial(_idx_score_kernel, n_pages=n_pages, page=page, pad=pad)
    grid_spec = pltpu.PrefetchScalarGridSpec(
        num_scalar_prefetch=1, grid=(db,),
        in_specs=[pl.BlockSpec((None, h_idx, dh), seq), pl.BlockSpec((None, h_idx, 1), seq),
                  pl.BlockSpec((None, 1, dh), seq)] + _page_specs(n_pages, (dh, page)),
        out_specs=pl.BlockSpec((None, 1, tk), seq))
    out = pl.pallas_call(kern, grid_spec=grid_spec, out_shape=jax.ShapeDtypeStruct((db, 1, tk), F32),
                         compiler_params=_params("parallel"), name="idx_scores")(
        page_table, q_i, w_i, k_new, *([cache_t] * n_pages))
    return out.reshape(db, tk)


def _select_kernel(s_ref, o_ref, score_sc, bias_sc, *, kc, k_sel, q_pos):
    r = s_ref.shape[0]
    nc = s_ref.shape[1] // kc
    for c in range(nc):
        score_sc[c] = s_ref[:, c * kc:(c + 1) * kc]
    row_pos = jnp.full((r, 1), q_pos, I32)
    _select_bias(score_sc, bias_sc, nc, k_sel, row_pos, 0)
    for c in range(nc):
        o_ref[:, c * kc:(c + 1) * kc] = bias_sc[c]


def _select(scores, k_sel, q_pos, kc):
    r, tk = scores.shape
    nc = tk // kc
    kern = functools.partial(_select_kernel, kc=kc, k_sel=k_sel, q_pos=q_pos)
    return pl.pallas_call(
        kern, grid=(1,), in_specs=[pl.BlockSpec((r, tk), lambda i: (0, 0))],
        out_specs=pl.BlockSpec((r, tk), lambda i: (0, 0)), out_shape=jax.ShapeDtypeStruct((r, tk), F32),
        scratch_shapes=[pltpu.VMEM((nc, r, kc), F32), pltpu.VMEM((nc, r, kc), F32)],
        compiler_params=_params("arbitrary"), name="select")(scores)


def _block_diag_rows(q, width):
    t = jnp.concatenate([q] * (width // HEAD_DIM), axis=1)
    row = lax.broadcasted_iota(I32, t.shape, 0)
    col = lax.broadcasted_iota(I32, t.shape, 1)
    return jnp.where((col >= row * HEAD_DIM) & (col < (row + 1) * HEAD_DIM), t, jnp.zeros_like(t))


def _decode_attn_kernel(pt_ref, lam_ref, gsub_ref, qd_ref, kdn_ref, vdn_ref, qs_ref, ksn_ref, vsn_ref, bias_ref,
                        *refs, n_pages, page, lam_init):
    kd_pages = refs[:n_pages]
    vd_pages = refs[n_pages:2 * n_pages]
    ks_pages = refs[2 * n_pages:3 * n_pages]
    vs_pages = refs[3 * n_pages:4 * n_pages]
    od_ref, os_ref = refs[4 * n_pages], refs[4 * n_pages + 1]
    h_diff = vd_pages[0].shape[0] // page

    def probs(q_ref, knew_ref, k_pages, bias):
        q = q_ref[...]
        qbd = _block_diag_rows(q, k_pages[0].shape[0])
        s = [_dot(qbd, k_pages[j][...].astype(BF16)) for j in range(n_pages)]
        s_new = jnp.sum(q.astype(F32) * knew_ref[...].astype(F32), axis=-1, keepdims=True)
        if bias is not None:
            s = [s[j] + bias[:, j * page:(j + 1) * page] for j in range(n_pages)]
            s_new = s_new + bias[:, n_pages * page:n_pages * page + 1]
        m = s_new
        for j in range(n_pages):
            m = jnp.maximum(m, jnp.max(s[j], axis=-1, keepdims=True))
        p_new = jnp.exp2(s_new - m)
        p = [jnp.exp2(s[j] - m) for j in range(n_pages)]
        l = p_new
        for j in range(n_pages):
            l = l + jnp.sum(p[j], axis=-1, keepdims=True)
        return p, p_new, 1.0 / l

    p, p_new, inv_l = probs(qd_ref, kdn_ref, kd_pages, None)
    acc = p_new * vdn_ref[...].astype(F32)
    for j in range(n_pages):
        v = jnp.concatenate([vd_pages[j][pl.ds(hh, page, stride=h_diff), :] for hh in range(h_diff)], axis=1)
        acc = acc + _dot(p[j].astype(BF16), v.astype(BF16))
    a = acc * inv_l
    row = lax.broadcasted_iota(I32, a.shape, 0)
    col = lax.broadcasted_iota(I32, a.shape, 1)
    own = (col // LANES) == (row // 2)
    coef = jnp.where(row % 2 == 0, 1.0, -_lambda(lam_ref, lam_init))
    o = jnp.sum(jnp.where(own, a * coef, 0.0), axis=0, keepdims=True)
    parts = []
    for c in range(0, o.shape[1], LANES):
        oc = o[:, c:c + LANES]
        ms = jnp.mean(oc * oc, axis=-1, keepdims=True)
        parts.append(((oc * lax.rsqrt(ms + EPS)) * gsub_ref[...]) * (1.0 - lam_init))
    od_ref[...] = jnp.concatenate(parts, axis=1).astype(BF16)

    p, p_new, inv_l = probs(qs_ref, ksn_ref, ks_pages, bias_ref[...])
    acc = p_new * vsn_ref[...].astype(F32)
    for j in range(n_pages):
        acc = acc + _dot_nt(p[j].astype(BF16), vs_pages[j][...].astype(BF16))
    a = acc * inv_l
    row = lax.broadcasted_iota(I32, a.shape, 0)
    col = lax.broadcasted_iota(I32, a.shape, 1)
    os_ref[...] = jnp.sum(jnp.where((col // HEAD_DIM) == row, a, 0.0), axis=0, keepdims=True).astype(BF16)


def _decode_attn(page_table, lam_vecs, g_sub, qd, kdn, vdn, qs, ksn, vsn, bias, caches, lam_init):
    db, n_pages = page_table.shape
    c_kd, c_vd, c_ks, c_vs = caches
    w_diff, page = c_kd.shape[1:]
    w_dsa = c_ks.shape[1]
    seq = lambda b, pt: (b, 0, 0)
    const = lambda b, pt: (0, 0)
    in_specs = [pl.BlockSpec(lam_vecs.shape, const), pl.BlockSpec(g_sub.shape, const),
                pl.BlockSpec((None,) + qd.shape[1:], seq), pl.BlockSpec((None,) + kdn.shape[1:], seq),
                pl.BlockSpec((None,) + vdn.shape[1:], seq),
                pl.BlockSpec((None,) + qs.shape[1:], seq), pl.BlockSpec((None,) + ksn.shape[1:], seq),
                pl.BlockSpec((None,) + vsn.shape[1:], seq), pl.BlockSpec((None,) + bias.shape[1:], seq)]
    for cache in caches:
        in_specs += _page_specs(n_pages, tuple(cache.shape[1:]))
    kern = functools.partial(_decode_attn_kernel, n_pages=n_pages, page=page, lam_init=lam_init)
    grid_spec = pltpu.PrefetchScalarGridSpec(
        num_scalar_prefetch=1, grid=(db,), in_specs=in_specs,
        out_specs=[pl.BlockSpec((None, 1, w_diff), seq), pl.BlockSpec((None, 1, w_dsa), seq)])
    pages = []
    for cache in caches:
        pages += [cache] * n_pages
    od, os_ = pl.pallas_call(
        kern, grid_spec=grid_spec,
        out_shape=[jax.ShapeDtypeStruct((db, 1, w_diff), BF16), jax.ShapeDtypeStruct((db, 1, w_dsa), BF16)],
        compiler_params=_params("parallel"), name="decode_attn")(
        page_table, lam_vecs, g_sub, qd, kdn, vdn, qs, ksn, vsn, bias, *pages)
    return od.reshape(db, w_diff), os_.reshape(db, w_dsa)


def _outproj_kernel(x_ref, od_ref, os_ref, gate_ref, wod_ref, wos_ref, wout_ref, gffn_ref, wrh_ref, wrl_ref,
                    x1_ref, h2_ref, comb_ref, *, n_groups, n_experts):
    d = x_ref.shape[1]
    yd = _dot(od_ref[...], wod_ref[...])
    ys = _dot(os_ref[...], wos_ref[...])
    merged = gate_ref[:, :d].astype(F32) * yd + gate_ref[:, d:].astype(F32) * ys
    x1 = x_ref[...] + _dot(merged.astype(BF16), wout_ref[...])
    x1_ref[...] = x1

    ms = jnp.mean(x1 * x1, axis=-1, keepdims=True)
    h2 = (x1 * lax.rsqrt(ms + EPS)) * gffn_ref[...]
    h2_ref[...] = h2.astype(BF16)

    hi, lo = _split_bf16(h2)
    lg = _dot(hi, wrh_ref[...]) + (_dot(hi, wrl_ref[...]) + _dot(lo, wrh_ref[...]))
    lane = lax.broadcasted_iota(I32, lg.shape, 1)
    lanef = lane.astype(F32)
    big = float(LANES)
    epg = n_experts // n_groups

    gmask = (lane >= n_experts) & (lane < n_experts + n_groups)
    gl = jnp.where(gmask, lg, -jnp.inf)
    gmax = jnp.max(gl, axis=-1, keepdims=True)
    gsel = jnp.min(jnp.where(gl == gmax, lanef, big), axis=-1, keepdims=True) - n_experts
    g_w = 1.0 / jnp.sum(jnp.exp(gl - gmax), axis=-1, keepdims=True)

    lo_lane = gsel * epg
    emask = (lanef >= lo_lane) & (lanef < lo_lane + epg)
    el = jnp.where(emask, lg, -jnp.inf)
    v1 = jnp.max(el, axis=-1, keepdims=True)
    i1 = jnp.min(jnp.where(el == v1, lanef, big), axis=-1, keepdims=True)
    el2 = jnp.where(lanef == i1, -jnp.inf, el)
    v2 = jnp.max(el2, axis=-1, keepdims=True)
    i2 = jnp.min(jnp.where(el2 == v2, lanef, big), axis=-1, keepdims=True)
    e21 = jnp.exp(v2 - v1)
    den = 1.0 / (1.0 + e21)
    comb_ref[...] = (jnp.where(lanef == i1, den * g_w, 0.0) + jnp.where(lanef == i2, (e21 * den) * g_w, 0.0))


def _outproj(x, od, os_, gates, prm, tm):
    n, d = x.shape
    row = lambda i: (i, 0)
    const = lambda i: (0, 0)
    full = lambda a: pl.BlockSpec(a.shape, const)
    kern = functools.partial(_outproj_kernel, n_groups=prm["n_groups"], n_experts=prm["n_experts"])
    return pl.pallas_call(
        kern, grid=(n // tm,),
        in_specs=[pl.BlockSpec((tm, d), row), pl.BlockSpec((tm, od.shape[1]), row), pl.BlockSpec((tm, os_.shape[1]), row),
                  pl.BlockSpec((tm, gates.shape[1]), row), full(prm["w_od"]), full(prm["w_os"]), full(prm["w_out"]),
                  full(prm["g_ffn"]), full(prm["w_rt_hi"]), full(prm["w_rt_lo"])],
        out_specs=[pl.BlockSpec((tm, d), row), pl.BlockSpec((tm, d), row), pl.BlockSpec((tm, LANES), row)],
        out_shape=[jax.ShapeDtypeStruct((n, d), F32), jax.ShapeDtypeStruct((n, d), BF16),
                   jax.ShapeDtypeStruct((n, LANES), F32)],
        compiler_params=_params("parallel"), name="outproj")(
        x, od, os_, gates, prm["w_od"], prm["w_os"], prm["w_out"], prm["g_ffn"], prm["w_rt_hi"], prm["w_rt_lo"])


def _moe_kernel(x1_ref, h2_ref, comb_ref, wg_ref, wu_ref, wd_ref, y_ref, acc_sc):
    e = pl.program_id(1)

    @pl.when(e == 0)
    def _():
        acc_sc[...] = jnp.zeros(acc_sc.shape, F32)

    h = h2_ref[...]
    a = _dot(h, wg_ref[...])
    u = _dot(h, wu_ref[...])
    act = (a * (1.0 / (1.0 + jnp.exp(-a)))) * u
    o = _dot(act.astype(BF16), wd_ref[...])
    comb = comb_ref[...]
    lane = lax.broadcasted_iota(I32, comb.shape, 1)
    c = jnp.sum(jnp.where(lane == e, comb, 0.0), axis=-1, keepdims=True)
    acc_sc[...] += o * c

    @pl.when(e == pl.num_programs(1) - 1)
    def _():
        y_ref[...] = x1_ref[...] + acc_sc[...]


def _moe(x1, h2, comb, w_gate, w_up, w_down, tm):
    n, d = x1.shape
    ne, _, de = w_gate.shape
    row = lambda i, e: (i, 0)
    return pl.pallas_call(
        _moe_kernel, grid=(n // tm, ne),
        in_specs=[pl.BlockSpec((tm, d), row), pl.BlockSpec((tm, d), row), pl.BlockSpec((tm, LANES), row),
                  pl.BlockSpec((None, d, de), lambda i, e: (e, 0, 0)), pl.BlockSpec((None, d, de), lambda i, e: (e, 0, 0)),
                  pl.BlockSpec((None, de, d), lambda i, e: (e, 0, 0))],
        out_specs=pl.BlockSpec((tm, d), row), out_shape=jax.ShapeDtypeStruct((n, d), F32),
        scratch_shapes=[pltpu.VMEM((tm, d), F32)],
        compiler_params=_params("parallel", "arbitrary"), name="moe")(x1, h2, comb, w_gate, w_up, w_down)


def _rope_tables(pos):
    half = HEAD_DIM // 2
    inv_freq = jnp.power(ROPE_THETA, -jnp.arange(half, dtype=F32) / half)
    ang = pos.astype(F32)[:, None] * inv_freq[None, :]
    cos, sin = jnp.cos(ang), jnp.sin(ang)
    reps = LANES // HEAD_DIM
    return jnp.tile(cos, (1, 2 * reps)), jnp.tile(jnp.concatenate([-sin, sin], axis=1), (1, reps))


def _tile_rows(n, target):
    t = min(n, target)
    while n % t:
        t //= 2
    return t


def _key_outputs(o, bsz, t, h_diff, h_dsa, dh):
    kd = o["kdT_f"].reshape(bsz, h_diff, 2, dh, t).transpose(0, 4, 1, 2, 3)
    vd = o["vd_f"].reshape(bsz, t, h_diff, 2 * dh)
    ks = o["ksT_f"].reshape(bsz, h_dsa, dh, t).transpose(0, 3, 1, 2)
    vs = o["vsT_f"].reshape(bsz, h_dsa, dh, t).transpose(0, 3, 1, 2)
    ki = o["kiT_f"].transpose(0, 2, 1)
    return kd, vd, ks, vs, ki


def kernel(x_prompt, x_sample, cache_diff_k, cache_diff_v, cache_dsa_k, cache_dsa_v, cache_idx_k, page_table, g_mix, w_in, g_q_diff, g_k_diff, lam_q1, lam_k1, lam_q2, lam_k2, g_subln, g_q_dsa, g_k_dsa, w_o_diff, w_o_dsa, w_out, g_ffn, w_group, w_router, w_gate, w_up, w_down):
    bsz, seq, d_model = x_prompt.shape
    db, dec_seq, _ = x_sample.shape
    depth, n_pool, page, h_diff, _, dh = cache_diff_k.shape
    h_dsa = cache_dsa_k.shape[3]
    n_pages = page_table.shape[1]
    past_len = n_pages * page
    n_groups = w_group.shape[2]
    n_experts = w_router.shape[2]
    w_diff = h_diff * 2 * dh
    w_dsa = h_dsa * dh
    h_idx = (w_in.shape[2] - (3 * w_diff + 3 * w_dsa + dh + 2 * d_model)) // (dh + 1)
    w_idx = h_idx * dh
    assert dh == HEAD_DIM and cache_dsa_k.shape[4] == HEAD_DIM and cache_idx_k.shape[3] == HEAD_DIM
    assert dec_seq == 1 and depth == 1
    assert 3 * w_diff + 3 * w_dsa + w_idx + dh + h_idx + 2 * d_model == w_in.shape[2]
    k_sel_p = min(TOPK_MAX, seq // 4)
    k_sel_s = min(TOPK_MAX, (past_len + dec_seq) // 4)

    l = 0
    lam_init = 0.8 - 0.6 * math.exp(-0.3 * l)
    n_qkv = 3 * w_diff + 3 * w_dsa + w_idx
    w_l = w_in[l]
    rt = jnp.concatenate([w_router[l], w_group[l]], axis=1)
    rt = jnp.pad(rt, ((0, 0), (0, LANES - rt.shape[1])))
    rt_hi = rt.astype(BF16)
    blk = lax.broadcasted_iota(I32, (2 * LANES, 2 * LANES), 0) // HEAD_DIM
    blk_c = lax.broadcasted_iota(I32, (2 * LANES, 2 * LANES), 1) // HEAD_DIM
    prm = dict(
        w_diff=w_diff, w_dsa=w_dsa, w_idx=w_idx, h_idx=h_idx, n_groups=n_groups, n_experts=n_experts,
        g_mix=g_mix[l][None, :],
        w_qkv=w_l[:, :n_qkv].astype(BF16),
        w_small=jnp.pad(w_l[:, n_qkv:n_qkv + dh + h_idx], ((0, 0), (0, LANES - dh - h_idx))).astype(BF16),
        w_gate=w_l[:, n_qkv + dh + h_idx:].astype(BF16),
        g_qd=jnp.tile(g_q_diff[l], w_diff // dh)[None, :], g_kd=jnp.tile(g_k_diff[l], w_diff // dh)[None, :],
        g_qs=jnp.tile(g_q_dsa[l], w_dsa // dh)[None, :], g_ks=jnp.tile(g_k_dsa[l], w_dsa // dh)[None, :],
        bd=(blk == blk_c).astype(BF16),
        w_od=w_o_diff[l].astype(BF16), w_os=w_o_dsa[l].astype(BF16), w_out=w_out[l].astype(BF16),
        g_ffn=g_ffn[l][None, :], w_rt_hi=rt_hi, w_rt_lo=(rt - rt_hi.astype(F32)).astype(BF16))
    lam_vecs = jnp.stack([lam_q1[l], lam_k1[l], lam_q2[l], lam_k2[l]])
    g_sub = g_subln[l][None, :]
    wg_b, wu_b, wd_b = w_gate[l].astype(BF16), w_up[l].astype(BF16), w_down[l].astype(BF16)

    n_p = bsz * seq
    tm = _tile_rows(seq, ROW_TILE)
    kc = _tile_rows(seq, KEY_CHUNK)
    cos_p, sin_p = _rope_tables(jnp.arange(seq, dtype=I32))
    x_p = x_prompt.reshape(n_p, d_model)
    o = _inproj(x_p, cos_p, sin_p, bsz, tm, kc, prm, natural_k=False)
    r3 = lambda a: a.reshape(bsz, seq, a.shape[1])
    od = _diff_attn(r3(o["qd"]), o["kdT_b"], r3(o["vd_b"]), lam_vecs, g_sub, lam_init, tm)
    os_ = _dsa_attn(r3(o["qi"]), o["ki2T_b"], r3(o["wi"]), r3(o["qs"]), o["ksT_b"], r3(o["vs_b"]),
                    k_sel_p, h_idx, _tile_rows(seq, DSA_Q_TILE))
    x1, h2, comb = _outproj(x_p, od.reshape(n_p, w_diff), os_.reshape(n_p, w_dsa), o["gates"], prm, tm)
    y_p = _moe(x1, h2, comb, wg_b, wu_b, wd_b, _tile_rows(n_p, 1024))
    outs_p = _key_outputs(o, bsz, seq, h_diff, h_dsa, dh)

    cos_s, sin_s = _rope_tables(jnp.full((db,), past_len, I32))
    x_s = x_sample.reshape(db, d_model)
    o = _inproj(x_s, cos_s, sin_s, 1, db, db, prm, natural_k=True)
    pad = 2 * LANES
    assert pad >= k_sel_s
    scores = _idx_scores(page_table, o["qi"].reshape(db, h_idx, dh), o["wi"][:, dh:dh + h_idx].reshape(db, h_idx, 1),
                         o["ki_b"][:, :dh].reshape(db, 1, dh), cache_idx_k[l].transpose(0, 2, 1), pad)
    bias = _select(scores, k_sel_s, past_len, pad)
    caches = (cache_diff_k[l].transpose(0, 2, 3, 4, 1).reshape(n_pool, w_diff, page),
              cache_diff_v[l].reshape(n_pool, page * h_diff, 2 * dh),
              cache_dsa_k[l].transpose(0, 2, 3, 1).reshape(n_pool, w_dsa, page),
              cache_dsa_v[l].transpose(0, 2, 3, 1).reshape(n_pool, w_dsa, page))
    od, os_ = _decode_attn(page_table, lam_vecs, g_sub,
                           o["qd"].reshape(db, 2 * h_diff, dh), o["kd_b"].reshape(db, 2 * h_diff, dh),
                           o["vd_b"].reshape(db, 1, w_diff),
                           o["qs"].reshape(db, h_dsa, dh), o["ks_b"].reshape(db, h_dsa, dh),
                           o["vs_b"].reshape(db, 1, w_dsa),
                           bias.reshape(db, 1, bias.shape[1]), caches, lam_init)
    x1, h2, comb = _outproj(x_s, od, os_, o["gates"], prm, db)
    y_s = _moe(x1, h2, comb, wg_b, wu_b, wd_b, db)
    outs_s = tuple(a.reshape((db, dec_seq) + a.shape[2:]) for a in _key_outputs(o, 1, db, h_diff, h_dsa, dh))

    return ((y_p.reshape(bsz, seq, d_model), y_s.reshape(db, dec_seq, d_model))
            + tuple(a[None] for a in outs_p) + tuple(a[None] for a in outs_s))
```

```python
import functools
import math

import jax
import jax.numpy as jnp
from jax import lax
from jax.experimental import pallas as pl
from jax.experimental.pallas import tpu as pltpu

F32 = jnp.float32
BF16 = jnp.bfloat16
I32 = jnp.int32

EPS = 1e-6
ROPE_THETA = 10000.0
TOPK_MAX = 256
LANES = 128
HEAD_DIM = 64
MASK_NEG = -1e30
INT_MIN = -(2 ** 31)
KEY_NEG_INF = -2139095041
VMEM_LIMIT = 56 * 1024 * 1024
ROW_TILE = 256
KEY_CHUNK = 512
DSA_Q_TILE = 128
MOE_ROW_TILE = 256
LOG2E = 1.4426950408889634


def _params(*sem):
    return pltpu.CompilerParams(dimension_semantics=sem, vmem_limit_bytes=VMEM_LIMIT)


def _dot(a, b):
    return jnp.dot(a, b, preferred_element_type=F32)


def _dot_nt(a, b):
    return lax.dot_general(a, b, (((1,), (1,)), ((), ())), preferred_element_type=F32)


def _split_bf16(a):
    hi = a.astype(BF16)
    lo = (a - hi.astype(F32)).astype(BF16)
    return hi, lo


_INPROJ_OUTS = ("qd", "kdT_f", "kdT_b", "vd_f", "vd_b", "qs", "ksT_f", "ksT_b", "vsT_f", "vs_b",
                "qi", "kiT_f", "ki2T_b", "wi", "gates")
_INPROJ_NATURAL = ("kd_b", "ks_b", "ki_b")


def _inproj_kernel(x_ref, cos_ref, sin_ref, gmix_ref, wqkv_ref, wsm_ref, wgate_ref,
                   gqd_ref, gkd_ref, gqs_ref, gks_ref, bd_ref, *out_refs,
                   names, w_diff, w_dsa, w_idx, h_idx, qk_scale):
    o = dict(zip(names, out_refs))
    x = x_ref[...]
    tm = x.shape[0]
    ms = jnp.mean(x * x, axis=-1, keepdims=True)
    h = ((x * lax.rsqrt(ms + EPS)) * gmix_ref[...]).astype(BF16)

    cos = cos_ref[...]
    sin = sin_ref[...]
    lane = lax.broadcasted_iota(I32, (tm, LANES), 1)
    low_half = (lane & (HEAD_DIM // 2)) == 0

    def rope(y):
        partner = jnp.where(low_half, pltpu.roll(y, LANES - HEAD_DIM // 2, 1), pltpu.roll(y, HEAD_DIM // 2, 1))
        return y * cos + partner * sin

    bd = bd_ref[...]

    def head_norm(p, g_ref):
        hi, lo = _split_bf16(p * p)
        width = bd.shape[0]
        parts = []
        for c in range(0, p.shape[1], width):
            parts.append(_dot(hi[:, c:c + width], bd) + _dot(lo[:, c:c + width], bd))
        ss = jnp.concatenate(parts, axis=1) if len(parts) > 1 else parts[0]
        return (p * lax.rsqrt(ss * (1.0 / HEAD_DIM) + EPS)) * g_ref[...]

    def seg(c0, width):
        return _dot(h, wqkv_ref[:, c0:c0 + width])

    def rope_all(y, scale=None):
        outs = []
        for c in range(0, y.shape[1], LANES):
            r = rope(y[:, c:c + LANES])
            outs.append(r if scale is None else r * scale)
        return jnp.concatenate(outs, axis=1) if len(outs) > 1 else outs[0]

    c0 = 0
    o["qd"][...] = rope_all(head_norm(seg(c0, w_diff), gqd_ref), qk_scale * LOG2E).astype(BF16)
    c0 += w_diff
    kd = rope_all(head_norm(seg(c0, w_diff), gkd_ref))
    kd_t = kd.T
    o["kdT_f"][...] = kd_t
    o["kdT_b"][...] = kd_t.astype(BF16)
    if "kd_b" in o:
        o["kd_b"][...] = kd.astype(BF16)
    c0 += w_diff
    vd = seg(c0, w_diff)
    o["vd_f"][...] = vd
    o["vd_b"][...] = vd.astype(BF16)
    c0 += w_diff
    o["qs"][...] = rope_all(head_norm(seg(c0, w_dsa), gqs_ref), qk_scale * LOG2E).astype(BF16)
    c0 += w_dsa
    ks = rope_all(head_norm(seg(c0, w_dsa), gks_ref))
    ks_t = ks.T
    o["ksT_f"][...] = ks_t
    o["ksT_b"][...] = ks_t.astype(BF16)
    if "ks_b" in o:
        o["ks_b"][...] = ks.astype(BF16)
    c0 += w_dsa
    vs = seg(c0, w_dsa)
    o["vsT_f"][...] = vs.T
    o["vs_b"][...] = vs.astype(BF16)
    c0 += w_dsa
    o["qi"][...] = rope_all(seg(c0, w_idx), qk_scale).astype(BF16)

    sm = _dot(h, wsm_ref[...])
    ki = rope(sm)
    ki2 = jnp.where(lane < HEAD_DIM, ki, pltpu.roll(ki, HEAD_DIM, 1))
    ki2_t = ki2.T
    o["kiT_f"][...] = ki2_t[:HEAD_DIM]
    o["ki2T_b"][...] = ki2_t.astype(BF16)
    if "ki_b" in o:
        o["ki_b"][...] = ki2.astype(BF16)
    o["wi"][...] = sm * (h_idx ** -0.5)

    g = _dot(h, wgate_ref[...])
    o["gates"][...] = (1.0 / (1.0 + jnp.exp(-g))).astype(BF16)


def _inproj(x, cos_t, sin_t, bsz, tm, kc, prm, natural_k):
    n, d = x.shape
    t = n // bsz
    tiles = t // tm
    per_chunk = kc // tm
    w_diff, w_dsa, w_idx, h_idx = prm["w_diff"], prm["w_dsa"], prm["w_idx"], prm["h_idx"]
    wg = prm["w_gate"].shape[1]
    row = lambda i: (i, 0)
    const = lambda i: (0, 0)
    pos = lambda i: (i % tiles, 0)
    full = lambda a: pl.BlockSpec(a.shape, const)
    in_specs = [pl.BlockSpec((tm, d), row), pl.BlockSpec((tm, LANES), pos), pl.BlockSpec((tm, LANES), pos),
                full(prm["g_mix"]), full(prm["w_qkv"]), full(prm["w_small"]), full(prm["w_gate"]),
                full(prm["g_qd"]), full(prm["g_kd"]), full(prm["g_qs"]), full(prm["g_ks"]), full(prm["bd"])]

    def nat(w, dt):
        return jax.ShapeDtypeStruct((n, w), dt), pl.BlockSpec((tm, w), row)

    def tr_f(w):
        return (jax.ShapeDtypeStruct((bsz, w, t), F32),
                pl.BlockSpec((None, w, tm), lambda i: (i // tiles, 0, i % tiles)))

    def tr_b(w):
        return (jax.ShapeDtypeStruct((bsz, t // kc, w, kc), BF16),
                pl.BlockSpec((None, None, w, tm),
                             lambda i: (i // tiles, (i % tiles) // per_chunk, 0, (i % tiles) % per_chunk)))

    table = dict(qd=nat(w_diff, BF16), kdT_f=tr_f(w_diff), kdT_b=tr_b(w_diff), vd_f=nat(w_diff, F32),
                 vd_b=nat(w_diff, BF16), qs=nat(w_dsa, BF16), ksT_f=tr_f(w_dsa), ksT_b=tr_b(w_dsa),
                 vsT_f=tr_f(w_dsa), vs_b=nat(w_dsa, BF16), qi=nat(w_idx, BF16), kiT_f=tr_f(HEAD_DIM),
                 ki2T_b=tr_b(LANES), wi=nat(LANES, F32), gates=nat(wg, BF16),
                 kd_b=nat(w_diff, BF16), ks_b=nat(w_dsa, BF16), ki_b=nat(LANES, BF16))
    names = _INPROJ_OUTS + (_INPROJ_NATURAL if natural_k else ())
    kern = functools.partial(_inproj_kernel, names=names, w_diff=w_diff, w_dsa=w_dsa, w_idx=w_idx, h_idx=h_idx,
                             qk_scale=HEAD_DIM ** -0.5)
    outs = pl.pallas_call(kern, grid=(n // tm,), in_specs=in_specs,
                          out_specs=[table[k][1] for k in names], out_shape=[table[k][0] for k in names],
                          compiler_params=_params("parallel"), name="inproj")(
        x, cos_t, sin_t, prm["g_mix"], prm["w_qkv"], prm["w_small"], prm["w_gate"],
        prm["g_qd"], prm["g_kd"], prm["g_qs"], prm["g_ks"], prm["bd"])
    return dict(zip(names, outs))


def _stack_halves(q):
    lane = lax.broadcasted_iota(I32, q.shape, 1)
    zero = jnp.zeros_like(q)
    return jnp.concatenate([jnp.where(lane < HEAD_DIM, q, zero), jnp.where(lane >= HEAD_DIM, q, zero)], axis=0)


def _lane_fold(a):
    out = a[:, :LANES]
    for c in range(LANES, a.shape[1], LANES):
        out = out + a[:, c:c + LANES]
    return out


def _online_update(s_ref, v, m_sc, l_sc, acc_sc, fin=None):
    s = s_ref[...]
    if fin is not None:
        s = fin(s, 0)
    m_prev = m_sc[...]
    m_new = jnp.maximum(m_prev, jnp.max(s, axis=-1, keepdims=True))
    alpha = jnp.exp2(m_prev - m_new)
    p = jnp.exp2(s - jnp.tile(m_new, (1, s.shape[1] // LANES)))
    l_sc[...] = alpha * l_sc[...] + _lane_fold(p)
    acc_sc[...] = alpha * acc_sc[...] + _dot(p.astype(BF16), v)
    m_sc[...] = m_new


def _finish(l_sc, acc_sc):
    return acc_sc[...] * (1.0 / jnp.sum(l_sc[...], axis=-1, keepdims=True))


def _attend_chunks(n, score, value, s_a, s_b, m_sc, l_sc, acc_sc, last=None):
    s_a[...] = score(0)

    def pair(p, carry):
        j = 2 * p
        s_b[...] = score(j + 1)
        _online_update(s_a, value(j), m_sc, l_sc, acc_sc)
        s_a[...] = score(j + 2)
        _online_update(s_b, value(j + 1), m_sc, l_sc, acc_sc)
        return carry

    lax.fori_loop(0, (n - 1) // 2, pair, 0)

    @pl.when(n % 2 == 1)
    def _():
        _online_update(s_a, value(n - 1), m_sc, l_sc, acc_sc, last)

    @pl.when(n % 2 == 0)
    def _():
        s_b[...] = score(n - 1)
        _online_update(s_a, value(n - 2), m_sc, l_sc, acc_sc)
        _online_update(s_b, value(n - 1), m_sc, l_sc, acc_sc, last)


def _lambda(lam_ref, lam_init):
    a = jnp.sum(lam_ref[0:1, :] * lam_ref[1:2, :], axis=-1, keepdims=True)
    b = jnp.sum(lam_ref[2:3, :] * lam_ref[3:4, :], axis=-1, keepdims=True)
    return jnp.exp(a) - jnp.exp(b) + lam_init


def _attn_scratch(rows, kc):
    return [pltpu.VMEM((rows, LANES), F32)] * 3 + [pltpu.VMEM((rows, kc), F32)] * 2


def _diff_attn_kernel(lam_ref, gsub_ref, q_ref, kt_ref, v_ref, o_ref, m_sc, l_sc, acc_sc, s_a, s_b, *,
                      tq, kc, lam_init):
    q0 = pl.program_id(2) * tq
    n_c = q0 // kc + 1
    q2 = _stack_halves(q_ref[...])
    m_sc[...] = jnp.full(m_sc.shape, -jnp.inf, F32)
    l_sc[...] = jnp.zeros(l_sc.shape, F32)
    acc_sc[...] = jnp.zeros(acc_sc.shape, F32)

    def causal(s, row0):
        r = lax.broadcasted_iota(I32, s.shape, 0) + row0
        r = jnp.where(r >= tq, r - tq, r) + q0
        c = lax.broadcasted_iota(I32, s.shape, 1) + (n_c - 1) * kc
        return jnp.where(c <= r, s, -jnp.inf)

    _attend_chunks(n_c, lambda j: _dot(q2, kt_ref[j]),
                   lambda j: v_ref[pl.ds(pl.multiple_of(j * kc, kc), kc), :],
                   s_a, s_b, m_sc, l_sc, acc_sc, last=causal)

    a = _finish(l_sc, acc_sc)
    o = a[:tq] - _lambda(lam_ref, lam_init) * a[tq:]
    ms = jnp.mean(o * o, axis=-1, keepdims=True)
    on = ((o * lax.rsqrt(ms + EPS)) * gsub_ref[...]) * (1.0 - lam_init)
    o_ref[...] = on.astype(BF16)


def _diff_attn(q, kt, v, lam_vecs, g_sub, lam_init, tq):
    b, t, w = q.shape
    nc, kc = kt.shape[1], kt.shape[3]
    assert kc % tq == 0
    grid = (b, w // LANES, t // tq)
    qmap = lambda bi, hi, qi: (bi, qi, hi)
    const = lambda bi, hi, qi: (0, 0)
    kern = functools.partial(_diff_attn_kernel, tq=tq, kc=kc, lam_init=lam_init)
    return pl.pallas_call(
        kern, grid=grid,
        in_specs=[pl.BlockSpec(lam_vecs.shape, const), pl.BlockSpec(g_sub.shape, const),
                  pl.BlockSpec((None, tq, LANES), qmap),
                  pl.BlockSpec((None, nc, LANES, kc), lambda bi, hi, qi: (bi, 0, hi, 0)),
                  pl.BlockSpec((None, t, LANES), lambda bi, hi, qi: (bi, 0, hi))],
        out_specs=pl.BlockSpec((None, tq, LANES), qmap),
        out_shape=jax.ShapeDtypeStruct((b, t, w), BF16),
        scratch_shapes=_attn_scratch(2 * tq, kc),
        compiler_params=_params("parallel", "parallel", "arbitrary"), name="diff_attn")(lam_vecs, g_sub, q, kt, v)


def _key_to_f32(k):
    b = k ^ (lax.shift_right_arithmetic(k, 31) & 0x7FFFFFFF)
    return lax.bitcast_convert_type(b, F32)


def _select_bias(score_sc, bias_sc, n_c, k_sel, row_pos, col0):
    _, r, kc = score_sc.shape

    def count(pred):
        def body(c, a):
            return a + _lane_fold(jnp.where(pred(score_sc[c]), 1.0, 0.0))
        a = lax.fori_loop(0, n_c, body, jnp.zeros((r, LANES), F32))
        return jnp.sum(a, axis=-1, keepdims=True)

    def bit_body(i, t):
        cand = t + lax.shift_left(jnp.int32(1), 31 - i)
        cand_f = _key_to_f32(cand)
        cnt = count(lambda sc: sc >= cand_f)
        return jnp.where(cnt >= k_sel, cand, t)

    t = lax.fori_loop(0, 32, bit_body, jnp.full((r, 1), INT_MIN, I32))
    thr = _key_to_f32(jnp.maximum(t, KEY_NEG_INF))
    need = k_sel - count(lambda sc: sc > thr)

    ri = lax.broadcasted_iota(I32, (kc, kc), 0)
    ci = lax.broadcasted_iota(I32, (kc, kc), 1)
    tri = jnp.where(ri <= ci, 1.0, 0.0).astype(BF16)
    col = lax.broadcasted_iota(I32, (r, kc), 1)

    def bias_body(c, run):
        sc = score_sc[c]
        eq = sc == thr
        pre = _dot(jnp.where(eq, 1.0, 0.0).astype(BF16), tri)
        sel = (sc > thr) | (eq & ((pre + run) <= need))
        causal = (col + (col0 + c * kc)) <= row_pos
        bias_sc[c] = jnp.where(sel & causal, 0.0, MASK_NEG)
        return run + pre[:, kc - 1:kc]

    lax.fori_loop(0, n_c, bias_body, jnp.zeros((r, 1), F32))


def _dsa_attn_kernel(qi_ref, ki2t_ref, wi_ref, qs_ref, kst_ref, vs_ref, o_ref,
                     qstack_sc, score_sc, bias_sc, m_sc, l_sc, acc_sc, s_a, s_b, *, tq, kc, k_sel, h_idx):
    qb = pl.program_id(1)
    q0 = qb * tq
    n_c = q0 // kc + 1
    row_pos = q0 + lax.broadcasted_iota(I32, (tq, 1), 0)

    for hh in range(h_idx):
        pair = qi_ref[:, (hh // 2) * LANES:(hh // 2 + 1) * LANES]
        lane = lax.broadcasted_iota(I32, pair.shape, 1)
        keep = (lane < HEAD_DIM) if hh % 2 == 0 else (lane >= HEAD_DIM)
        qstack_sc[hh * tq:(hh + 1) * tq, :] = jnp.where(keep, pair, jnp.zeros_like(pair))
    wi = wi_ref[...]

    def score_body(c, carry):
        d = _dot(qstack_sc[...], ki2t_ref[c])
        acc = jnp.zeros((tq, kc), F32)
        for hh in range(h_idx):
            acc = acc + jnp.maximum(d[hh * tq:(hh + 1) * tq], 0.0) * wi[:, HEAD_DIM + hh:HEAD_DIM + hh + 1]
        col = c * kc + lax.broadcasted_iota(I32, (tq, kc), 1)
        score_sc[c] = jnp.where(col <= row_pos, acc, -jnp.inf)
        return carry

    lax.fori_loop(0, n_c, score_body, 0)
    _select_bias(score_sc, bias_sc, n_c, k_sel, row_pos, 0)

    for hp in range(qs_ref.shape[1] // LANES):
        q2 = _stack_halves(qs_ref[:, hp * LANES:(hp + 1) * LANES])
        m_sc[...] = jnp.full(m_sc.shape, MASK_NEG, F32)
        l_sc[...] = jnp.zeros(l_sc.shape, F32)
        acc_sc[...] = jnp.zeros(acc_sc.shape, F32)

        def score(c, q2=q2, hp=hp):
            bias = bias_sc[c]
            return _dot(q2, kst_ref[c, hp * LANES:(hp + 1) * LANES, :]) + jnp.concatenate([bias, bias], axis=0)

        def value(c, hp=hp):
            return vs_ref[pl.ds(pl.multiple_of(c * kc, kc), kc), hp * LANES:(hp + 1) * LANES]

        _attend_chunks(n_c, score, value, s_a, s_b, m_sc, l_sc, acc_sc)
        a = _finish(l_sc, acc_sc)
        lane = lax.broadcasted_iota(I32, (tq, LANES), 1)
        o_ref[:, hp * LANES:(hp + 1) * LANES] = jnp.where(lane < HEAD_DIM, a[:tq], a[tq:]).astype(BF16)


def _dsa_attn(qi, ki2t, wi, qs, kst, vs, k_sel, h_idx, tq):
    b, t, w = qs.shape
    nc, kc = kst.shape[1], kst.shape[3]
    assert kc % tq == 0 and kc >= k_sel
    qmap = lambda bi, qb: (bi, qb, 0)
    kern = functools.partial(_dsa_attn_kernel, tq=tq, kc=kc, k_sel=k_sel, h_idx=h_idx)
    return pl.pallas_call(
        kern, grid=(b, t // tq),
        in_specs=[pl.BlockSpec((None, tq, qi.shape[2]), qmap),
                  pl.BlockSpec((None, nc, LANES, kc), lambda bi, qb: (bi, 0, 0, 0)),
                  pl.BlockSpec((None, tq, LANES), qmap), pl.BlockSpec((None, tq, w), qmap),
                  pl.BlockSpec((None, nc, w, kc), lambda bi, qb: (bi, 0, 0, 0)),
                  pl.BlockSpec((None, t, w), lambda bi, qb: (bi, 0, 0))],
        out_specs=pl.BlockSpec((None, tq, w), qmap),
        out_shape=jax.ShapeDtypeStruct((b, t, w), BF16),
        scratch_shapes=[pltpu.VMEM((h_idx * tq, LANES), BF16), pltpu.VMEM((nc, tq, kc), F32),
                        pltpu.VMEM((nc, tq, kc), F32)] + _attn_scratch(2 * tq, kc),
        compiler_params=_params("parallel", "arbitrary"), name="dsa_attn")(qi, ki2t, wi, qs, kst, vs)


def _page_specs(n_pages, block):
    def spec(j):
        return pl.BlockSpec((None,) + block, lambda b, pt, j=j: (pt[b, j],) + (0,) * len(block))
    return [spec(j) for j in range(n_pages)]


def _idx_score_kernel(pt_ref, q_ref, w_ref, knew_ref, *refs, n_pages, page, pad):
    page_refs, o_ref = refs[:n_pages], refs[n_pages]
    q = q_ref[...]
    w = w_ref[...]
    for j in range(n_pages):
        d = _dot(q, page_refs[j][...].astype(BF16))
        o_ref[:, j * page:(j + 1) * page] = jnp.sum(jnp.maximum(d, 0.0) * w, axis=0, keepdims=True)
    d_new = jnp.sum(q.astype(F32) * knew_ref[...].astype(F32), axis=-1, keepdims=True)
    s_new = jnp.sum(jnp.maximum(d_new, 0.0) * w, axis=0, keepdims=True)
    lane = lax.broadcasted_iota(I32, (1, pad), 1)
    o_ref[:, n_pages * page:] = jnp.where(lane == 0, s_new, -jnp.inf)


def _idx_scores(page_table, q_i, w_i, k_new, cache_t, pad):
    db, n_pages = page_table.shape
    _, dh, page = cache_t.shape
    h_idx = q_i.shape[1]
    tk = n_pages * page + pad
    seq = lambda b, pt: (b, 0, 0)
    kern = functools.partial(_idx_score_kernel, n_pages=n_pages, page=page, pad=pad)
    grid_spec = pltpu.PrefetchScalarGridSpec(
        num_scalar_prefetch=1, grid=(db,),
        in_specs=[pl.BlockSpec((None, h_idx, dh), seq), pl.BlockSpec((None, h_idx, 1), seq),
                  pl.BlockSpec((None, 1, dh), seq)] + _page_specs(n_pages, (dh, page)),
        out_specs=pl.BlockSpec((None, 1, tk), seq))
    out = pl.pallas_call(kern, grid_spec=grid_spec, out_shape=jax.ShapeDtypeStruct((db, 1, tk), F32),
                         compiler_params=_params("parallel"), name="idx_scores")(
        page_table, q_i, w_i, k_new, *([cache_t] * n_pages))
    return out.reshape(db, tk)


def _select_kernel(s_ref, o_ref, score_sc, bias_sc, *, kc, k_sel, q_pos):
    r = s_ref.shape[0]
    nc = s_ref.shape[1] // kc
    for c in range(nc):
        score_sc[c] = s_ref[:, c * kc:(c + 1) * kc]
    row_pos = jnp.full((r, 1), q_pos, I32)
    _select_bias(score_sc, bias_sc, nc, k_sel, row_pos, 0)
    for c in range(nc):
        o_ref[:, c * kc:(c + 1) * kc] = bias_sc[c]


def _select(scores, k_sel, q_pos, kc):
    r, tk = scores.shape
    nc = tk // kc
    kern = functools.partial(_select_kernel, kc=kc, k_sel=k_sel, q_pos=q_pos)
    return pl.pallas_call(
        kern, grid=(1,), in_specs=[pl.BlockSpec((r, tk), lambda i: (0, 0))],
        out_specs=pl.BlockSpec((r, tk), lambda i: (0, 0)), out_shape=jax.ShapeDtypeStruct((r, tk), F32),
        scratch_shapes=[pltpu.VMEM((nc, r, kc), F32), pltpu.VMEM((nc, r, kc), F32)],
        compiler_params=_params("arbitrary"), name="select")(scores)


def _block_diag_rows(q, width):
    t = jnp.concatenate([q] * (width // HEAD_DIM), axis=1)
    row = lax.broadcasted_iota(I32, t.shape, 0)
    col = lax.broadcasted_iota(I32, t.shape, 1)
    return jnp.where((col >= row * HEAD_DIM) & (col < (row + 1) * HEAD_DIM), t, jnp.zeros_like(t))


def _decode_attn_kernel(pt_ref, lam_ref, gsub_ref, qd_ref, kdn_ref, vdn_ref, qs_ref, ksn_ref, vsn_ref, bias_ref,
                        *refs, n_pages, page, lam_init):
    kd_pages = refs[:n_pages]
    vd_pages = refs[n_pages:2 * n_pages]
    ks_pages = refs[2 * n_pages:3 * n_pages]
    vs_pages = refs[3 * n_pages:4 * n_pages]
    od_ref, os_ref = refs[4 * n_pages], refs[4 * n_pages + 1]
    h_diff = vd_pages[0].shape[0] // page

    def probs(q_ref, knew_ref, k_pages, bias):
        q = q_ref[...]
        qbd = _block_diag_rows(q, k_pages[0].shape[0])
        s = [_dot(qbd, k_pages[j][...].astype(BF16)) for j in range(n_pages)]
        s_new = jnp.sum(q.astype(F32) * knew_ref[...].astype(F32), axis=-1, keepdims=True)
        if bias is not None:
            s = [s[j] + bias[:, j * page:(j + 1) * page] for j in range(n_pages)]
            s_new = s_new + bias[:, n_pages * page:n_pages * page + 1]
        m = s_new
        for j in range(n_pages):
            m = jnp.maximum(m, jnp.max(s[j], axis=-1, keepdims=True))
        p_new = jnp.exp2(s_new - m)
        p = [jnp.exp2(s[j] - m) for j in range(n_pages)]
        l = p_new
        for j in range(n_pages):
            l = l + jnp.sum(p[j], axis=-1, keepdims=True)
        return p, p_new, 1.0 / l

    p, p_new, inv_l = probs(qd_ref, kdn_ref, kd_pages, None)
    acc = p_new * vdn_ref[...].astype(F32)
    for j in range(n_pages):
        v = jnp.concatenate([vd_pages[j][pl.ds(hh, page, stride=h_diff), :] for hh in range(h_diff)], axis=1)
        acc = acc + _dot(p[j].astype(BF16), v.astype(BF16))
    a = acc * inv_l
    row = lax.broadcasted_iota(I32, a.shape, 0)
    col = lax.broadcasted_iota(I32, a.shape, 1)
    own = (col // LANES) == (row // 2)
    coef = jnp.where(row % 2 == 0, 1.0, -_lambda(lam_ref, lam_init))
    o = jnp.sum(jnp.where(own, a * coef, 0.0), axis=0, keepdims=True)
    parts = []
    for c in range(0, o.shape[1], LANES):
        oc = o[:, c:c + LANES]
        ms = jnp.mean(oc * oc, axis=-1, keepdims=True)
        parts.append(((oc * lax.rsqrt(ms + EPS)) * gsub_ref[...]) * (1.0 - lam_init))
    od_ref[...] = jnp.concatenate(parts, axis=1).astype(BF16)

    p, p_new, inv_l = probs(qs_ref, ksn_ref, ks_pages, bias_ref[...])
    acc = p_new * vsn_ref[...].astype(F32)
    for j in range(n_pages):
        acc = acc + _dot_nt(p[j].astype(BF16), vs_pages[j][...].astype(BF16))
    a = acc * inv_l
    row = lax.broadcasted_iota(I32, a.shape, 0)
    col = lax.broadcasted_iota(I32, a.shape, 1)
    os_ref[...] = jnp.sum(jnp.where((col // HEAD_DIM) == row, a, 0.0), axis=0, keepdims=True).astype(BF16)


def _decode_attn(page_table, lam_vecs, g_sub, qd, kdn, vdn, qs, ksn, vsn, bias, caches, lam_init):
    db, n_pages = page_table.shape
    c_kd, c_vd, c_ks, c_vs = caches
    w_diff, page = c_kd.shape[1:]
    w_dsa = c_ks.shape[1]
    seq = lambda b, pt: (b, 0, 0)
    const = lambda b, pt: (0, 0)
    in_specs = [pl.BlockSpec(lam_vecs.shape, const), pl.BlockSpec(g_sub.shape, const),
                pl.BlockSpec((None,) + qd.shape[1:], seq), pl.BlockSpec((None,) + kdn.shape[1:], seq),
                pl.BlockSpec((None,) + vdn.shape[1:], seq),
                pl.BlockSpec((None,) + qs.shape[1:], seq), pl.BlockSpec((None,) + ksn.shape[1:], seq),
                pl.BlockSpec((None,) + vsn.shape[1:], seq), pl.BlockSpec((None,) + bias.shape[1:], seq)]
    for cache in caches:
        in_specs += _page_specs(n_pages, tuple(cache.shape[1:]))
    kern = functools.partial(_decode_attn_kernel, n_pages=n_pages, page=page, lam_init=lam_init)
    grid_spec = pltpu.PrefetchScalarGridSpec(
        num_scalar_prefetch=1, grid=(db,), in_specs=in_specs,
        out_specs=[pl.BlockSpec((None, 1, w_diff), seq), pl.BlockSpec((None, 1, w_dsa), seq)])
    pages = []
    for cache in caches:
        pages += [cache] * n_pages
    od, os_ = pl.pallas_call(
        kern, grid_spec=grid_spec,
        out_shape=[jax.ShapeDtypeStruct((db, 1, w_diff), BF16), jax.ShapeDtypeStruct((db, 1, w_dsa), BF16)],
        compiler_params=_params("parallel"), name="decode_attn")(
        page_table, lam_vecs, g_sub, qd, kdn, vdn, qs, ksn, vsn, bias, *pages)
    return od.reshape(db, w_diff), os_.reshape(db, w_dsa)


ROUTE_E1, ROUTE_E2, ROUTE_W1, ROUTE_W2, ROUTE_R1, ROUTE_R2 = range(6)


def _outproj_kernel(x_ref, od_ref, os_ref, gate_ref, wod_ref, wos_ref, wout_ref, gffn_ref, wrh_ref, wrl_ref,
                    x1_ref, h2_ref, comb_ref, route_ref, count_ref, *, n_groups, n_experts):
    d = x_ref.shape[1]

    @pl.when(pl.program_id(0) == 0)
    def _():
        count_ref[...] = jnp.zeros(count_ref.shape, F32)

    yd = _dot(od_ref[...], wod_ref[...])
    ys = _dot(os_ref[...], wos_ref[...])
    merged = gate_ref[:, :d].astype(F32) * yd + gate_ref[:, d:].astype(F32) * ys
    x1 = x_ref[...] + _dot(merged.astype(BF16), wout_ref[...])
    x1_ref[...] = x1

    ms = jnp.mean(x1 * x1, axis=-1, keepdims=True)
    h2 = (x1 * lax.rsqrt(ms + EPS)) * gffn_ref[...]
    h2_ref[...] = h2

    hi, lo = _split_bf16(h2)
    lg = _dot(hi, wrh_ref[...]) + (_dot(hi, wrl_ref[...]) + _dot(lo, wrh_ref[...]))
    lane = lax.broadcasted_iota(I32, lg.shape, 1)
    lanef = lane.astype(F32)
    big = float(LANES)
    epg = n_experts // n_groups

    gmask = (lane >= n_experts) & (lane < n_experts + n_groups)
    gl = jnp.where(gmask, lg, -jnp.inf)
    gmax = jnp.max(gl, axis=-1, keepdims=True)
    gsel = jnp.min(jnp.where(gl == gmax, lanef, big), axis=-1, keepdims=True) - n_experts
    g_w = 1.0 / jnp.sum(jnp.exp(gl - gmax), axis=-1, keepdims=True)

    lo_lane = gsel * epg
    emask = (lanef >= lo_lane) & (lanef < lo_lane + epg)
    el = jnp.where(emask, lg, -jnp.inf)
    v1 = jnp.max(el, axis=-1, keepdims=True)
    i1 = jnp.min(jnp.where(el == v1, lanef, big), axis=-1, keepdims=True)
    el2 = jnp.where(lanef == i1, -jnp.inf, el)
    v2 = jnp.max(el2, axis=-1, keepdims=True)
    i2 = jnp.min(jnp.where(el2 == v2, lanef, big), axis=-1, keepdims=True)
    e21 = jnp.exp(v2 - v1)
    den = 1.0 / (1.0 + e21)
    w1 = den * g_w
    w2 = (e21 * den) * g_w
    comb_ref[...] = jnp.where(lanef == i1, w1, 0.0) + jnp.where(lanef == i2, w2, 0.0)

    tm = lg.shape[0]
    onehot = jnp.where((lanef == i1) | (lanef == i2), 1.0, 0.0)
    ri = lax.broadcasted_iota(I32, (tm, tm), 0)
    ci = lax.broadcasted_iota(I32, (tm, tm), 1)
    before = jnp.where(ci < ri, 1.0, 0.0).astype(BF16)
    rank_e = count_ref[...] + _dot(before, onehot.astype(BF16))
    r1 = jnp.sum(jnp.where(lanef == i1, rank_e, 0.0), axis=-1, keepdims=True)
    r2 = jnp.sum(jnp.where(lanef == i2, rank_e, 0.0), axis=-1, keepdims=True)
    count_ref[...] += jnp.sum(onehot, axis=0, keepdims=True)
    rec = jnp.zeros(lg.shape, F32)
    for slot, val in ((ROUTE_E1, i1), (ROUTE_E2, i2), (ROUTE_W1, w1), (ROUTE_W2, w2), (ROUTE_R1, r1), (ROUTE_R2, r2)):
        rec = jnp.where(lane == slot, val, rec)
    route_ref[...] = rec


def _outproj(x, od, os_, gates, prm, tm):
    n, d = x.shape
    row = lambda i: (i, 0)
    const = lambda i: (0, 0)
    full = lambda a: pl.BlockSpec(a.shape, const)
    kern = functools.partial(_outproj_kernel, n_groups=prm["n_groups"], n_experts=prm["n_experts"])
    return pl.pallas_call(
        kern, grid=(n // tm,),
        in_specs=[pl.BlockSpec((tm, d), row), pl.BlockSpec((tm, od.shape[1]), row), pl.BlockSpec((tm, os_.shape[1]), row),
                  pl.BlockSpec((tm, gates.shape[1]), row), full(prm["w_od"]), full(prm["w_os"]), full(prm["w_out"]),
                  full(prm["g_ffn"]), full(prm["w_rt_hi"]), full(prm["w_rt_lo"])],
        out_specs=[pl.BlockSpec((tm, d), row), pl.BlockSpec((tm, d), row), pl.BlockSpec((tm, LANES), row),
                   pl.BlockSpec((tm, LANES), row), pl.BlockSpec((1, LANES), const)],
        out_shape=[jax.ShapeDtypeStruct((n, d), F32), jax.ShapeDtypeStruct((n, d), F32),
                   jax.ShapeDtypeStruct((n, LANES), F32), jax.ShapeDtypeStruct((n, LANES), F32),
                   jax.ShapeDtypeStruct((1, LANES), F32)],
        compiler_params=_params("arbitrary"), name="outproj")(
        x, od, os_, gates, prm["w_od"], prm["w_os"], prm["w_out"], prm["g_ffn"], prm["w_rt_hi"], prm["w_rt_lo"])


def _expert_ffn(h, wg_ref, wu_ref, wd_ref):
    h = h.astype(BF16)
    a = _dot(h, wg_ref[...].astype(BF16))
    u = _dot(h, wu_ref[...].astype(BF16))
    act = (a * (1.0 / (1.0 + jnp.exp(-a)))) * u
    return _dot(act.astype(BF16), wd_ref[...].astype(BF16))


def _moe_kernel(x1_ref, h2_ref, comb_ref, wg_ref, wu_ref, wd_ref, y_ref, acc_sc):
    e = pl.program_id(1)

    @pl.when(e == 0)
    def _():
        acc_sc[...] = jnp.zeros(acc_sc.shape, F32)

    o = _expert_ffn(h2_ref[...], wg_ref, wu_ref, wd_ref)
    comb = comb_ref[...]
    lane = lax.broadcasted_iota(I32, comb.shape, 1)
    c = jnp.sum(jnp.where(lane == e, comb, 0.0), axis=-1, keepdims=True)
    acc_sc[...] += o * c

    @pl.when(e == pl.num_programs(1) - 1)
    def _():
        y_ref[...] = x1_ref[...] + acc_sc[...]


def _moe(x1, h2, comb, w_gate, w_up, w_down, tm):
    n, d = x1.shape
    ne, _, de = w_gate.shape
    row = lambda i, e: (i, 0)
    return pl.pallas_call(
        _moe_kernel, grid=(n // tm, ne),
        in_specs=[pl.BlockSpec((tm, d), row), pl.BlockSpec((tm, d), row), pl.BlockSpec((tm, LANES), row),
                  pl.BlockSpec((None, d, de), lambda i, e: (e, 0, 0)), pl.BlockSpec((None, d, de), lambda i, e: (e, 0, 0)),
                  pl.BlockSpec((None, de, d), lambda i, e: (e, 0, 0))],
        out_specs=pl.BlockSpec((tm, d), row), out_shape=jax.ShapeDtypeStruct((n, d), F32),
        scratch_shapes=[pltpu.VMEM((tm, d), F32)],
        compiler_params=_params("parallel", "arbitrary"), name="moe")(x1, h2, comb, w_gate, w_up, w_down)


def _row_copy(src, src_row, dst, dst_row, sem):
    return pltpu.make_async_copy(src.at[pl.ds(src_row, 1), :], dst.at[pl.ds(dst_row, 1), :], sem)


def _scatter_kernel(pos1_ref, pos2_ref, h2_hbm, xs_init_hbm, xs_hbm, sem, *, tm):
    del xs_init_hbm
    base = pl.program_id(0) * tm

    def start(t, carry):
        _row_copy(h2_hbm, base + t, xs_hbm, pos1_ref[base + t], sem).start()
        _row_copy(h2_hbm, base + t, xs_hbm, pos2_ref[base + t], sem).start()
        return carry

    def wait(t, carry):
        _row_copy(h2_hbm, base + t, xs_hbm, pos1_ref[base + t], sem).wait()
        _row_copy(h2_hbm, base + t, xs_hbm, pos2_ref[base + t], sem).wait()
        return carry

    lax.fori_loop(0, tm, start, 0)
    lax.fori_loop(0, tm, wait, 0)


def _scatter_rows(pos1, pos2, h2, n_rows, tm):
    n, d = h2.shape
    any_spec = pl.BlockSpec(memory_space=pl.ANY)
    grid_spec = pltpu.PrefetchScalarGridSpec(
        num_scalar_prefetch=2, grid=(n // tm,), in_specs=[any_spec, any_spec], out_specs=any_spec,
        scratch_shapes=[pltpu.SemaphoreType.DMA(())])
    return pl.pallas_call(
        functools.partial(_scatter_kernel, tm=tm), grid_spec=grid_spec,
        out_shape=jax.ShapeDtypeStruct((n_rows, d), F32), input_output_aliases={3: 0},
        compiler_params=_params("arbitrary"), name="moe_scatter")(pos1, pos2, h2, jnp.zeros((n_rows, d), F32))


def _grouped_ffn_kernel(tile_expert_ref, n_tiles_ref, xs_ref, wg_ref, wu_ref, wd_ref, ys_ref):
    used = pl.program_id(0) < n_tiles_ref[0]

    @pl.when(used)
    def _():
        ys_ref[...] = _expert_ffn(xs_ref[...], wg_ref, wu_ref, wd_ref)

    @pl.when(jnp.logical_not(used))
    def _():
        ys_ref[...] = jnp.zeros(ys_ref.shape, F32)


def _grouped_ffn(tile_expert, n_tiles, xs, w_gate, w_up, w_down, tr):
    rows, d = xs.shape
    ne, _, de = w_gate.shape
    last = lambda i, te, nt: jnp.minimum(i, nt[0] - 1)
    xmap = lambda i, te, nt: (last(i, te, nt), 0)
    wmap = lambda i, te, nt: (te[last(i, te, nt)], 0, 0)
    grid_spec = pltpu.PrefetchScalarGridSpec(
        num_scalar_prefetch=2, grid=(rows // tr,),
        in_specs=[pl.BlockSpec((tr, d), xmap), pl.BlockSpec((None, d, de), wmap), pl.BlockSpec((None, d, de), wmap),
                  pl.BlockSpec((None, de, d), wmap)],
        out_specs=pl.BlockSpec((tr, d), lambda i, te, nt: (i, 0)))
    return pl.pallas_call(
        _grouped_ffn_kernel, grid_spec=grid_spec, out_shape=jax.ShapeDtypeStruct((rows, d), F32),
        compiler_params=_params("arbitrary"), name="moe_ffn")(tile_expert, n_tiles, xs, w_gate, w_up, w_down)


def _combine_kernel(pos1_ref, pos2_ref, x1_ref, route_ref, ys_hbm, y_ref, buf1, buf2, sem, *, tm):
    base = pl.program_id(0) * tm

    def start(t, carry):
        _row_copy(ys_hbm, pos1_ref[base + t], buf1, t, sem).start()
        _row_copy(ys_hbm, pos2_ref[base + t], buf2, t, sem).start()
        return carry

    def wait(t, carry):
        _row_copy(ys_hbm, pos1_ref[base + t], buf1, t, sem).wait()
        _row_copy(ys_hbm, pos2_ref[base + t], buf2, t, sem).wait()
        return carry

    lax.fori_loop(0, tm, start, 0)
    lax.fori_loop(0, tm, wait, 0)
    route = route_ref[...]
    w1 = route[:, ROUTE_W1:ROUTE_W1 + 1]
    w2 = route[:, ROUTE_W2:ROUTE_W2 + 1]
    y_ref[...] = (x1_ref[...] + buf1[...] * w1) + buf2[...] * w2


def _combine_rows(pos1, pos2, x1, route, ys, tm):
    n, d = x1.shape
    row = lambda i, p1, p2: (i, 0)
    grid_spec = pltpu.PrefetchScalarGridSpec(
        num_scalar_prefetch=2, grid=(n // tm,),
        in_specs=[pl.BlockSpec((tm, d), row), pl.BlockSpec((tm, LANES), row), pl.BlockSpec(memory_space=pl.ANY)],
        out_specs=pl.BlockSpec((tm, d), row),
        scratch_shapes=[pltpu.VMEM((tm, d), F32), pltpu.VMEM((tm, d), F32), pltpu.SemaphoreType.DMA(())])
    return pl.pallas_call(
        functools.partial(_combine_kernel, tm=tm), grid_spec=grid_spec, out_shape=jax.ShapeDtypeStruct((n, d), F32),
        compiler_params=_params("arbitrary"), name="moe_combine")(pos1, pos2, x1, route, ys)


def _routed_moe(x1, h2, route, counts, w_gate, w_up, w_down, n_experts, tm, tr):
    n, d = x1.shape
    cnt = counts[0, :n_experts].astype(I32)
    group = ((cnt + tr - 1) // tr) * tr
    ends = jnp.cumsum(group)
    offs = ends - group
    col = lambda j: route[:, j].astype(I32)
    pos1 = offs[col(ROUTE_E1)] + col(ROUTE_R1)
    pos2 = offs[col(ROUTE_E2)] + col(ROUTE_R2)
    n_rows = 2 * n + n_experts * tr
    tile_start = jnp.arange(n_rows // tr, dtype=I32) * tr
    tile_expert = jnp.minimum(jnp.sum(tile_start[:, None] >= ends[None, :], axis=1), n_experts - 1).astype(I32)
    n_tiles = (ends[-1:] // tr).astype(I32)
    xs = _scatter_rows(pos1, pos2, h2, n_rows, tm)
    ys = _grouped_ffn(tile_expert, n_tiles, xs, w_gate, w_up, w_down, tr)
    return _combine_rows(pos1, pos2, x1, route, ys, tm)


def _rope_tables(pos):
    half = HEAD_DIM // 2
    inv_freq = jnp.power(ROPE_THETA, -jnp.arange(half, dtype=F32) / half)
    ang = pos.astype(F32)[:, None] * inv_freq[None, :]
    cos, sin = jnp.cos(ang), jnp.sin(ang)
    reps = LANES // HEAD_DIM
    return jnp.tile(cos, (1, 2 * reps)), jnp.tile(jnp.concatenate([-sin, sin], axis=1), (1, reps))


def _tile_rows(n, target):
    t = min(n, target)
    while n % t:
        t //= 2
    return t


def _key_outputs(o, bsz, t, h_diff, h_dsa, dh):
    kd = o["kdT_f"].reshape(bsz, h_diff, 2, dh, t).transpose(0, 4, 1, 2, 3)
    vd = o["vd_f"].reshape(bsz, t, h_diff, 2 * dh)
    ks = o["ksT_f"].reshape(bsz, h_dsa, dh, t).transpose(0, 3, 1, 2)
    vs = o["vsT_f"].reshape(bsz, h_dsa, dh, t).transpose(0, 3, 1, 2)
    ki = o["kiT_f"].transpose(0, 2, 1)
    return kd, vd, ks, vs, ki


def kernel(x_prompt, x_sample, cache_diff_k, cache_diff_v, cache_dsa_k, cache_dsa_v, cache_idx_k, page_table, g_mix, w_in, g_q_diff, g_k_diff, lam_q1, lam_k1, lam_q2, lam_k2, g_subln, g_q_dsa, g_k_dsa, w_o_diff, w_o_dsa, w_out, g_ffn, w_group, w_router, w_gate, w_up, w_down):
    bsz, seq, d_model = x_prompt.shape
    db, dec_seq, _ = x_sample.shape
    depth, n_pool, page, h_diff, _, dh = cache_diff_k.shape
    h_dsa = cache_dsa_k.shape[3]
    n_pages = page_table.shape[1]
    past_len = n_pages * page
    n_groups = w_group.shape[2]
    n_experts = w_router.shape[2]
    w_diff = h_diff * 2 * dh
    w_dsa = h_dsa * dh
    h_idx = (w_in.shape[2] - (3 * w_diff + 3 * w_dsa + dh + 2 * d_model)) // (dh + 1)
    w_idx = h_idx * dh
    assert dh == HEAD_DIM and cache_dsa_k.shape[4] == HEAD_DIM and cache_idx_k.shape[3] == HEAD_DIM
    assert dec_seq == 1 and depth == 1
    assert 3 * w_diff + 3 * w_dsa + w_idx + dh + h_idx + 2 * d_model == w_in.shape[2]
    k_sel_p = min(TOPK_MAX, seq // 4)
    k_sel_s = min(TOPK_MAX, (past_len + dec_seq) // 4)

    l = 0
    lam_init = 0.8 - 0.6 * math.exp(-0.3 * l)
    n_qkv = 3 * w_diff + 3 * w_dsa + w_idx
    w_l = w_in[l]
    rt = jnp.concatenate([w_router[l], w_group[l]], axis=1)
    rt = jnp.pad(rt, ((0, 0), (0, LANES - rt.shape[1])))
    rt_hi = rt.astype(BF16)
    blk = lax.broadcasted_iota(I32, (2 * LANES, 2 * LANES), 0) // HEAD_DIM
    blk_c = lax.broadcasted_iota(I32, (2 * LANES, 2 * LANES), 1) // HEAD_DIM
    prm = dict(
        w_diff=w_diff, w_dsa=w_dsa, w_idx=w_idx, h_idx=h_idx, n_groups=n_groups, n_experts=n_experts,
        g_mix=g_mix[l][None, :],
        w_qkv=w_l[:, :n_qkv].astype(BF16),
        w_small=jnp.pad(w_l[:, n_qkv:n_qkv + dh + h_idx], ((0, 0), (0, LANES - dh - h_idx))).astype(BF16),
        w_gate=w_l[:, n_qkv + dh + h_idx:].astype(BF16),
        g_qd=jnp.tile(g_q_diff[l], w_diff // dh)[None, :], g_kd=jnp.tile(g_k_diff[l], w_diff // dh)[None, :],
        g_qs=jnp.tile(g_q_dsa[l], w_dsa // dh)[None, :], g_ks=jnp.tile(g_k_dsa[l], w_dsa // dh)[None, :],
        bd=(blk == blk_c).astype(BF16),
        w_od=w_o_diff[l].astype(BF16), w_os=w_o_dsa[l].astype(BF16), w_out=w_out[l].astype(BF16),
        g_ffn=g_ffn[l][None, :], w_rt_hi=rt_hi, w_rt_lo=(rt - rt_hi.astype(F32)).astype(BF16))
    lam_vecs = jnp.stack([lam_q1[l], lam_k1[l], lam_q2[l], lam_k2[l]])
    g_sub = g_subln[l][None, :]
    moe_w = (w_gate[l], w_up[l], w_down[l])

    n_p = bsz * seq
    tm = _tile_rows(seq, ROW_TILE)
    kc = _tile_rows(seq, KEY_CHUNK)
    cos_p, sin_p = _rope_tables(jnp.arange(seq, dtype=I32))
    x_p = x_prompt.reshape(n_p, d_model)
    o = _inproj(x_p, cos_p, sin_p, bsz, tm, kc, prm, natural_k=False)
    r3 = lambda a: a.reshape(bsz, seq, a.shape[1])
    od = _diff_attn(r3(o["qd"]), o["kdT_b"], r3(o["vd_b"]), lam_vecs, g_sub, lam_init, tm)
    os_ = _dsa_attn(r3(o["qi"]), o["ki2T_b"], r3(o["wi"]), r3(o["qs"]), o["ksT_b"], r3(o["vs_b"]),
                    k_sel_p, h_idx, _tile_rows(seq, DSA_Q_TILE))
    x1, h2, _, route, counts = _outproj(x_p, od.reshape(n_p, w_diff), os_.reshape(n_p, w_dsa), o["gates"], prm, tm)
    y_p = _routed_moe(x1, h2, route, counts, *moe_w, n_experts, tm, MOE_ROW_TILE)
    outs_p = _key_outputs(o, bsz, seq, h_diff, h_dsa, dh)

    cos_s, sin_s = _rope_tables(jnp.full((db,), past_len, I32))
    x_s = x_sample.reshape(db, d_model)
    o = _inproj(x_s, cos_s, sin_s, 1, db, db, prm, natural_k=True)
    pad = 2 * LANES
    assert pad >= k_sel_s
    scores = _idx_scores(page_table, o["qi"].reshape(db, h_idx, dh), o["wi"][:, dh:dh + h_idx].reshape(db, h_idx, 1),
                         o["ki_b"][:, :dh].reshape(db, 1, dh), cache_idx_k[l].transpose(0, 2, 1), pad)
    bias = _select(scores, k_sel_s, past_len, pad)
    caches = (cache_diff_k[l].transpose(0, 2, 3, 4, 1).reshape(n_pool, w_diff, page),
              cache_diff_v[l].reshape(n_pool, page * h_diff, 2 * dh),
              cache_dsa_k[l].transpose(0, 2, 3, 1).reshape(n_pool, w_dsa, page),
              cache_dsa_v[l].transpose(0, 2, 3, 1).reshape(n_pool, w_dsa, page))
    od, os_ = _decode_attn(page_table, lam_vecs, g_sub,
                           o["qd"].reshape(db, 2 * h_diff, dh), o["kd_b"].reshape(db, 2 * h_diff, dh),
                           o["vd_b"].reshape(db, 1, w_diff),
                           o["qs"].reshape(db, h_dsa, dh), o["ks_b"].reshape(db, h_dsa, dh),
                           o["vs_b"].reshape(db, 1, w_dsa),
                           bias.reshape(db, 1, bias.shape[1]), caches, lam_init)
    x1, h2, comb, _, _ = _outproj(x_s, od, os_, o["gates"], prm, db)
    y_s = _moe(x1, h2, comb, *moe_w, db)
    outs_s = tuple(a.reshape((db, dec_seq) + a.shape[2:]) for a in _key_outputs(o, 1, db, h_diff, h_dsa, dh))

    return ((y_p.reshape(bsz, seq, d_model), y_s.reshape(db, dec_seq, d_model))
            + tuple(a[None] for a in outs_p) + tuple(a[None] for a in outs_s))
```

```python
import functools
import math

import jax
import jax.numpy as jnp
from jax import lax
from jax.experimental import pallas as pl
from jax.experimental.pallas import tpu as pltpu

F32 = jnp.float32
BF16 = jnp.bfloat16
I32 = jnp.int32

EPS = 1e-6
ROPE_THETA = 10000.0
TOPK_MAX = 256
LANES = 128
HEAD_DIM = 64
MASK_NEG = -1e30
INT_MIN = -(2 ** 31)
KEY_NEG_INF = -2139095041
VMEM_LIMIT = 56 * 1024 * 1024
ROW_TILE = 256
KEY_CHUNK = 512
DSA_Q_TILE = 128
MOE_ROW_TILE = 256
LOG2E = 1.4426950408889634


def _params(*sem, **kw):
    return pltpu.CompilerParams(dimension_semantics=sem, vmem_limit_bytes=VMEM_LIMIT, **kw)


def _dot(a, b):
    return jnp.dot(a, b, preferred_element_type=F32)


def _dot_nt(a, b):
    return lax.dot_general(a, b, (((1,), (1,)), ((), ())), preferred_element_type=F32)


def _split_bf16(a):
    hi = a.astype(BF16)
    lo = (a - hi.astype(F32)).astype(BF16)
    return hi, lo


_INPROJ_OUTS = ("qd", "kdT_f", "kdT_b", "vd_f", "vd_b", "qs", "ksT_f", "ksT_b", "vsT_f", "vs_b",
                "qi", "kiT_f", "ki2T_b", "wi", "gates")
_INPROJ_NATURAL = ("kd_b", "ks_b", "ki_b")


def _mm(a, w_ref, wlo_ref, cols=slice(None)):
    if wlo_ref is None:
        return _dot(a.astype(BF16), w_ref[:, cols])
    hi, lo = _split_bf16(a)
    return _dot(hi, w_ref[:, cols]) + (_dot(hi, wlo_ref[:, cols]) + _dot(lo, w_ref[:, cols]))


def _dot3(a, b, nt=False):
    dot = _dot_nt if nt else _dot
    a_hi, a_lo = _split_bf16(a)
    b_hi, b_lo = _split_bf16(b)
    return dot(a_hi, b_hi) + (dot(a_hi, b_lo) + dot(a_lo, b_hi))


def _inproj_kernel(x_ref, cos_ref, sin_ref, gmix_ref, wqkv_ref, wsm_ref, wgate_ref,
                   gqd_ref, gkd_ref, gqs_ref, gks_ref, bd_ref, *refs,
                   names, precise, w_diff, w_dsa, w_idx, h_idx, qk_scale):
    wqkv_lo, wsm_lo, wgate_lo = refs[:3] if precise else (None, None, None)
    o = dict(zip(names, refs[3:] if precise else refs))
    act = F32 if precise else BF16
    x = x_ref[...]
    tm = x.shape[0]
    ms = jnp.mean(x * x, axis=-1, keepdims=True)
    h = (x * lax.rsqrt(ms + EPS)) * gmix_ref[...]
    if not precise:
        h = h.astype(BF16)

    cos = cos_ref[...]
    sin = sin_ref[...]
    lane = lax.broadcasted_iota(I32, (tm, LANES), 1)
    low_half = (lane & (HEAD_DIM // 2)) == 0

    def rope(y):
        partner = jnp.where(low_half, pltpu.roll(y, LANES - HEAD_DIM // 2, 1), pltpu.roll(y, HEAD_DIM // 2, 1))
        return y * cos + partner * sin

    bd = bd_ref[...]

    def head_norm(p, g_ref):
        hi, lo = _split_bf16(p * p)
        width = bd.shape[0]
        parts = []
        for c in range(0, p.shape[1], width):
            parts.append(_dot(hi[:, c:c + width], bd) + _dot(lo[:, c:c + width], bd))
        ss = jnp.concatenate(parts, axis=1) if len(parts) > 1 else parts[0]
        return (p * lax.rsqrt(ss * (1.0 / HEAD_DIM) + EPS)) * g_ref[...]

    def seg(c0, width):
        return _mm(h, wqkv_ref, wqkv_lo, slice(c0, c0 + width))

    def rope_all(y, scale=None):
        outs = []
        for c in range(0, y.shape[1], LANES):
            r = rope(y[:, c:c + LANES])
            outs.append(r if scale is None else r * scale)
        return jnp.concatenate(outs, axis=1) if len(outs) > 1 else outs[0]

    c0 = 0
    o["qd"][...] = rope_all(head_norm(seg(c0, w_diff), gqd_ref), qk_scale * LOG2E).astype(act)
    c0 += w_diff
    kd = rope_all(head_norm(seg(c0, w_diff), gkd_ref))
    kd_t = kd.T
    o["kdT_f"][...] = kd_t
    o["kdT_b"][...] = kd_t.astype(BF16)
    if "kd_b" in o:
        o["kd_b"][...] = kd.astype(act)
    c0 += w_diff
    vd = seg(c0, w_diff)
    o["vd_f"][...] = vd
    o["vd_b"][...] = vd.astype(act)
    c0 += w_diff
    o["qs"][...] = rope_all(head_norm(seg(c0, w_dsa), gqs_ref), qk_scale * LOG2E).astype(act)
    c0 += w_dsa
    ks = rope_all(head_norm(seg(c0, w_dsa), gks_ref))
    ks_t = ks.T
    o["ksT_f"][...] = ks_t
    o["ksT_b"][...] = ks_t.astype(BF16)
    if "ks_b" in o:
        o["ks_b"][...] = ks.astype(act)
    c0 += w_dsa
    vs = seg(c0, w_dsa)
    o["vsT_f"][...] = vs.T
    o["vs_b"][...] = vs.astype(act)
    c0 += w_dsa
    o["qi"][...] = rope_all(seg(c0, w_idx), qk_scale).astype(act)

    sm = _mm(h, wsm_ref, wsm_lo)
    ki = rope(sm)
    ki2 = jnp.where(lane < HEAD_DIM, ki, pltpu.roll(ki, HEAD_DIM, 1))
    ki2_t = ki2.T
    o["kiT_f"][...] = ki2_t[:HEAD_DIM]
    o["ki2T_b"][...] = ki2_t.astype(BF16)
    if "ki_b" in o:
        o["ki_b"][...] = ki2.astype(act)
    o["wi"][...] = sm * (h_idx ** -0.5)

    g = _mm(h, wgate_ref, wgate_lo)
    o["gates"][...] = (1.0 / (1.0 + jnp.exp(-g))).astype(act)


def _inproj(x, cos_t, sin_t, bsz, tm, kc, prm, sample):
    n, d = x.shape
    t = n // bsz
    tiles = t // tm
    per_chunk = kc // tm
    w_diff, w_dsa, w_idx, h_idx = prm["w_diff"], prm["w_dsa"], prm["w_idx"], prm["h_idx"]
    wg = prm["w_gate"].shape[1]
    act = F32 if sample else BF16
    row = lambda i: (i, 0)
    const = lambda i: (0, 0)
    pos = lambda i: (i % tiles, 0)
    full = lambda a: pl.BlockSpec(a.shape, const, pipeline_mode=pl.Buffered(1))
    weights = [prm[k] for k in ("g_mix", "w_qkv", "w_small", "w_gate", "g_qd", "g_kd", "g_qs", "g_ks", "bd")]
    if sample:
        weights += [prm["w_qkv_lo"], prm["w_small_lo"], prm["w_gate_lo"]]
    in_specs = ([pl.BlockSpec((tm, d), row), pl.BlockSpec((tm, LANES), pos), pl.BlockSpec((tm, LANES), pos)]
                + [full(w) for w in weights])

    def nat(w, dt):
        return jax.ShapeDtypeStruct((n, w), dt), pl.BlockSpec((tm, w), row)

    def tr_f(w):
        return (jax.ShapeDtypeStruct((bsz, w, t), F32),
                pl.BlockSpec((None, w, tm), lambda i: (i // tiles, 0, i % tiles)))

    def tr_b(w):
        return (jax.ShapeDtypeStruct((bsz, t // kc, w, kc), BF16),
                pl.BlockSpec((None, None, w, tm),
                             lambda i: (i // tiles, (i % tiles) // per_chunk, 0, (i % tiles) % per_chunk)))

    table = dict(qd=nat(w_diff, act), kdT_f=tr_f(w_diff), kdT_b=tr_b(w_diff), vd_f=nat(w_diff, F32),
                 vd_b=nat(w_diff, act), qs=nat(w_dsa, act), ksT_f=tr_f(w_dsa), ksT_b=tr_b(w_dsa),
                 vsT_f=tr_f(w_dsa), vs_b=nat(w_dsa, act), qi=nat(w_idx, act), kiT_f=tr_f(HEAD_DIM),
                 ki2T_b=tr_b(LANES), wi=nat(LANES, F32), gates=nat(wg, act),
                 kd_b=nat(w_diff, act), ks_b=nat(w_dsa, act), ki_b=nat(LANES, act))
    names = _INPROJ_OUTS + (_INPROJ_NATURAL if sample else ())
    kern = functools.partial(_inproj_kernel, names=names, precise=sample, w_diff=w_diff, w_dsa=w_dsa, w_idx=w_idx,
                             h_idx=h_idx, qk_scale=HEAD_DIM ** -0.5)
    outs = pl.pallas_call(kern, grid=(n // tm,), in_specs=in_specs,
                          out_specs=[table[k][1] for k in names], out_shape=[table[k][0] for k in names],
                          compiler_params=_params("parallel"), name="inproj")(x, cos_t, sin_t, *weights)
    return dict(zip(names, outs))


def _stack_halves(q):
    lane = lax.broadcasted_iota(I32, q.shape, 1)
    zero = jnp.zeros_like(q)
    return jnp.concatenate([jnp.where(lane < HEAD_DIM, q, zero), jnp.where(lane >= HEAD_DIM, q, zero)], axis=0)


def _lane_fold(a):
    out = a[:, :LANES]
    for c in range(LANES, a.shape[1], LANES):
        out = out + a[:, c:c + LANES]
    return out


def _online_update(s_ref, v, m_sc, l_sc, acc_sc, fin=None):
    s = s_ref[...]
    if fin is not None:
        s = fin(s, 0)
    m_prev = m_sc[...]
    m_new = jnp.maximum(m_prev, jnp.max(s, axis=-1, keepdims=True))
    alpha = jnp.exp2(m_prev - m_new)
    p = jnp.exp2(s - jnp.tile(m_new, (1, s.shape[1] // LANES)))
    l_sc[...] = alpha * l_sc[...] + _lane_fold(p)
    acc_sc[...] = alpha * acc_sc[...] + _dot(p.astype(BF16), v)
    m_sc[...] = m_new


def _finish(l_sc, acc_sc):
    return acc_sc[...] * (1.0 / jnp.sum(l_sc[...], axis=-1, keepdims=True))


def _attend_chunks(n, score, value, s_a, s_b, m_sc, l_sc, acc_sc, last=None):
    s_a[...] = score(0)

    def pair(p, carry):
        j = 2 * p
        s_b[...] = score(j + 1)
        _online_update(s_a, value(j), m_sc, l_sc, acc_sc)
        s_a[...] = score(j + 2)
        _online_update(s_b, value(j + 1), m_sc, l_sc, acc_sc)
        return carry

    lax.fori_loop(0, (n - 1) // 2, pair, 0)

    @pl.when(n % 2 == 1)
    def _():
        _online_update(s_a, value(n - 1), m_sc, l_sc, acc_sc, last)

    @pl.when(n % 2 == 0)
    def _():
        s_b[...] = score(n - 1)
        _online_update(s_a, value(n - 2), m_sc, l_sc, acc_sc)
        _online_update(s_b, value(n - 1), m_sc, l_sc, acc_sc, last)


def _lambda(lam_ref, lam_init):
    a = jnp.sum(lam_ref[0:1, :] * lam_ref[1:2, :], axis=-1, keepdims=True)
    b = jnp.sum(lam_ref[2:3, :] * lam_ref[3:4, :], axis=-1, keepdims=True)
    return jnp.exp(a) - jnp.exp(b) + lam_init


def _attn_scratch(rows, kc):
    return [pltpu.VMEM((rows, LANES), F32)] * 3 + [pltpu.VMEM((rows, kc), F32)] * 2


def _diff_attn_kernel(lam_ref, gsub_ref, q_ref, kt_ref, v_ref, o_ref, m_sc, l_sc, acc_sc, s_a, s_b, *,
                      tq, kc, lam_init):
    q0 = pl.program_id(2) * tq
    n_c = q0 // kc + 1
    q2 = _stack_halves(q_ref[...])
    m_sc[...] = jnp.full(m_sc.shape, -jnp.inf, F32)
    l_sc[...] = jnp.zeros(l_sc.shape, F32)
    acc_sc[...] = jnp.zeros(acc_sc.shape, F32)

    def causal(s, row0):
        r = lax.broadcasted_iota(I32, s.shape, 0) + row0
        r = jnp.where(r >= tq, r - tq, r) + q0
        c = lax.broadcasted_iota(I32, s.shape, 1) + (n_c - 1) * kc
        return jnp.where(c <= r, s, -jnp.inf)

    _attend_chunks(n_c, lambda j: _dot(q2, kt_ref[j]),
                   lambda j: v_ref[pl.ds(pl.multiple_of(j * kc, kc), kc), :],
                   s_a, s_b, m_sc, l_sc, acc_sc, last=causal)

    a = _finish(l_sc, acc_sc)
    o = a[:tq] - _lambda(lam_ref, lam_init) * a[tq:]
    ms = jnp.mean(o * o, axis=-1, keepdims=True)
    on = ((o * lax.rsqrt(ms + EPS)) * gsub_ref[...]) * (1.0 - lam_init)
    o_ref[...] = on.astype(BF16)


def _diff_attn(q, kt, v, lam_vecs, g_sub, lam_init, tq):
    b, t, w = q.shape
    nc, kc = kt.shape[1], kt.shape[3]
    assert kc % tq == 0
    grid = (b, w // LANES, t // tq)
    qmap = lambda bi, hi, qi: (bi, qi, hi)
    const = lambda bi, hi, qi: (0, 0)
    kern = functools.partial(_diff_attn_kernel, tq=tq, kc=kc, lam_init=lam_init)
    return pl.pallas_call(
        kern, grid=grid,
        in_specs=[pl.BlockSpec(lam_vecs.shape, const), pl.BlockSpec(g_sub.shape, const),
                  pl.BlockSpec((None, tq, LANES), qmap),
                  pl.BlockSpec((None, nc, LANES, kc), lambda bi, hi, qi: (bi, 0, hi, 0)),
                  pl.BlockSpec((None, t, LANES), lambda bi, hi, qi: (bi, 0, hi))],
        out_specs=pl.BlockSpec((None, tq, LANES), qmap),
        out_shape=jax.ShapeDtypeStruct((b, t, w), BF16),
        scratch_shapes=_attn_scratch(2 * tq, kc),
        compiler_params=_params("parallel", "parallel", "arbitrary"), name="diff_attn")(lam_vecs, g_sub, q, kt, v)


def _key_to_f32(k):
    b = k ^ (lax.shift_right_arithmetic(k, 31) & 0x7FFFFFFF)
    return lax.bitcast_convert_type(b, F32)


def _select_bias(score_sc, bias_sc, n_c, k_sel, row_pos, col0):
    _, r, kc = score_sc.shape

    def count(pred):
        def body(c, a):
            return a + _lane_fold(jnp.where(pred(score_sc[c]), 1.0, 0.0))
        a = lax.fori_loop(0, n_c, body, jnp.zeros((r, LANES), F32))
        return jnp.sum(a, axis=-1, keepdims=True)

    def bit_body(i, t):
        cand = t + lax.shift_left(jnp.int32(1), 31 - i)
        cand_f = _key_to_f32(cand)
        cnt = count(lambda sc: sc >= cand_f)
        return jnp.where(cnt >= k_sel, cand, t)

    t = lax.fori_loop(0, 32, bit_body, jnp.full((r, 1), INT_MIN, I32))
    thr = _key_to_f32(jnp.maximum(t, KEY_NEG_INF))
    need = k_sel - count(lambda sc: sc > thr)

    ri = lax.broadcasted_iota(I32, (kc, kc), 0)
    ci = lax.broadcasted_iota(I32, (kc, kc), 1)
    tri = jnp.where(ri <= ci, 1.0, 0.0).astype(BF16)
    col = lax.broadcasted_iota(I32, (r, kc), 1)

    def bias_body(c, run):
        sc = score_sc[c]
        eq = sc == thr
        pre = _dot(jnp.where(eq, 1.0, 0.0).astype(BF16), tri)
        sel = (sc > thr) | (eq & ((pre + run) <= need))
        causal = (col + (col0 + c * kc)) <= row_pos
        bias_sc[c] = jnp.where(sel & causal, 0.0, MASK_NEG)
        return run + pre[:, kc - 1:kc]

    lax.fori_loop(0, n_c, bias_body, jnp.zeros((r, 1), F32))


def _dsa_attn_kernel(qi_ref, ki2t_ref, wi_ref, qs_ref, kst_ref, vs_ref, o_ref,
                     qstack_sc, score_sc, bias_sc, m_sc, l_sc, acc_sc, s_a, s_b, *, tq, kc, k_sel, h_idx):
    qb = pl.program_id(1)
    q0 = qb * tq
    n_c = q0 // kc + 1
    row_pos = q0 + lax.broadcasted_iota(I32, (tq, 1), 0)

    for hh in range(h_idx):
        pair = qi_ref[:, (hh // 2) * LANES:(hh // 2 + 1) * LANES]
        lane = lax.broadcasted_iota(I32, pair.shape, 1)
        keep = (lane < HEAD_DIM) if hh % 2 == 0 else (lane >= HEAD_DIM)
        qstack_sc[hh * tq:(hh + 1) * tq, :] = jnp.where(keep, pair, jnp.zeros_like(pair))
    wi = wi_ref[...]

    def score_body(c, carry):
        d = _dot(qstack_sc[...], ki2t_ref[c])
        acc = jnp.zeros((tq, kc), F32)
        for hh in range(h_idx):
            acc = acc + jnp.maximum(d[hh * tq:(hh + 1) * tq], 0.0) * wi[:, HEAD_DIM + hh:HEAD_DIM + hh + 1]
        col = c * kc + lax.broadcasted_iota(I32, (tq, kc), 1)
        score_sc[c] = jnp.where(col <= row_pos, acc, -jnp.inf)
        return carry

    lax.fori_loop(0, n_c, score_body, 0)
    _select_bias(score_sc, bias_sc, n_c, k_sel, row_pos, 0)

    for hp in range(qs_ref.shape[1] // LANES):
        q2 = _stack_halves(qs_ref[:, hp * LANES:(hp + 1) * LANES])
        m_sc[...] = jnp.full(m_sc.shape, MASK_NEG, F32)
        l_sc[...] = jnp.zeros(l_sc.shape, F32)
        acc_sc[...] = jnp.zeros(acc_sc.shape, F32)

        def score(c, q2=q2, hp=hp):
            bias = bias_sc[c]
            return _dot(q2, kst_ref[c, hp * LANES:(hp + 1) * LANES, :]) + jnp.concatenate([bias, bias], axis=0)

        def value(c, hp=hp):
            return vs_ref[pl.ds(pl.multiple_of(c * kc, kc), kc), hp * LANES:(hp + 1) * LANES]

        _attend_chunks(n_c, score, value, s_a, s_b, m_sc, l_sc, acc_sc)
        a = _finish(l_sc, acc_sc)
        lane = lax.broadcasted_iota(I32, (tq, LANES), 1)
        o_ref[:, hp * LANES:(hp + 1) * LANES] = jnp.where(lane < HEAD_DIM, a[:tq], a[tq:]).astype(BF16)


def _dsa_attn(qi, ki2t, wi, qs, kst, vs, k_sel, h_idx, tq):
    b, t, w = qs.shape
    nc, kc = kst.shape[1], kst.shape[3]
    assert kc % tq == 0 and kc >= k_sel
    qmap = lambda bi, qb: (bi, qb, 0)
    kern = functools.partial(_dsa_attn_kernel, tq=tq, kc=kc, k_sel=k_sel, h_idx=h_idx)
    return pl.pallas_call(
        kern, grid=(b, t // tq),
        in_specs=[pl.BlockSpec((None, tq, qi.shape[2]), qmap),
                  pl.BlockSpec((None, nc, LANES, kc), lambda bi, qb: (bi, 0, 0, 0)),
                  pl.BlockSpec((None, tq, LANES), qmap), pl.BlockSpec((None, tq, w), qmap),
                  pl.BlockSpec((None, nc, w, kc), lambda bi, qb: (bi, 0, 0, 0)),
                  pl.BlockSpec((None, t, w), lambda bi, qb: (bi, 0, 0))],
        out_specs=pl.BlockSpec((None, tq, w), qmap),
        out_shape=jax.ShapeDtypeStruct((b, t, w), BF16),
        scratch_shapes=[pltpu.VMEM((h_idx * tq, LANES), BF16), pltpu.VMEM((nc, tq, kc), F32),
                        pltpu.VMEM((nc, tq, kc), F32)] + _attn_scratch(2 * tq, kc),
        compiler_params=_params("parallel", "arbitrary"), name="dsa_attn")(qi, ki2t, wi, qs, kst, vs)


def _page_specs(n_pages, block):
    def spec(j):
        return pl.BlockSpec((None,) + block, lambda b, pt, j=j: (pt[b, j],) + (0,) * len(block))
    return [spec(j) for j in range(n_pages)]


def _idx_score_kernel(pt_ref, q_ref, w_ref, knew_ref, *refs, n_pages, page, pad):
    page_refs, o_ref = refs[:n_pages], refs[n_pages]
    q = q_ref[...]
    w = w_ref[...]
    for j in range(n_pages):
        d = _dot3(q, page_refs[j][...])
        o_ref[:, j * page:(j + 1) * page] = jnp.sum(jnp.maximum(d, 0.0) * w, axis=0, keepdims=True)
    d_new = jnp.sum(q * knew_ref[...], axis=-1, keepdims=True)
    s_new = jnp.sum(jnp.maximum(d_new, 0.0) * w, axis=0, keepdims=True)
    lane = lax.broadcasted_iota(I32, (1, pad), 1)
    o_ref[:, n_pages * page:] = jnp.where(lane == 0, s_new, -jnp.inf)


def _idx_scores(page_table, q_i, w_i, k_new, cache_t, pad):
    db, n_pages = page_table.shape
    _, dh, page = cache_t.shape
    h_idx = q_i.shape[1]
    tk = n_pages * page + pad
    seq = lambda b, pt: (b, 0, 0)
    kern = functools.partial(_idx_score_kernel, n_pages=n_pages, page=page, pad=pad)
    grid_spec = pltpu.PrefetchScalarGridSpec(
        num_scalar_prefetch=1, grid=(db,),
        in_specs=[pl.BlockSpec((None, h_idx, dh), seq), pl.BlockSpec((None, h_idx, 1), seq),
                  pl.BlockSpec((None, 1, dh), seq)] + _page_specs(n_pages, (dh, page)),
        out_specs=pl.BlockSpec((None, 1, tk), seq))
    out = pl.pallas_call(kern, grid_spec=grid_spec, out_shape=jax.ShapeDtypeStruct((db, 1, tk), F32),
                         compiler_params=_params("parallel"), name="idx_scores")(
        page_table, q_i, w_i, k_new, *([cache_t] * n_pages))
    return out.reshape(db, tk)


def _select_kernel(s_ref, o_ref, score_sc, bias_sc, *, kc, k_sel, q_pos):
    r = s_ref.shape[0]
    nc = s_ref.shape[1] // kc
    for c in range(nc):
        score_sc[c] = s_ref[:, c * kc:(c + 1) * kc]
    row_pos = jnp.full((r, 1), q_pos, I32)
    _select_bias(score_sc, bias_sc, nc, k_sel, row_pos, 0)
    for c in range(nc):
        o_ref[:, c * kc:(c + 1) * kc] = bias_sc[c]


def _select(scores, k_sel, q_pos, kc):
    r, tk = scores.shape
    nc = tk // kc
    kern = functools.partial(_select_kernel, kc=kc, k_sel=k_sel, q_pos=q_pos)
    return pl.pallas_call(
        kern, grid=(1,), in_specs=[pl.BlockSpec((r, tk), lambda i: (0, 0))],
        out_specs=pl.BlockSpec((r, tk), lambda i: (0, 0)), out_shape=jax.ShapeDtypeStruct((r, tk), F32),
        scratch_shapes=[pltpu.VMEM((nc, r, kc), F32), pltpu.VMEM((nc, r, kc), F32)],
        compiler_params=_params("arbitrary"), name="select")(scores)


def _block_diag_rows(q, width):
    t = jnp.concatenate([q] * (width // HEAD_DIM), axis=1)
    row = lax.broadcasted_iota(I32, t.shape, 0)
    col = lax.broadcasted_iota(I32, t.shape, 1)
    return jnp.where((col >= row * HEAD_DIM) & (col < (row + 1) * HEAD_DIM), t, jnp.zeros_like(t))


def _decode_attn_kernel(pt_ref, lam_ref, gsub_ref, qd_ref, kdn_ref, vdn_ref, qs_ref, ksn_ref, vsn_ref, bias_ref,
                        *refs, n_pages, page, lam_init):
    kd_pages = refs[:n_pages]
    vd_pages = refs[n_pages:2 * n_pages]
    ks_pages = refs[2 * n_pages:3 * n_pages]
    vs_pages = refs[3 * n_pages:4 * n_pages]
    od_ref, os_ref = refs[4 * n_pages], refs[4 * n_pages + 1]
    h_diff = vd_pages[0].shape[0] // page

    def probs(q_ref, knew_ref, k_pages, bias):
        q = q_ref[...]
        qbd = _block_diag_rows(q, k_pages[0].shape[0])
        s = [_dot3(qbd, k_pages[j][...]) for j in range(n_pages)]
        s_new = jnp.sum(q * knew_ref[...], axis=-1, keepdims=True)
        if bias is not None:
            s = [s[j] + bias[:, j * page:(j + 1) * page] for j in range(n_pages)]
            s_new = s_new + bias[:, n_pages * page:n_pages * page + 1]
        m = s_new
        for j in range(n_pages):
            m = jnp.maximum(m, jnp.max(s[j], axis=-1, keepdims=True))
        p_new = jnp.exp2(s_new - m)
        p = [jnp.exp2(s[j] - m) for j in range(n_pages)]
        l = p_new
        for j in range(n_pages):
            l = l + jnp.sum(p[j], axis=-1, keepdims=True)
        return p, p_new, 1.0 / l

    p, p_new, inv_l = probs(qd_ref, kdn_ref, kd_pages, None)
    acc = p_new * vdn_ref[...]
    for j in range(n_pages):
        v = jnp.concatenate([vd_pages[j][pl.ds(hh, page, stride=h_diff), :] for hh in range(h_diff)], axis=1)
        acc = acc + _dot3(p[j], v)
    a = acc * inv_l
    row = lax.broadcasted_iota(I32, a.shape, 0)
    col = lax.broadcasted_iota(I32, a.shape, 1)
    own = (col // LANES) == (row // 2)
    coef = jnp.where(row % 2 == 0, 1.0, -_lambda(lam_ref, lam_init))
    o = jnp.sum(jnp.where(own, a * coef, 0.0), axis=0, keepdims=True)
    parts = []
    for c in range(0, o.shape[1], LANES):
        oc = o[:, c:c + LANES]
        ms = jnp.mean(oc * oc, axis=-1, keepdims=True)
        parts.append(((oc * lax.rsqrt(ms + EPS)) * gsub_ref[...]) * (1.0 - lam_init))
    od_ref[...] = jnp.concatenate(parts, axis=1)

    p, p_new, inv_l = probs(qs_ref, ksn_ref, ks_pages, bias_ref[...])
    acc = p_new * vsn_ref[...]
    for j in range(n_pages):
        acc = acc + _dot3(p[j], vs_pages[j][...], nt=True)
    a = acc * inv_l
    row = lax.broadcasted_iota(I32, a.shape, 0)
    col = lax.broadcasted_iota(I32, a.shape, 1)
    os_ref[...] = jnp.sum(jnp.where((col // HEAD_DIM) == row, a, 0.0), axis=0, keepdims=True)


def _decode_attn(page_table, lam_vecs, g_sub, qd, kdn, vdn, qs, ksn, vsn, bias, caches, lam_init):
    db, n_pages = page_table.shape
    c_kd, c_vd, c_ks, c_vs = caches
    w_diff, page = c_kd.shape[1:]
    w_dsa = c_ks.shape[1]
    seq = lambda b, pt: (b, 0, 0)
    const = lambda b, pt: (0, 0)
    in_specs = [pl.BlockSpec(lam_vecs.shape, const), pl.BlockSpec(g_sub.shape, const),
                pl.BlockSpec((None,) + qd.shape[1:], seq), pl.BlockSpec((None,) + kdn.shape[1:], seq),
                pl.BlockSpec((None,) + vdn.shape[1:], seq),
                pl.BlockSpec((None,) + qs.shape[1:], seq), pl.BlockSpec((None,) + ksn.shape[1:], seq),
                pl.BlockSpec((None,) + vsn.shape[1:], seq), pl.BlockSpec((None,) + bias.shape[1:], seq)]
    for cache in caches:
        in_specs += _page_specs(n_pages, tuple(cache.shape[1:]))
    kern = functools.partial(_decode_attn_kernel, n_pages=n_pages, page=page, lam_init=lam_init)
    grid_spec = pltpu.PrefetchScalarGridSpec(
        num_scalar_prefetch=1, grid=(db,), in_specs=in_specs,
        out_specs=[pl.BlockSpec((None, 1, w_diff), seq), pl.BlockSpec((None, 1, w_dsa), seq)])
    pages = []
    for cache in caches:
        pages += [cache] * n_pages
    od, os_ = pl.pallas_call(
        kern, grid_spec=grid_spec,
        out_shape=[jax.ShapeDtypeStruct((db, 1, w_diff), F32), jax.ShapeDtypeStruct((db, 1, w_dsa), F32)],
        compiler_params=_params("parallel"), name="decode_attn")(
        page_table, lam_vecs, g_sub, qd, kdn, vdn, qs, ksn, vsn, bias, *pages)
    return od.reshape(db, w_diff), os_.reshape(db, w_dsa)


ROUTE_E1, ROUTE_E2, ROUTE_R1, ROUTE_R2, ROUTE_W1, ROUTE_W2 = range(6)


def _outproj_kernel(x_ref, od_ref, os_ref, gate_ref, wod_ref, wos_ref, wout_ref, gffn_ref, wrh_ref, wrl_ref, *refs,
                    precise, n_groups, n_experts):
    wod_lo, wos_lo, wout_lo = refs[:3] if precise else (None, None, None)
    x1_ref, h2_ref, comb_ref, route_ref, count_ref = refs[3:] if precise else refs
    d = x_ref.shape[1]

    @pl.when(pl.program_id(0) == 0)
    def _():
        count_ref[...] = jnp.zeros(count_ref.shape, F32)

    yd = _mm(od_ref[...], wod_ref, wod_lo)
    ys = _mm(os_ref[...], wos_ref, wos_lo)
    merged = gate_ref[:, :d].astype(F32) * yd + gate_ref[:, d:].astype(F32) * ys
    x1 = x_ref[...] + _mm(merged, wout_ref, wout_lo)
    x1_ref[...] = x1

    ms = jnp.mean(x1 * x1, axis=-1, keepdims=True)
    h2 = (x1 * lax.rsqrt(ms + EPS)) * gffn_ref[...]
    h2_ref[...] = h2

    hi, lo = _split_bf16(h2)
    lg = _dot(hi, wrh_ref[...]) + (_dot(hi, wrl_ref[...]) + _dot(lo, wrh_ref[...]))
    lane = lax.broadcasted_iota(I32, lg.shape, 1)
    lanef = lane.astype(F32)
    big = float(LANES)
    epg = n_experts // n_groups

    gmask = (lane >= n_experts) & (lane < n_experts + n_groups)
    gl = jnp.where(gmask, lg, -jnp.inf)
    gmax = jnp.max(gl, axis=-1, keepdims=True)
    gsel = jnp.min(jnp.where(gl == gmax, lanef, big), axis=-1, keepdims=True) - n_experts
    g_w = 1.0 / jnp.sum(jnp.exp(gl - gmax), axis=-1, keepdims=True)

    lo_lane = gsel * epg
    emask = (lanef >= lo_lane) & (lanef < lo_lane + epg)
    el = jnp.where(emask, lg, -jnp.inf)
    v1 = jnp.max(el, axis=-1, keepdims=True)
    i1 = jnp.min(jnp.where(el == v1, lanef, big), axis=-1, keepdims=True)
    el2 = jnp.where(lanef == i1, -jnp.inf, el)
    v2 = jnp.max(el2, axis=-1, keepdims=True)
    i2 = jnp.min(jnp.where(el2 == v2, lanef, big), axis=-1, keepdims=True)
    e21 = jnp.exp(v2 - v1)
    den = 1.0 / (1.0 + e21)
    w1 = den * g_w
    w2 = (e21 * den) * g_w
    comb_ref[...] = jnp.where(lanef == i1, w1, 0.0) + jnp.where(lanef == i2, w2, 0.0)

    tm = lg.shape[0]
    onehot = jnp.where((lanef == i1) | (lanef == i2), 1.0, 0.0)
    ri = lax.broadcasted_iota(I32, (tm, tm), 0)
    ci = lax.broadcasted_iota(I32, (tm, tm), 1)
    before = jnp.where(ci < ri, 1.0, 0.0).astype(BF16)
    rank_e = count_ref[...] + _dot(before, onehot.astype(BF16))
    r1 = jnp.sum(jnp.where(lanef == i1, rank_e, 0.0), axis=-1, keepdims=True)
    r2 = jnp.sum(jnp.where(lanef == i2, rank_e, 0.0), axis=-1, keepdims=True)
    count_ref[...] += jnp.sum(onehot, axis=0, keepdims=True)
    rec = jnp.zeros(lg.shape, F32)
    for slot, val in ((ROUTE_E1, i1), (ROUTE_E2, i2), (ROUTE_W1, w1), (ROUTE_W2, w2), (ROUTE_R1, r1), (ROUTE_R2, r2)):
        rec = jnp.where(lane == slot, val, rec)
    route_ref[...] = rec


def _outproj(x, od, os_, gates, prm, tm, sample):
    n, d = x.shape
    row = lambda i: (i, 0)
    const = lambda i: (0, 0)
    full = lambda a: pl.BlockSpec(a.shape, const)
    weights = [prm[k] for k in ("w_od", "w_os", "w_out", "g_ffn", "w_rt_hi", "w_rt_lo")]
    if sample:
        weights += [prm["w_od_lo"], prm["w_os_lo"], prm["w_out_lo"]]
    kern = functools.partial(_outproj_kernel, precise=sample, n_groups=prm["n_groups"], n_experts=prm["n_experts"])
    return pl.pallas_call(
        kern, grid=(n // tm,),
        in_specs=[pl.BlockSpec((tm, d), row), pl.BlockSpec((tm, od.shape[1]), row), pl.BlockSpec((tm, os_.shape[1]), row),
                  pl.BlockSpec((tm, gates.shape[1]), row)] + [full(w) for w in weights],
        out_specs=[pl.BlockSpec((tm, d), row), pl.BlockSpec((tm, d), row), pl.BlockSpec((tm, LANES), row),
                   pl.BlockSpec((tm, LANES), row), pl.BlockSpec((1, LANES), const)],
        out_shape=[jax.ShapeDtypeStruct((n, d), F32), jax.ShapeDtypeStruct((n, d), F32),
                   jax.ShapeDtypeStruct((n, LANES), F32), jax.ShapeDtypeStruct((n, LANES), F32),
                   jax.ShapeDtypeStruct((1, LANES), F32)],
        compiler_params=_params("arbitrary"), name="outproj")(x, od, os_, gates, *weights)


def _expert_ffn(h, wg_ref, wu_ref, wd_ref):
    h = h.astype(BF16)
    a = _dot(h, wg_ref[...].astype(BF16))
    u = _dot(h, wu_ref[...].astype(BF16))
    act = (a * (1.0 / (1.0 + jnp.exp(-a)))) * u
    return _dot(act.astype(BF16), wd_ref[...].astype(BF16))


def _moe_kernel(x1_ref, h2_ref, comb_ref, wg_ref, wu_ref, wd_ref, y_ref, acc_sc):
    e = pl.program_id(1)

    @pl.when(e == 0)
    def _():
        acc_sc[...] = jnp.zeros(acc_sc.shape, F32)

    o = _expert_ffn(h2_ref[...], wg_ref, wu_ref, wd_ref)
    comb = comb_ref[...]
    lane = lax.broadcasted_iota(I32, comb.shape, 1)
    c = jnp.sum(jnp.where(lane == e, comb, 0.0), axis=-1, keepdims=True)
    acc_sc[...] += o * c

    @pl.when(e == pl.num_programs(1) - 1)
    def _():
        y_ref[...] = x1_ref[...] + acc_sc[...]


def _moe(x1, h2, comb, w_gate, w_up, w_down, tm):
    n, d = x1.shape
    ne, _, de = w_gate.shape
    row = lambda i, e: (i, 0)
    return pl.pallas_call(
        _moe_kernel, grid=(n // tm, ne),
        in_specs=[pl.BlockSpec((tm, d), row), pl.BlockSpec((tm, d), row), pl.BlockSpec((tm, LANES), row),
                  pl.BlockSpec((None, d, de), lambda i, e: (e, 0, 0)), pl.BlockSpec((None, d, de), lambda i, e: (e, 0, 0)),
                  pl.BlockSpec((None, de, d), lambda i, e: (e, 0, 0))],
        out_specs=pl.BlockSpec((tm, d), row), out_shape=jax.ShapeDtypeStruct((n, d), F32),
        scratch_shapes=[pltpu.VMEM((tm, d), F32)],
        compiler_params=_params("parallel", "arbitrary"), name="moe")(x1, h2, comb, w_gate, w_up, w_down)


def _row_copy(src, src_row, dst, dst_row, sem):
    return pltpu.make_async_copy(src.at[pl.ds(src_row, 1), :], dst.at[pl.ds(dst_row, 1), :], sem)


def _slot_rows(assign_ref, offs_ref, n, tok):
    return (offs_ref[assign_ref[tok]] + assign_ref[2 * n + tok],
            offs_ref[assign_ref[n + tok]] + assign_ref[3 * n + tok])


def _scatter_kernel(assign_ref, offs_ref, group_ref, n_tiles_ref, h2_ref, xs_hbm, zero_sc, sem, zero_sem, *,
                    n, tm, tr):
    base = pl.program_id(0) * tm
    n_experts = offs_ref.shape[0]
    total_tiles = xs_hbm.shape[0] // tr

    def zero_tile(row0):
        return pltpu.make_async_copy(zero_sc, xs_hbm.at[pl.ds(pl.multiple_of(row0, tr), tr), :], zero_sem)

    @pl.when(pl.program_id(0) == 0)
    def _():
        zero_sc[...] = jnp.zeros(zero_sc.shape, F32)
        for wait in (False, True):
            def group_tail(e, carry, wait=wait):
                @pl.when(group_ref[e] > 0)
                def _():
                    cp = zero_tile(offs_ref[e] + group_ref[e] - tr)
                    cp.wait() if wait else cp.start()
                return carry

            def unused(j, carry, wait=wait):
                cp = zero_tile(j * tr)
                cp.wait() if wait else cp.start()
                return carry

            lax.fori_loop(0, n_experts, group_tail, 0)
            lax.fori_loop(n_tiles_ref[0], total_tiles, unused, 0)

    def start(t, carry):
        p1, p2 = _slot_rows(assign_ref, offs_ref, n, base + t)
        _row_copy(h2_ref, t, xs_hbm, p1, sem).start()
        _row_copy(h2_ref, t, xs_hbm, p2, sem).start()
        return carry

    def wait(t, carry):
        p1, p2 = _slot_rows(assign_ref, offs_ref, n, base + t)
        _row_copy(h2_ref, t, xs_hbm, p1, sem).wait()
        _row_copy(h2_ref, t, xs_hbm, p2, sem).wait()
        return carry

    lax.fori_loop(0, tm, start, 0, unroll=8)
    lax.fori_loop(0, tm, wait, 0, unroll=8)


def _scatter_rows(assign, offs, group, n_tiles, h2, n_rows, tm, tr):
    n, d = h2.shape
    grid_spec = pltpu.PrefetchScalarGridSpec(
        num_scalar_prefetch=4, grid=(n // tm,),
        in_specs=[pl.BlockSpec((tm, d), lambda i, *_: (i, 0))], out_specs=pl.BlockSpec(memory_space=pl.ANY),
        scratch_shapes=[pltpu.VMEM((tr, d), F32), pltpu.SemaphoreType.DMA(()), pltpu.SemaphoreType.DMA(())])
    return pl.pallas_call(
        functools.partial(_scatter_kernel, n=n, tm=tm, tr=tr), grid_spec=grid_spec,
        out_shape=jax.ShapeDtypeStruct((n_rows, d), F32),
        compiler_params=_params("arbitrary", disable_bounds_checks=True), name="moe_scatter")(
        assign, offs, group, n_tiles, h2)


def _grouped_ffn_kernel(tile_expert_ref, n_tiles_ref, xs_ref, wg_ref, wu_ref, wd_ref, ys_ref):
    used = pl.program_id(0) < n_tiles_ref[0]

    @pl.when(used)
    def _():
        ys_ref[...] = _expert_ffn(xs_ref[...], wg_ref, wu_ref, wd_ref)

    @pl.when(jnp.logical_not(used))
    def _():
        ys_ref[...] = jnp.zeros(ys_ref.shape, F32)


def _grouped_ffn(tile_expert, n_tiles, xs, w_gate, w_up, w_down, tr):
    rows, d = xs.shape
    ne, _, de = w_gate.shape
    last = lambda i, te, nt: jnp.minimum(i, nt[0] - 1)
    xmap = lambda i, te, nt: (last(i, te, nt), 0)
    wmap = lambda i, te, nt: (te[last(i, te, nt)], 0, 0)
    grid_spec = pltpu.PrefetchScalarGridSpec(
        num_scalar_prefetch=2, grid=(rows // tr,),
        in_specs=[pl.BlockSpec((tr, d), xmap), pl.BlockSpec((None, d, de), wmap), pl.BlockSpec((None, d, de), wmap),
                  pl.BlockSpec((None, de, d), wmap)],
        out_specs=pl.BlockSpec((tr, d), lambda i, te, nt: (i, 0)))
    return pl.pallas_call(
        _grouped_ffn_kernel, grid_spec=grid_spec, out_shape=jax.ShapeDtypeStruct((rows, d), F32),
        compiler_params=_params("arbitrary"), name="moe_ffn")(tile_expert, n_tiles, xs, w_gate, w_up, w_down)


def _combine_kernel(assign_ref, offs_ref, x1_ref, route_ref, ys_hbm, y_ref, buf1, buf2, sem, *, n, tm):
    base = pl.program_id(0) * tm

    def start(t, carry):
        p1, p2 = _slot_rows(assign_ref, offs_ref, n, base + t)
        _row_copy(ys_hbm, p1, buf1, t, sem).start()
        _row_copy(ys_hbm, p2, buf2, t, sem).start()
        return carry

    def wait(t, carry):
        p1, p2 = _slot_rows(assign_ref, offs_ref, n, base + t)
        _row_copy(ys_hbm, p1, buf1, t, sem).wait()
        _row_copy(ys_hbm, p2, buf2, t, sem).wait()
        return carry

    lax.fori_loop(0, tm, start, 0, unroll=8)
    lax.fori_loop(0, tm, wait, 0, unroll=8)
    route = route_ref[...]
    w1 = route[:, ROUTE_W1:ROUTE_W1 + 1]
    w2 = route[:, ROUTE_W2:ROUTE_W2 + 1]
    y_ref[...] = (x1_ref[...] + buf1[...] * w1) + buf2[...] * w2


def _combine_rows(assign, offs, x1, route, ys, tm):
    n, d = x1.shape
    row = lambda i, *_: (i, 0)
    grid_spec = pltpu.PrefetchScalarGridSpec(
        num_scalar_prefetch=2, grid=(n // tm,),
        in_specs=[pl.BlockSpec((tm, d), row), pl.BlockSpec((tm, LANES), row), pl.BlockSpec(memory_space=pl.ANY)],
        out_specs=pl.BlockSpec((tm, d), row),
        scratch_shapes=[pltpu.VMEM((tm, d), F32), pltpu.VMEM((tm, d), F32), pltpu.SemaphoreType.DMA(())])
    return pl.pallas_call(
        functools.partial(_combine_kernel, n=n, tm=tm), grid_spec=grid_spec,
        out_shape=jax.ShapeDtypeStruct((n, d), F32),
        compiler_params=_params("arbitrary", disable_bounds_checks=True), name="moe_combine")(
        assign, offs, x1, route, ys)


def _routed_moe(x1, h2, route, counts, w_gate, w_up, w_down, n_experts, tm, tr):
    n, d = x1.shape
    cnt = counts[0, :n_experts].astype(I32)
    group = ((cnt + tr - 1) // tr) * tr
    ends = jnp.cumsum(group)
    offs = ends - group
    assign = route[:, ROUTE_E1:ROUTE_R2 + 1].astype(I32).T.reshape(-1)
    n_rows = 2 * n + n_experts * tr
    tile_start = jnp.arange(n_rows // tr, dtype=I32) * tr
    tile_expert = jnp.minimum(jnp.sum(tile_start[:, None] >= ends[None, :], axis=1), n_experts - 1).astype(I32)
    n_tiles = (ends[-1:] // tr).astype(I32)
    xs = _scatter_rows(assign, offs, group, n_tiles, h2, n_rows, tm, tr)
    ys = _grouped_ffn(tile_expert, n_tiles, xs, w_gate, w_up, w_down, tr)
    return _combine_rows(assign, offs, x1, route, ys, tm)


def _rope_tables(pos):
    half = HEAD_DIM // 2
    inv_freq = jnp.power(ROPE_THETA, -jnp.arange(half, dtype=F32) / half)
    ang = pos.astype(F32)[:, None] * inv_freq[None, :]
    cos, sin = jnp.cos(ang), jnp.sin(ang)
    reps = LANES // HEAD_DIM
    return jnp.tile(cos, (1, 2 * reps)), jnp.tile(jnp.concatenate([-sin, sin], axis=1), (1, reps))


def _tile_rows(n, target):
    t = min(n, target)
    while n % t:
        t //= 2
    return t


def _key_outputs(o, bsz, t, h_diff, h_dsa, dh):
    kd = o["kdT_f"].reshape(bsz, h_diff, 2, dh, t).transpose(0, 4, 1, 2, 3)
    vd = o["vd_f"].reshape(bsz, t, h_diff, 2 * dh)
    ks = o["ksT_f"].reshape(bsz, h_dsa, dh, t).transpose(0, 3, 1, 2)
    vs = o["vsT_f"].reshape(bsz, h_dsa, dh, t).transpose(0, 3, 1, 2)
    ki = o["kiT_f"].transpose(0, 2, 1)
    return kd, vd, ks, vs, ki


def kernel(x_prompt, x_sample, cache_diff_k, cache_diff_v, cache_dsa_k, cache_dsa_v, cache_idx_k, page_table, g_mix, w_in, g_q_diff, g_k_diff, lam_q1, lam_k1, lam_q2, lam_k2, g_subln, g_q_dsa, g_k_dsa, w_o_diff, w_o_dsa, w_out, g_ffn, w_group, w_router, w_gate, w_up, w_down):
    bsz, seq, d_model = x_prompt.shape
    db, dec_seq, _ = x_sample.shape
    depth, n_pool, page, h_diff, _, dh = cache_diff_k.shape
    h_dsa = cache_dsa_k.shape[3]
    n_pages = page_table.shape[1]
    past_len = n_pages * page
    n_groups = w_group.shape[2]
    n_experts = w_router.shape[2]
    w_diff = h_diff * 2 * dh
    w_dsa = h_dsa * dh
    h_idx = (w_in.shape[2] - (3 * w_diff + 3 * w_dsa + dh + 2 * d_model)) // (dh + 1)
    w_idx = h_idx * dh
    assert dh == HEAD_DIM and cache_dsa_k.shape[4] == HEAD_DIM and cache_idx_k.shape[3] == HEAD_DIM
    assert dec_seq == 1 and depth == 1
    assert 3 * w_diff + 3 * w_dsa + w_idx + dh + h_idx + 2 * d_model == w_in.shape[2]
    k_sel_p = min(TOPK_MAX, seq // 4)
    k_sel_s = min(TOPK_MAX, (past_len + dec_seq) // 4)

    l = 0
    lam_init = 0.8 - 0.6 * math.exp(-0.3 * l)
    n_qkv = 3 * w_diff + 3 * w_dsa + w_idx
    w_l = w_in[l]
    rt = jnp.concatenate([w_router[l], w_group[l]], axis=1)
    rt = jnp.pad(rt, ((0, 0), (0, LANES - rt.shape[1])))
    blk = lax.broadcasted_iota(I32, (2 * LANES, 2 * LANES), 0) // HEAD_DIM
    blk_c = lax.broadcasted_iota(I32, (2 * LANES, 2 * LANES), 1) // HEAD_DIM
    prm = dict(
        w_diff=w_diff, w_dsa=w_dsa, w_idx=w_idx, h_idx=h_idx, n_groups=n_groups, n_experts=n_experts,
        g_mix=g_mix[l][None, :],
        g_qd=jnp.tile(g_q_diff[l], w_diff // dh)[None, :], g_kd=jnp.tile(g_k_diff[l], w_diff // dh)[None, :],
        g_qs=jnp.tile(g_q_dsa[l], w_dsa // dh)[None, :], g_ks=jnp.tile(g_k_dsa[l], w_dsa // dh)[None, :],
        bd=(blk == blk_c).astype(BF16), g_ffn=g_ffn[l][None, :])
    for name, w in (("w_qkv", w_l[:, :n_qkv]),
                    ("w_small", jnp.pad(w_l[:, n_qkv:n_qkv + dh + h_idx], ((0, 0), (0, LANES - dh - h_idx)))),
                    ("w_gate", w_l[:, n_qkv + dh + h_idx:]),
                    ("w_od", w_o_diff[l]), ("w_os", w_o_dsa[l]), ("w_out", w_out[l]), ("w_rt", rt)):
        hi = w.astype(BF16)
        prm[name] = hi
        prm[name + "_lo"] = (w - hi.astype(F32)).astype(BF16)
    prm["w_rt_hi"] = prm["w_rt"]
    lam_vecs = jnp.stack([lam_q1[l], lam_k1[l], lam_q2[l], lam_k2[l]])
    g_sub = g_subln[l][None, :]
    moe_w = (w_gate[l], w_up[l], w_down[l])

    n_p = bsz * seq
    tm = _tile_rows(seq, ROW_TILE)
    kc = _tile_rows(seq, KEY_CHUNK)
    cos_p, sin_p = _rope_tables(jnp.arange(seq, dtype=I32))
    x_p = x_prompt.reshape(n_p, d_model)
    o = _inproj(x_p, cos_p, sin_p, bsz, tm, kc, prm, sample=False)
    r3 = lambda a: a.reshape(bsz, seq, a.shape[1])
    od = _diff_attn(r3(o["qd"]), o["kdT_b"], r3(o["vd_b"]), lam_vecs, g_sub, lam_init, tm)
    os_ = _dsa_attn(r3(o["qi"]), o["ki2T_b"], r3(o["wi"]), r3(o["qs"]), o["ksT_b"], r3(o["vs_b"]),
                    k_sel_p, h_idx, _tile_rows(seq, DSA_Q_TILE))
    x1, h2, _, route, counts = _outproj(x_p, od.reshape(n_p, w_diff), os_.reshape(n_p, w_dsa), o["gates"], prm, tm,
                                        sample=False)
    y_p = _routed_moe(x1, h2, route, counts, *moe_w, n_experts, tm, MOE_ROW_TILE)
    outs_p = _key_outputs(o, bsz, seq, h_diff, h_dsa, dh)

    cos_s, sin_s = _rope_tables(jnp.full((db,), past_len, I32))
    x_s = x_sample.reshape(db, d_model)
    o = _inproj(x_s, cos_s, sin_s, 1, db, db, prm, sample=True)
    pad = 2 * LANES
    assert pad >= k_sel_s
    scores = _idx_scores(page_table, o["qi"].reshape(db, h_idx, dh), o["wi"][:, dh:dh + h_idx].reshape(db, h_idx, 1),
                         o["ki_b"][:, :dh].reshape(db, 1, dh), cache_idx_k[l].transpose(0, 2, 1), pad)
    bias = _select(scores, k_sel_s, past_len, pad)
    caches = (cache_diff_k[l].transpose(0, 2, 3, 4, 1).reshape(n_pool, w_diff, page),
              cache_diff_v[l].reshape(n_pool, page * h_diff, 2 * dh),
              cache_dsa_k[l].transpose(0, 2, 3, 1).reshape(n_pool, w_dsa, page),
              cache_dsa_v[l].transpose(0, 2, 3, 1).reshape(n_pool, w_dsa, page))
    od, os_ = _decode_attn(page_table, lam_vecs, g_sub,
                           o["qd"].reshape(db, 2 * h_diff, dh), o["kd_b"].reshape(db, 2 * h_diff, dh),
                           o["vd_b"].reshape(db, 1, w_diff),
                           o["qs"].reshape(db, h_dsa, dh), o["ks_b"].reshape(db, h_dsa, dh),
                           o["vs_b"].reshape(db, 1, w_dsa),
                           bias.reshape(db, 1, bias.shape[1]), caches, lam_init)
    x1, h2, comb, _, _ = _outproj(x_s, od, os_, o["gates"], prm, db, sample=True)
    y_s = _moe(x1, h2, comb, *moe_w, db)
    outs_s = tuple(a.reshape((db, dec_seq) + a.shape[2:]) for a in _key_outputs(o, 1, db, h_diff, h_dsa, dh))

    return ((y_p.reshape(bsz, seq, d_model), y_s.reshape(db, dec_seq, d_model))
            + tuple(a[None] for a in outs_p) + tuple(a[None] for a in outs_s))
```

```python
import functools
import math

import jax
import jax.numpy as jnp
from jax import lax
from jax.experimental import pallas as pl
from jax.experimental.pallas import tpu as pltpu

F32 = jnp.float32
BF16 = jnp.bfloat16
I32 = jnp.int32

EPS = 1e-6
ROPE_THETA = 10000.0
TOPK_MAX = 256
LANES = 128
HEAD_DIM = 64
MASK_NEG = -1e30
INT_MIN = -(2 ** 31)
KEY_NEG_INF = -2139095041
VMEM_LIMIT = 56 * 1024 * 1024
ROW_TILE = 256
KEY_CHUNK = 512
DSA_Q_TILE = 128
MOE_ROW_TILE = 256
LOG2E = 1.4426950408889634


def _params(*sem, **kw):
    return pltpu.CompilerParams(dimension_semantics=sem, vmem_limit_bytes=VMEM_LIMIT, **kw)


def _dot(a, b):
    return jnp.dot(a, b, preferred_element_type=F32)


def _dot_nt(a, b):
    return lax.dot_general(a, b, (((1,), (1,)), ((), ())), preferred_element_type=F32)


def _split_bf16(a):
    hi = a.astype(BF16)
    lo = (a - hi.astype(F32)).astype(BF16)
    return hi, lo


_INPROJ_OUTS = ("qd", "kdT_f", "kdT_b", "vd_f", "vd_b", "qs", "ksT_f", "ksT_b", "vsT_f", "vs_b",
                "qi", "kiT_f", "ki2T_b", "wi", "gates")
_INPROJ_NATURAL = ("kd_b", "ks_b", "ki_b")


def _mm(a, w_ref, wlo_ref, cols=slice(None)):
    if wlo_ref is None:
        return _dot(a.astype(BF16), w_ref[:, cols])
    hi, lo = _split_bf16(a)
    return _dot(hi, w_ref[:, cols]) + (_dot(hi, wlo_ref[:, cols]) + _dot(lo, w_ref[:, cols]))


def _dot3(a, b, nt=False):
    dot = _dot_nt if nt else _dot
    a_hi, a_lo = _split_bf16(a)
    b_hi, b_lo = _split_bf16(b)
    return dot(a_hi, b_hi) + (dot(a_hi, b_lo) + dot(a_lo, b_hi))


def _inproj_kernel(x_ref, cos_ref, sin_ref, gmix_ref, wqkv_ref, wsm_ref, wgate_ref,
                   gqd_ref, gkd_ref, gqs_ref, gks_ref, bd_ref, *refs,
                   names, precise, w_diff, w_dsa, w_idx, h_idx, qk_scale):
    wqkv_lo, wsm_lo, wgate_lo = refs[:3] if precise else (None, None, None)
    o = dict(zip(names, refs[3:] if precise else refs))
    act = F32 if precise else BF16
    x = x_ref[...]
    tm = x.shape[0]
    ms = jnp.mean(x * x, axis=-1, keepdims=True)
    h = (x * lax.rsqrt(ms + EPS)) * gmix_ref[...]
    if not precise:
        h = h.astype(BF16)

    cos = cos_ref[...]
    sin = sin_ref[...]
    lane = lax.broadcasted_iota(I32, (tm, LANES), 1)
    low_half = (lane & (HEAD_DIM // 2)) == 0

    def rope(y):
        partner = jnp.where(low_half, pltpu.roll(y, LANES - HEAD_DIM // 2, 1), pltpu.roll(y, HEAD_DIM // 2, 1))
        return y * cos + partner * sin

    bd = bd_ref[...]

    def head_norm(p, g_ref):
        hi, lo = _split_bf16(p * p)
        width = bd.shape[0]
        parts = []
        for c in range(0, p.shape[1], width):
            parts.append(_dot(hi[:, c:c + width], bd) + _dot(lo[:, c:c + width], bd))
        ss = jnp.concatenate(parts, axis=1) if len(parts) > 1 else parts[0]
        return (p * lax.rsqrt(ss * (1.0 / HEAD_DIM) + EPS)) * g_ref[...]

    def seg(c0, width):
        return _mm(h, wqkv_ref, wqkv_lo, slice(c0, c0 + width))

    def rope_all(y, scale=None):
        outs = []
        for c in range(0, y.shape[1], LANES):
            r = rope(y[:, c:c + LANES])
            outs.append(r if scale is None else r * scale)
        return jnp.concatenate(outs, axis=1) if len(outs) > 1 else outs[0]

    c0 = 0
    o["qd"][...] = rope_all(head_norm(seg(c0, w_diff), gqd_ref), qk_scale * LOG2E).astype(act)
    c0 += w_diff
    kd = rope_all(head_norm(seg(c0, w_diff), gkd_ref))
    kd_t = kd.T
    o["kdT_f"][...] = kd_t
    o["kdT_b"][...] = kd_t.astype(BF16)
    if "kd_b" in o:
        o["kd_b"][...] = kd.astype(act)
    c0 += w_diff
    vd = seg(c0, w_diff)
    o["vd_f"][...] = vd
    o["vd_b"][...] = vd.astype(act)
    c0 += w_diff
    o["qs"][...] = rope_all(head_norm(seg(c0, w_dsa), gqs_ref), qk_scale * LOG2E).astype(act)
    c0 += w_dsa
    ks = rope_all(head_norm(seg(c0, w_dsa), gks_ref))
    ks_t = ks.T
    o["ksT_f"][...] = ks_t
    o["ksT_b"][...] = ks_t.astype(BF16)
    if "ks_b" in o:
        o["ks_b"][...] = ks.astype(act)
    c0 += w_dsa
    vs = seg(c0, w_dsa)
    o["vsT_f"][...] = vs.T
    o["vs_b"][...] = vs.astype(act)
    c0 += w_dsa
    o["qi"][...] = rope_all(seg(c0, w_idx), qk_scale).astype(act)

    sm = _mm(h, wsm_ref, wsm_lo)
    ki = rope(sm)
    ki2 = jnp.where(lane < HEAD_DIM, ki, pltpu.roll(ki, HEAD_DIM, 1))
    ki2_t = ki2.T
    o["kiT_f"][...] = ki2_t[:HEAD_DIM]
    o["ki2T_b"][...] = ki2_t.astype(BF16)
    if "ki_b" in o:
        o["ki_b"][...] = ki2.astype(act)
    o["wi"][...] = sm * (h_idx ** -0.5)

    g = _mm(h, wgate_ref, wgate_lo)
    o["gates"][...] = (1.0 / (1.0 + jnp.exp(-g))).astype(act)


def _inproj(x, cos_t, sin_t, bsz, tm, kc, prm, sample):
    n, d = x.shape
    t = n // bsz
    tiles = t // tm
    per_chunk = kc // tm
    w_diff, w_dsa, w_idx, h_idx = prm["w_diff"], prm["w_dsa"], prm["w_idx"], prm["h_idx"]
    wg = prm["w_gate"].shape[1]
    act = F32 if sample else BF16
    row = lambda i: (i, 0)
    const = lambda i: (0, 0)
    pos = lambda i: (i % tiles, 0)
    full = lambda a: pl.BlockSpec(a.shape, const, pipeline_mode=pl.Buffered(1))
    weights = [prm[k] for k in ("g_mix", "w_qkv", "w_small", "w_gate", "g_qd", "g_kd", "g_qs", "g_ks", "bd")]
    if sample:
        weights += [prm["w_qkv_lo"], prm["w_small_lo"], prm["w_gate_lo"]]
    in_specs = ([pl.BlockSpec((tm, d), row), pl.BlockSpec((tm, LANES), pos), pl.BlockSpec((tm, LANES), pos)]
                + [full(w) for w in weights])

    def nat(w, dt):
        return jax.ShapeDtypeStruct((n, w), dt), pl.BlockSpec((tm, w), row)

    def tr_f(w):
        return (jax.ShapeDtypeStruct((bsz, w, t), F32),
                pl.BlockSpec((None, w, tm), lambda i: (i // tiles, 0, i % tiles)))

    def tr_b(w):
        return (jax.ShapeDtypeStruct((bsz, t // kc, w, kc), BF16),
                pl.BlockSpec((None, None, w, tm),
                             lambda i: (i // tiles, (i % tiles) // per_chunk, 0, (i % tiles) % per_chunk)))

    table = dict(qd=nat(w_diff, act), kdT_f=tr_f(w_diff), kdT_b=tr_b(w_diff), vd_f=nat(w_diff, F32),
                 vd_b=nat(w_diff, act), qs=nat(w_dsa, act), ksT_f=tr_f(w_dsa), ksT_b=tr_b(w_dsa),
                 vsT_f=tr_f(w_dsa), vs_b=nat(w_dsa, act), qi=nat(w_idx, act), kiT_f=tr_f(HEAD_DIM),
                 ki2T_b=tr_b(LANES), wi=nat(LANES, F32), gates=nat(wg, act),
                 kd_b=nat(w_diff, act), ks_b=nat(w_dsa, act), ki_b=nat(LANES, act))
    names = _INPROJ_OUTS + (_INPROJ_NATURAL if sample else ())
    kern = functools.partial(_inproj_kernel, names=names, precise=sample, w_diff=w_diff, w_dsa=w_dsa, w_idx=w_idx,
                             h_idx=h_idx, qk_scale=HEAD_DIM ** -0.5)
    outs = pl.pallas_call(kern, grid=(n // tm,), in_specs=in_specs,
                          out_specs=[table[k][1] for k in names], out_shape=[table[k][0] for k in names],
                          compiler_params=_params("parallel"), name="inproj")(x, cos_t, sin_t, *weights)
    return dict(zip(names, outs))


def _stack_halves(q):
    lane = lax.broadcasted_iota(I32, q.shape, 1)
    zero = jnp.zeros_like(q)
    return jnp.concatenate([jnp.where(lane < HEAD_DIM, q, zero), jnp.where(lane >= HEAD_DIM, q, zero)], axis=0)


def _lane_fold(a):
    out = a[:, :LANES]
    for c in range(LANES, a.shape[1], LANES):
        out = out + a[:, c:c + LANES]
    return out


def _online_update(s_ref, v, m_sc, l_sc, acc_sc, fin=None):
    s = s_ref[...]
    if fin is not None:
        s = fin(s, 0)
    m_prev = m_sc[...]
    m_new = jnp.maximum(m_prev, jnp.max(s, axis=-1, keepdims=True))
    alpha = jnp.exp2(m_prev - m_new)
    p = jnp.exp2(s - jnp.tile(m_new, (1, s.shape[1] // LANES)))
    l_sc[...] = alpha * l_sc[...] + _lane_fold(p)
    acc_sc[...] = alpha * acc_sc[...] + _dot(p.astype(BF16), v)
    m_sc[...] = m_new


def _finish(l_sc, acc_sc):
    return acc_sc[...] * (1.0 / jnp.sum(l_sc[...], axis=-1, keepdims=True))


def _attend_chunks(n, score, value, s_a, s_b, m_sc, l_sc, acc_sc, last=None):
    s_a[...] = score(0)

    def pair(p, carry):
        j = 2 * p
        s_b[...] = score(j + 1)
        _online_update(s_a, value(j), m_sc, l_sc, acc_sc)
        s_a[...] = score(j + 2)
        _online_update(s_b, value(j + 1), m_sc, l_sc, acc_sc)
        return carry

    lax.fori_loop(0, (n - 1) // 2, pair, 0)

    @pl.when(n % 2 == 1)
    def _():
        _online_update(s_a, value(n - 1), m_sc, l_sc, acc_sc, last)

    @pl.when(n % 2 == 0)
    def _():
        s_b[...] = score(n - 1)
        _online_update(s_a, value(n - 2), m_sc, l_sc, acc_sc)
        _online_update(s_b, value(n - 1), m_sc, l_sc, acc_sc, last)


def _lambda(lam_ref, lam_init):
    a = jnp.sum(lam_ref[0:1, :] * lam_ref[1:2, :], axis=-1, keepdims=True)
    b = jnp.sum(lam_ref[2:3, :] * lam_ref[3:4, :], axis=-1, keepdims=True)
    return jnp.exp(a) - jnp.exp(b) + lam_init


def _attn_scratch(rows, kc):
    return [pltpu.VMEM((rows, LANES), F32)] * 3 + [pltpu.VMEM((rows, kc), F32)] * 2


def _diff_attn_kernel(lam_ref, gsub_ref, q_ref, kt_ref, v_ref, o_ref, m_sc, l_sc, acc_sc, s_a, s_b, *,
                      tq, kc, lam_init):
    q0 = pl.program_id(2) * tq
    n_c = q0 // kc + 1
    q2 = _stack_halves(q_ref[...])
    m_sc[...] = jnp.full(m_sc.shape, -jnp.inf, F32)
    l_sc[...] = jnp.zeros(l_sc.shape, F32)
    acc_sc[...] = jnp.zeros(acc_sc.shape, F32)

    def causal(s, row0):
        r = lax.broadcasted_iota(I32, s.shape, 0) + row0
        r = jnp.where(r >= tq, r - tq, r) + q0
        c = lax.broadcasted_iota(I32, s.shape, 1) + (n_c - 1) * kc
        return jnp.where(c <= r, s, -jnp.inf)

    _attend_chunks(n_c, lambda j: _dot(q2, kt_ref[j]),
                   lambda j: v_ref[pl.ds(pl.multiple_of(j * kc, kc), kc), :],
                   s_a, s_b, m_sc, l_sc, acc_sc, last=causal)

    a = _finish(l_sc, acc_sc)
    o = a[:tq] - _lambda(lam_ref, lam_init) * a[tq:]
    ms = jnp.mean(o * o, axis=-1, keepdims=True)
    on = ((o * lax.rsqrt(ms + EPS)) * gsub_ref[...]) * (1.0 - lam_init)
    o_ref[...] = on.astype(BF16)


def _diff_attn(q, kt, v, lam_vecs, g_sub, lam_init, tq):
    b, t, w = q.shape
    nc, kc = kt.shape[1], kt.shape[3]
    assert kc % tq == 0
    grid = (b, w // LANES, t // tq)
    qmap = lambda bi, hi, qi: (bi, qi, hi)
    const = lambda bi, hi, qi: (0, 0)
    kern = functools.partial(_diff_attn_kernel, tq=tq, kc=kc, lam_init=lam_init)
    return pl.pallas_call(
        kern, grid=grid,
        in_specs=[pl.BlockSpec(lam_vecs.shape, const), pl.BlockSpec(g_sub.shape, const),
                  pl.BlockSpec((None, tq, LANES), qmap),
                  pl.BlockSpec((None, nc, LANES, kc), lambda bi, hi, qi: (bi, 0, hi, 0)),
                  pl.BlockSpec((None, t, LANES), lambda bi, hi, qi: (bi, 0, hi))],
        out_specs=pl.BlockSpec((None, tq, LANES), qmap),
        out_shape=jax.ShapeDtypeStruct((b, t, w), BF16),
        scratch_shapes=_attn_scratch(2 * tq, kc),
        compiler_params=_params("parallel", "parallel", "arbitrary"), name="diff_attn")(lam_vecs, g_sub, q, kt, v)


def _key_to_f32(k):
    b = k ^ (lax.shift_right_arithmetic(k, 31) & 0x7FFFFFFF)
    return lax.bitcast_convert_type(b, F32)


def _select_bias(score_sc, bias_sc, n_c, k_sel, row_pos, col0):
    _, r, kc = score_sc.shape

    def count(pred):
        def body(c, a):
            return a + _lane_fold(jnp.where(pred(score_sc[c]), 1.0, 0.0))
        a = lax.fori_loop(0, n_c, body, jnp.zeros((r, LANES), F32))
        return jnp.sum(a, axis=-1, keepdims=True)

    def bit_body(i, t):
        cand = t + lax.shift_left(jnp.int32(1), 31 - i)
        cand_f = _key_to_f32(cand)
        cnt = count(lambda sc: sc >= cand_f)
        return jnp.where(cnt >= k_sel, cand, t)

    t = lax.fori_loop(0, 32, bit_body, jnp.full((r, 1), INT_MIN, I32))
    thr = _key_to_f32(jnp.maximum(t, KEY_NEG_INF))
    need = k_sel - count(lambda sc: sc > thr)

    ri = lax.broadcasted_iota(I32, (kc, kc), 0)
    ci = lax.broadcasted_iota(I32, (kc, kc), 1)
    tri = jnp.where(ri <= ci, 1.0, 0.0).astype(BF16)
    col = lax.broadcasted_iota(I32, (r, kc), 1)

    def bias_body(c, run):
        sc = score_sc[c]
        eq = sc == thr
        pre = _dot(jnp.where(eq, 1.0, 0.0).astype(BF16), tri)
        sel = (sc > thr) | (eq & ((pre + run) <= need))
        causal = (col + (col0 + c * kc)) <= row_pos
        bias_sc[c] = jnp.where(sel & causal, 0.0, MASK_NEG)
        return run + pre[:, kc - 1:kc]

    lax.fori_loop(0, n_c, bias_body, jnp.zeros((r, 1), F32))


def _dsa_attn_kernel(qi_ref, ki2t_ref, wi_ref, qs_ref, kst_ref, vs_ref, o_ref,
                     qstack_sc, score_sc, bias_sc, q2_sc, m_sc, l_sc, acc_sc, *, tq, kc, k_sel, h_idx):
    qb = pl.program_id(1)
    q0 = qb * tq
    n_c = q0 // kc + 1
    row_pos = q0 + lax.broadcasted_iota(I32, (tq, 1), 0)

    for hh in range(h_idx):
        pair = qi_ref[:, (hh // 2) * LANES:(hh // 2 + 1) * LANES]
        lane = lax.broadcasted_iota(I32, pair.shape, 1)
        keep = (lane < HEAD_DIM) if hh % 2 == 0 else (lane >= HEAD_DIM)
        qstack_sc[hh * tq:(hh + 1) * tq, :] = jnp.where(keep, pair, jnp.zeros_like(pair))
    wi = wi_ref[...]

    def score_body(c, carry):
        d = _dot(qstack_sc[...], ki2t_ref[c])
        acc = jnp.zeros((tq, kc), F32)
        for hh in range(h_idx):
            acc = acc + jnp.maximum(d[hh * tq:(hh + 1) * tq], 0.0) * wi[:, HEAD_DIM + hh:HEAD_DIM + hh + 1]
        col = c * kc + lax.broadcasted_iota(I32, (tq, kc), 1)
        score_sc[c] = jnp.where(col <= row_pos, acc, -jnp.inf)
        return carry

    lax.fori_loop(0, n_c, score_body, 0)
    _select_bias(score_sc, bias_sc, n_c, k_sel, row_pos, 0)

    n_hp = qs_ref.shape[1] // LANES
    for hp in range(n_hp):
        q2_sc[hp] = _stack_halves(qs_ref[:, hp * LANES:(hp + 1) * LANES])
    m_sc[...] = jnp.full(m_sc.shape, MASK_NEG, F32)
    l_sc[...] = jnp.zeros(l_sc.shape, F32)
    acc_sc[...] = jnp.zeros(acc_sc.shape, F32)

    def attn_body(c, carry):
        bias = bias_sc[c]
        bias2 = jnp.concatenate([bias, bias], axis=0)
        rows = pl.ds(pl.multiple_of(c * kc, kc), kc)
        for hp in range(n_hp):
            cols = slice(hp * LANES, (hp + 1) * LANES)
            s = _dot(q2_sc[hp], kst_ref[c, cols, :]) + bias2
            _online_update(s, vs_ref[rows, cols], m_sc.at[hp], l_sc.at[hp], acc_sc.at[hp])
        return carry

    lax.fori_loop(0, n_c, attn_body, 0)
    lane = lax.broadcasted_iota(I32, (tq, LANES), 1)
    for hp in range(n_hp):
        a = _finish(l_sc.at[hp], acc_sc.at[hp])
        o_ref[:, hp * LANES:(hp + 1) * LANES] = jnp.where(lane < HEAD_DIM, a[:tq], a[tq:]).astype(BF16)


def _dsa_attn(qi, ki2t, wi, qs, kst, vs, k_sel, h_idx, tq):
    b, t, w = qs.shape
    nc, kc = kst.shape[1], kst.shape[3]
    assert kc % tq == 0 and kc >= k_sel
    qmap = lambda bi, qb: (bi, qb, 0)
    kern = functools.partial(_dsa_attn_kernel, tq=tq, kc=kc, k_sel=k_sel, h_idx=h_idx)
    return pl.pallas_call(
        kern, grid=(b, t // tq),
        in_specs=[pl.BlockSpec((None, tq, qi.shape[2]), qmap),
                  pl.BlockSpec((None, nc, LANES, kc), lambda bi, qb: (bi, 0, 0, 0)),
                  pl.BlockSpec((None, tq, LANES), qmap), pl.BlockSpec((None, tq, w), qmap),
                  pl.BlockSpec((None, nc, w, kc), lambda bi, qb: (bi, 0, 0, 0)),
                  pl.BlockSpec((None, t, w), lambda bi, qb: (bi, 0, 0))],
        out_specs=pl.BlockSpec((None, tq, w), qmap),
        out_shape=jax.ShapeDtypeStruct((b, t, w), BF16),
        scratch_shapes=[pltpu.VMEM((h_idx * tq, LANES), BF16), pltpu.VMEM((nc, tq, kc), F32),
                        pltpu.VMEM((nc, tq, kc), F32), pltpu.VMEM((w // LANES, 2 * tq, LANES), BF16)]
                       + [pltpu.VMEM((w // LANES, 2 * tq, LANES), F32)] * 3,
        compiler_params=_params("parallel", "arbitrary"), name="dsa_attn")(qi, ki2t, wi, qs, kst, vs)


def _page_specs(n_pages, block):
    def spec(j):
        return pl.BlockSpec((None,) + block, lambda b, pt, j=j: (pt[b, j],) + (0,) * len(block))
    return [spec(j) for j in range(n_pages)]


def _idx_score_kernel(pt_ref, q_ref, w_ref, knew_ref, *refs, n_pages, page, pad):
    page_refs, o_ref = refs[:n_pages], refs[n_pages]
    q = q_ref[...]
    w = w_ref[...]
    for j in range(n_pages):
        d = _dot3(q, page_refs[j][...])
        o_ref[:, j * page:(j + 1) * page] = jnp.sum(jnp.maximum(d, 0.0) * w, axis=0, keepdims=True)
    d_new = jnp.sum(q * knew_ref[...], axis=-1, keepdims=True)
    s_new = jnp.sum(jnp.maximum(d_new, 0.0) * w, axis=0, keepdims=True)
    lane = lax.broadcasted_iota(I32, (1, pad), 1)
    o_ref[:, n_pages * page:] = jnp.where(lane == 0, s_new, -jnp.inf)


def _idx_scores(page_table, q_i, w_i, k_new, cache_t, pad):
    db, n_pages = page_table.shape
    _, dh, page = cache_t.shape
    h_idx = q_i.shape[1]
    tk = n_pages * page + pad
    seq = lambda b, pt: (b, 0, 0)
    kern = functools.partial(_idx_score_kernel, n_pages=n_pages, page=page, pad=pad)
    grid_spec = pltpu.PrefetchScalarGridSpec(
        num_scalar_prefetch=1, grid=(db,),
        in_specs=[pl.BlockSpec((None, h_idx, dh), seq), pl.BlockSpec((None, h_idx, 1), seq),
                  pl.BlockSpec((None, 1, dh), seq)] + _page_specs(n_pages, (dh, page)),
        out_specs=pl.BlockSpec((None, 1, tk), seq))
    out = pl.pallas_call(kern, grid_spec=grid_spec, out_shape=jax.ShapeDtypeStruct((db, 1, tk), F32),
                         compiler_params=_params("parallel"), name="idx_scores")(
        page_table, q_i, w_i, k_new, *([cache_t] * n_pages))
    return out.reshape(db, tk)


def _select_kernel(s_ref, o_ref, score_sc, bias_sc, *, kc, k_sel, q_pos):
    r = s_ref.shape[0]
    nc = s_ref.shape[1] // kc
    for c in range(nc):
        score_sc[c] = s_ref[:, c * kc:(c + 1) * kc]
    row_pos = jnp.full((r, 1), q_pos, I32)
    _select_bias(score_sc, bias_sc, nc, k_sel, row_pos, 0)
    for c in range(nc):
        o_ref[:, c * kc:(c + 1) * kc] = bias_sc[c]


def _select(scores, k_sel, q_pos, kc):
    r, tk = scores.shape
    nc = tk // kc
    kern = functools.partial(_select_kernel, kc=kc, k_sel=k_sel, q_pos=q_pos)
    return pl.pallas_call(
        kern, grid=(1,), in_specs=[pl.BlockSpec((r, tk), lambda i: (0, 0))],
        out_specs=pl.BlockSpec((r, tk), lambda i: (0, 0)), out_shape=jax.ShapeDtypeStruct((r, tk), F32),
        scratch_shapes=[pltpu.VMEM((nc, r, kc), F32), pltpu.VMEM((nc, r, kc), F32)],
        compiler_params=_params("arbitrary"), name="select")(scores)


def _block_diag_rows(q, width):
    t = jnp.concatenate([q] * (width // HEAD_DIM), axis=1)
    row = lax.broadcasted_iota(I32, t.shape, 0)
    col = lax.broadcasted_iota(I32, t.shape, 1)
    return jnp.where((col >= row * HEAD_DIM) & (col < (row + 1) * HEAD_DIM), t, jnp.zeros_like(t))


def _decode_attn_kernel(pt_ref, lam_ref, gsub_ref, qd_ref, kdn_ref, vdn_ref, qs_ref, ksn_ref, vsn_ref, bias_ref,
                        *refs, n_pages, page, lam_init):
    kd_pages = refs[:n_pages]
    vd_pages = refs[n_pages:2 * n_pages]
    ks_pages = refs[2 * n_pages:3 * n_pages]
    vs_pages = refs[3 * n_pages:4 * n_pages]
    od_ref, os_ref = refs[4 * n_pages], refs[4 * n_pages + 1]
    h_diff = vd_pages[0].shape[0] // page

    def probs(q_ref, knew_ref, k_pages, bias):
        q = q_ref[...]
        qbd = _block_diag_rows(q, k_pages[0].shape[0])
        s = [_dot3(qbd, k_pages[j][...]) for j in range(n_pages)]
        s_new = jnp.sum(q * knew_ref[...], axis=-1, keepdims=True)
        if bias is not None:
            s = [s[j] + bias[:, j * page:(j + 1) * page] for j in range(n_pages)]
            s_new = s_new + bias[:, n_pages * page:n_pages * page + 1]
        m = s_new
        for j in range(n_pages):
            m = jnp.maximum(m, jnp.max(s[j], axis=-1, keepdims=True))
        p_new = jnp.exp2(s_new - m)
        p = [jnp.exp2(s[j] - m) for j in range(n_pages)]
        l = p_new
        for j in range(n_pages):
            l = l + jnp.sum(p[j], axis=-1, keepdims=True)
        return p, p_new, 1.0 / l

    p, p_new, inv_l = probs(qd_ref, kdn_ref, kd_pages, None)
    acc = p_new * vdn_ref[...]
    for j in range(n_pages):
        v = jnp.concatenate([vd_pages[j][pl.ds(hh, page, stride=h_diff), :] for hh in range(h_diff)], axis=1)
        acc = acc + _dot3(p[j], v)
    a = acc * inv_l
    row = lax.broadcasted_iota(I32, a.shape, 0)
    col = lax.broadcasted_iota(I32, a.shape, 1)
    own = (col // LANES) == (row // 2)
    coef = jnp.where(row % 2 == 0, 1.0, -_lambda(lam_ref, lam_init))
    o = jnp.sum(jnp.where(own, a * coef, 0.0), axis=0, keepdims=True)
    parts = []
    for c in range(0, o.shape[1], LANES):
        oc = o[:, c:c + LANES]
        ms = jnp.mean(oc * oc, axis=-1, keepdims=True)
        parts.append(((oc * lax.rsqrt(ms + EPS)) * gsub_ref[...]) * (1.0 - lam_init))
    od_ref[...] = jnp.concatenate(parts, axis=1)

    p, p_new, inv_l = probs(qs_ref, ksn_ref, ks_pages, bias_ref[...])
    acc = p_new * vsn_ref[...]
    for j in range(n_pages):
        acc = acc + _dot3(p[j], vs_pages[j][...], nt=True)
    a = acc * inv_l
    row = lax.broadcasted_iota(I32, a.shape, 0)
    col = lax.broadcasted_iota(I32, a.shape, 1)
    os_ref[...] = jnp.sum(jnp.where((col // HEAD_DIM) == row, a, 0.0), axis=0, keepdims=True)


def _decode_attn(page_table, lam_vecs, g_sub, qd, kdn, vdn, qs, ksn, vsn, bias, caches, lam_init):
    db, n_pages = page_table.shape
    c_kd, c_vd, c_ks, c_vs = caches
    w_diff, page = c_kd.shape[1:]
    w_dsa = c_ks.shape[1]
    seq = lambda b, pt: (b, 0, 0)
    const = lambda b, pt: (0, 0)
    in_specs = [pl.BlockSpec(lam_vecs.shape, const), pl.BlockSpec(g_sub.shape, const),
                pl.BlockSpec((None,) + qd.shape[1:], seq), pl.BlockSpec((None,) + kdn.shape[1:], seq),
                pl.BlockSpec((None,) + vdn.shape[1:], seq),
                pl.BlockSpec((None,) + qs.shape[1:], seq), pl.BlockSpec((None,) + ksn.shape[1:], seq),
                pl.BlockSpec((None,) + vsn.shape[1:], seq), pl.BlockSpec((None,) + bias.shape[1:], seq)]
    for cache in caches:
        in_specs += _page_specs(n_pages, tuple(cache.shape[1:]))
    kern = functools.partial(_decode_attn_kernel, n_pages=n_pages, page=page, lam_init=lam_init)
    grid_spec = pltpu.PrefetchScalarGridSpec(
        num_scalar_prefetch=1, grid=(db,), in_specs=in_specs,
        out_specs=[pl.BlockSpec((None, 1, w_diff), seq), pl.BlockSpec((None, 1, w_dsa), seq)])
    pages = []
    for cache in caches:
        pages += [cache] * n_pages
    od, os_ = pl.pallas_call(
        kern, grid_spec=grid_spec,
        out_shape=[jax.ShapeDtypeStruct((db, 1, w_diff), F32), jax.ShapeDtypeStruct((db, 1, w_dsa), F32)],
        compiler_params=_params("parallel"), name="decode_attn")(
        page_table, lam_vecs, g_sub, qd, kdn, vdn, qs, ksn, vsn, bias, *pages)
    return od.reshape(db, w_diff), os_.reshape(db, w_dsa)


ROUTE_E1, ROUTE_E2, ROUTE_R1, ROUTE_R2, ROUTE_W1, ROUTE_W2 = range(6)


def _outproj_kernel(x_ref, od_ref, os_ref, gate_ref, wod_ref, wos_ref, wout_ref, gffn_ref, wrh_ref, wrl_ref, *refs,
                    precise, n_groups, n_experts):
    wod_lo, wos_lo, wout_lo = refs[:3] if precise else (None, None, None)
    x1_ref, h2_ref, comb_ref, route_ref, count_ref = refs[3:] if precise else refs
    d = x_ref.shape[1]

    @pl.when(pl.program_id(0) == 0)
    def _():
        count_ref[...] = jnp.zeros(count_ref.shape, F32)

    yd = _mm(od_ref[...], wod_ref, wod_lo)
    ys = _mm(os_ref[...], wos_ref, wos_lo)
    merged = gate_ref[:, :d].astype(F32) * yd + gate_ref[:, d:].astype(F32) * ys
    x1 = x_ref[...] + _mm(merged, wout_ref, wout_lo)
    x1_ref[...] = x1

    ms = jnp.mean(x1 * x1, axis=-1, keepdims=True)
    h2 = (x1 * lax.rsqrt(ms + EPS)) * gffn_ref[...]
    h2_ref[...] = h2

    hi, lo = _split_bf16(h2)
    lg = _dot(hi, wrh_ref[...]) + (_dot(hi, wrl_ref[...]) + _dot(lo, wrh_ref[...]))
    lane = lax.broadcasted_iota(I32, lg.shape, 1)
    lanef = lane.astype(F32)
    big = float(LANES)
    epg = n_experts // n_groups

    gmask = (lane >= n_experts) & (lane < n_experts + n_groups)
    gl = jnp.where(gmask, lg, -jnp.inf)
    gmax = jnp.max(gl, axis=-1, keepdims=True)
    gsel = jnp.min(jnp.where(gl == gmax, lanef, big), axis=-1, keepdims=True) - n_experts
    g_w = 1.0 / jnp.sum(jnp.exp(gl - gmax), axis=-1, keepdims=True)

    lo_lane = gsel * epg
    emask = (lanef >= lo_lane) & (lanef < lo_lane + epg)
    el = jnp.where(emask, lg, -jnp.inf)
    v1 = jnp.max(el, axis=-1, keepdims=True)
    i1 = jnp.min(jnp.where(el == v1, lanef, big), axis=-1, keepdims=True)
    el2 = jnp.where(lanef == i1, -jnp.inf, el)
    v2 = jnp.max(el2, axis=-1, keepdims=True)
    i2 = jnp.min(jnp.where(el2 == v2, lanef, big), axis=-1, keepdims=True)
    e21 = jnp.exp(v2 - v1)
    den = 1.0 / (1.0 + e21)
    w1 = den * g_w
    w2 = (e21 * den) * g_w
    comb_ref[...] = jnp.where(lanef == i1, w1, 0.0) + jnp.where(lanef == i2, w2, 0.0)

    tm = lg.shape[0]
    onehot = jnp.where((lanef == i1) | (lanef == i2), 1.0, 0.0)
    ri = lax.broadcasted_iota(I32, (tm, tm), 0)
    ci = lax.broadcasted_iota(I32, (tm, tm), 1)
    before = jnp.where(ci < ri, 1.0, 0.0).astype(BF16)
    rank_e = count_ref[...] + _dot(before, onehot.astype(BF16))
    r1 = jnp.sum(jnp.where(lanef == i1, rank_e, 0.0), axis=-1, keepdims=True)
    r2 = jnp.sum(jnp.where(lanef == i2, rank_e, 0.0), axis=-1, keepdims=True)
    count_ref[...] += jnp.sum(onehot, axis=0, keepdims=True)
    rec = jnp.zeros(lg.shape, F32)
    for slot, val in ((ROUTE_E1, i1), (ROUTE_E2, i2), (ROUTE_W1, w1), (ROUTE_W2, w2), (ROUTE_R1, r1), (ROUTE_R2, r2)):
        rec = jnp.where(lane == slot, val, rec)
    route_ref[...] = rec


def _outproj(x, od, os_, gates, prm, tm, sample):
    n, d = x.shape
    row = lambda i: (i, 0)
    const = lambda i: (0, 0)
    full = lambda a: pl.BlockSpec(a.shape, const)
    weights = [prm[k] for k in ("w_od", "w_os", "w_out", "g_ffn", "w_rt_hi", "w_rt_lo")]
    if sample:
        weights += [prm["w_od_lo"], prm["w_os_lo"], prm["w_out_lo"]]
    kern = functools.partial(_outproj_kernel, precise=sample, n_groups=prm["n_groups"], n_experts=prm["n_experts"])
    return pl.pallas_call(
        kern, grid=(n // tm,),
        in_specs=[pl.BlockSpec((tm, d), row), pl.BlockSpec((tm, od.shape[1]), row), pl.BlockSpec((tm, os_.shape[1]), row),
                  pl.BlockSpec((tm, gates.shape[1]), row)] + [full(w) for w in weights],
        out_specs=[pl.BlockSpec((tm, d), row), pl.BlockSpec((tm, d), row), pl.BlockSpec((tm, LANES), row),
                   pl.BlockSpec((tm, LANES), row), pl.BlockSpec((1, LANES), const)],
        out_shape=[jax.ShapeDtypeStruct((n, d), F32), jax.ShapeDtypeStruct((n, d), F32),
                   jax.ShapeDtypeStruct((n, LANES), F32), jax.ShapeDtypeStruct((n, LANES), F32),
                   jax.ShapeDtypeStruct((1, LANES), F32)],
        compiler_params=_params("arbitrary"), name="outproj")(x, od, os_, gates, *weights)


def _expert_ffn(h, wg_ref, wu_ref, wd_ref):
    h = h.astype(BF16)
    a = _dot(h, wg_ref[...].astype(BF16))
    u = _dot(h, wu_ref[...].astype(BF16))
    act = (a * (1.0 / (1.0 + jnp.exp(-a)))) * u
    return _dot(act.astype(BF16), wd_ref[...].astype(BF16))


def _moe_kernel(x1_ref, h2_ref, comb_ref, wg_ref, wu_ref, wd_ref, y_ref, acc_sc):
    e = pl.program_id(1)

    @pl.when(e == 0)
    def _():
        acc_sc[...] = jnp.zeros(acc_sc.shape, F32)

    o = _expert_ffn(h2_ref[...], wg_ref, wu_ref, wd_ref)
    comb = comb_ref[...]
    lane = lax.broadcasted_iota(I32, comb.shape, 1)
    c = jnp.sum(jnp.where(lane == e, comb, 0.0), axis=-1, keepdims=True)
    acc_sc[...] += o * c

    @pl.when(e == pl.num_programs(1) - 1)
    def _():
        y_ref[...] = x1_ref[...] + acc_sc[...]


def _moe(x1, h2, comb, w_gate, w_up, w_down, tm):
    n, d = x1.shape
    ne, _, de = w_gate.shape
    row = lambda i, e: (i, 0)
    return pl.pallas_call(
        _moe_kernel, grid=(n // tm, ne),
        in_specs=[pl.BlockSpec((tm, d), row), pl.BlockSpec((tm, d), row), pl.BlockSpec((tm, LANES), row),
                  pl.BlockSpec((None, d, de), lambda i, e: (e, 0, 0)), pl.BlockSpec((None, d, de), lambda i, e: (e, 0, 0)),
                  pl.BlockSpec((None, de, d), lambda i, e: (e, 0, 0))],
        out_specs=pl.BlockSpec((tm, d), row), out_shape=jax.ShapeDtypeStruct((n, d), F32),
        scratch_shapes=[pltpu.VMEM((tm, d), F32)],
        compiler_params=_params("parallel", "arbitrary"), name="moe")(x1, h2, comb, w_gate, w_up, w_down)


def _row_copy(src, src_row, dst, dst_row, sem):
    return pltpu.make_async_copy(src.at[pl.ds(src_row, 1), :], dst.at[pl.ds(dst_row, 1), :], sem)


def _slot_rows(assign_ref, offs_ref, n, tok):
    return (offs_ref[assign_ref[tok]] + assign_ref[2 * n + tok],
            offs_ref[assign_ref[n + tok]] + assign_ref[3 * n + tok])


def _scatter_kernel(assign_ref, offs_ref, group_ref, n_tiles_ref, h2_ref, xs_hbm, zero_sc, sem, zero_sem, *,
                    n, tm, tr):
    base = pl.program_id(0) * tm
    n_experts = offs_ref.shape[0]
    total_tiles = xs_hbm.shape[0] // tr

    def zero_tile(row0):
        return pltpu.make_async_copy(zero_sc, xs_hbm.at[pl.ds(pl.multiple_of(row0, tr), tr), :], zero_sem)

    @pl.when(pl.program_id(0) == 0)
    def _():
        zero_sc[...] = jnp.zeros(zero_sc.shape, F32)
        for wait in (False, True):
            def group_tail(e, carry, wait=wait):
                @pl.when(group_ref[e] > 0)
                def _():
                    cp = zero_tile(offs_ref[e] + group_ref[e] - tr)
                    cp.wait() if wait else cp.start()
                return carry

            def unused(j, carry, wait=wait):
                cp = zero_tile(j * tr)
                cp.wait() if wait else cp.start()
                return carry

            lax.fori_loop(0, n_experts, group_tail, 0)
            lax.fori_loop(n_tiles_ref[0], total_tiles, unused, 0)

    def start(t, carry):
        p1, p2 = _slot_rows(assign_ref, offs_ref, n, base + t)
        _row_copy(h2_ref, t, xs_hbm, p1, sem).start()
        _row_copy(h2_ref, t, xs_hbm, p2, sem).start()
        return carry

    def wait(t, carry):
        p1, p2 = _slot_rows(assign_ref, offs_ref, n, base + t)
        _row_copy(h2_ref, t, xs_hbm, p1, sem).wait()
        _row_copy(h2_ref, t, xs_hbm, p2, sem).wait()
        return carry

    lax.fori_loop(0, tm, start, 0, unroll=8)
    lax.fori_loop(0, tm, wait, 0, unroll=8)


def _scatter_rows(assign, offs, group, n_tiles, h2, n_rows, tm, tr):
    n, d = h2.shape
    grid_spec = pltpu.PrefetchScalarGridSpec(
        num_scalar_prefetch=4, grid=(n // tm,),
        in_specs=[pl.BlockSpec((tm, d), lambda i, *_: (i, 0))], out_specs=pl.BlockSpec(memory_space=pl.ANY),
        scratch_shapes=[pltpu.VMEM((tr, d), F32), pltpu.SemaphoreType.DMA(()), pltpu.SemaphoreType.DMA(())])
    return pl.pallas_call(
        functools.partial(_scatter_kernel, n=n, tm=tm, tr=tr), grid_spec=grid_spec,
        out_shape=jax.ShapeDtypeStruct((n_rows, d), F32),
        compiler_params=_params("arbitrary", disable_bounds_checks=True), name="moe_scatter")(
        assign, offs, group, n_tiles, h2)


def _grouped_ffn_kernel(tile_expert_ref, n_tiles_ref, xs_ref, wg_ref, wu_ref, wd_ref, ys_ref):
    used = pl.program_id(0) < n_tiles_ref[0]

    @pl.when(used)
    def _():
        ys_ref[...] = _expert_ffn(xs_ref[...], wg_ref, wu_ref, wd_ref)

    @pl.when(jnp.logical_not(used))
    def _():
        ys_ref[...] = jnp.zeros(ys_ref.shape, F32)


def _grouped_ffn(tile_expert, n_tiles, xs, w_gate, w_up, w_down, tr):
    rows, d = xs.shape
    ne, _, de = w_gate.shape
    last = lambda i, te, nt: jnp.minimum(i, nt[0] - 1)
    xmap = lambda i, te, nt: (last(i, te, nt), 0)
    wmap = lambda i, te, nt: (te[last(i, te, nt)], 0, 0)
    grid_spec = pltpu.PrefetchScalarGridSpec(
        num_scalar_prefetch=2, grid=(rows // tr,),
        in_specs=[pl.BlockSpec((tr, d), xmap), pl.BlockSpec((None, d, de), wmap), pl.BlockSpec((None, d, de), wmap),
                  pl.BlockSpec((None, de, d), wmap)],
        out_specs=pl.BlockSpec((tr, d), lambda i, te, nt: (i, 0)))
    return pl.pallas_call(
        _grouped_ffn_kernel, grid_spec=grid_spec, out_shape=jax.ShapeDtypeStruct((rows, d), F32),
        compiler_params=_params("arbitrary"), name="moe_ffn")(tile_expert, n_tiles, xs, w_gate, w_up, w_down)


def _combine_kernel(assign_ref, offs_ref, x1_ref, route_ref, ys_hbm, y_ref, buf1, buf2, sem, *, n, tm):
    base = pl.program_id(0) * tm

    def start(t, carry):
        p1, p2 = _slot_rows(assign_ref, offs_ref, n, base + t)
        _row_copy(ys_hbm, p1, buf1, t, sem).start()
        _row_copy(ys_hbm, p2, buf2, t, sem).start()
        return carry

    def wait(t, carry):
        p1, p2 = _slot_rows(assign_ref, offs_ref, n, base + t)
        _row_copy(ys_hbm, p1, buf1, t, sem).wait()
        _row_copy(ys_hbm, p2, buf2, t, sem).wait()
        return carry

    lax.fori_loop(0, tm, start, 0, unroll=8)
    lax.fori_loop(0, tm, wait, 0, unroll=8)
    route = route_ref[...]
    w1 = route[:, ROUTE_W1:ROUTE_W1 + 1]
    w2 = route[:, ROUTE_W2:ROUTE_W2 + 1]
    y_ref[...] = (x1_ref[...] + buf1[...] * w1) + buf2[...] * w2


def _combine_rows(assign, offs, x1, route, ys, tm):
    n, d = x1.shape
    row = lambda i, *_: (i, 0)
    grid_spec = pltpu.PrefetchScalarGridSpec(
        num_scalar_prefetch=2, grid=(n // tm,),
        in_specs=[pl.BlockSpec((tm, d), row), pl.BlockSpec((tm, LANES), row), pl.BlockSpec(memory_space=pl.ANY)],
        out_specs=pl.BlockSpec((tm, d), row),
        scratch_shapes=[pltpu.VMEM((tm, d), F32), pltpu.VMEM((tm, d), F32), pltpu.SemaphoreType.DMA(())])
    return pl.pallas_call(
        functools.partial(_combine_kernel, n=n, tm=tm), grid_spec=grid_spec,
        out_shape=jax.ShapeDtypeStruct((n, d), F32),
        compiler_params=_params("arbitrary", disable_bounds_checks=True), name="moe_combine")(
        assign, offs, x1, route, ys)


def _routed_moe(x1, h2, route, counts, w_gate, w_up, w_down, n_experts, tm, tr):
    n, d = x1.shape
    cnt = counts[0, :n_experts].astype(I32)
    group = ((cnt + tr - 1) // tr) * tr
    ends = jnp.cumsum(group)
    offs = ends - group
    assign = route[:, ROUTE_E1:ROUTE_R2 + 1].astype(I32).T.reshape(-1)
    n_rows = 2 * n + n_experts * tr
    tile_start = jnp.arange(n_rows // tr, dtype=I32) * tr
    tile_expert = jnp.minimum(jnp.sum(tile_start[:, None] >= ends[None, :], axis=1), n_experts - 1).astype(I32)
    n_tiles = (ends[-1:] // tr).astype(I32)
    xs = _scatter_rows(assign, offs, group, n_tiles, h2, n_rows, tm, tr)
    ys = _grouped_ffn(tile_expert, n_tiles, xs, w_gate, w_up, w_down, tr)
    return _combine_rows(assign, offs, x1, route, ys, tm)


def _rope_tables(pos):
    half = HEAD_DIM // 2
    inv_freq = jnp.power(ROPE_THETA, -jnp.arange(half, dtype=F32) / half)
    ang = pos.astype(F32)[:, None] * inv_freq[None, :]
    cos, sin = jnp.cos(ang), jnp.sin(ang)
    reps = LANES // HEAD_DIM
    return jnp.tile(cos, (1, 2 * reps)), jnp.tile(jnp.concatenate([-sin, sin], axis=1), (1, reps))


def _tile_rows(n, target):
    t = min(n, target)
    while n % t:
        t //= 2
    return t


def _key_outputs(o, bsz, t, h_diff, h_dsa, dh):
    kd = o["kdT_f"].reshape(bsz, h_diff, 2, dh, t).transpose(0, 4, 1, 2, 3)
    vd = o["vd_f"].reshape(bsz, t, h_diff, 2 * dh)
    ks = o["ksT_f"].reshape(bsz, h_dsa, dh, t).transpose(0, 3, 1, 2)
    vs = o["vsT_f"].reshape(bsz, h_dsa, dh, t).transpose(0, 3, 1, 2)
    ki = o["kiT_f"].transpose(0, 2, 1)
    return kd, vd, ks, vs, ki


def kernel(x_prompt, x_sample, cache_diff_k, cache_diff_v, cache_dsa_k, cache_dsa_v, cache_idx_k, page_table, g_mix, w_in, g_q_diff, g_k_diff, lam_q1, lam_k1, lam_q2, lam_k2, g_subln, g_q_dsa, g_k_dsa, w_o_diff, w_o_dsa, w_out, g_ffn, w_group, w_router, w_gate, w_up, w_down):
    bsz, seq, d_model = x_prompt.shape
    db, dec_seq, _ = x_sample.shape
    depth, n_pool, page, h_diff, _, dh = cache_diff_k.shape
    h_dsa = cache_dsa_k.shape[3]
    n_pages = page_table.shape[1]
    past_len = n_pages * page
    n_groups = w_group.shape[2]
    n_experts = w_router.shape[2]
    w_diff = h_diff * 2 * dh
    w_dsa = h_dsa * dh
    h_idx = (w_in.shape[2] - (3 * w_diff + 3 * w_dsa + dh + 2 * d_model)) // (dh + 1)
    w_idx = h_idx * dh
    assert dh == HEAD_DIM and cache_dsa_k.shape[4] == HEAD_DIM and cache_idx_k.shape[3] == HEAD_DIM
    assert dec_seq == 1 and depth == 1
    assert 3 * w_diff + 3 * w_dsa + w_idx + dh + h_idx + 2 * d_model == w_in.shape[2]
    k_sel_p = min(TOPK_MAX, seq // 4)
    k_sel_s = min(TOPK_MAX, (past_len + dec_seq) // 4)

    l = 0
    lam_init = 0.8 - 0.6 * math.exp(-0.3 * l)
    n_qkv = 3 * w_diff + 3 * w_dsa + w_idx
    w_l = w_in[l]
    rt = jnp.concatenate([w_router[l], w_group[l]], axis=1)
    rt = jnp.pad(rt, ((0, 0), (0, LANES - rt.shape[1])))
    blk = lax.broadcasted_iota(I32, (2 * LANES, 2 * LANES), 0) // HEAD_DIM
    blk_c = lax.broadcasted_iota(I32, (2 * LANES, 2 * LANES), 1) // HEAD_DIM
    prm = dict(
        w_diff=w_diff, w_dsa=w_dsa, w_idx=w_idx, h_idx=h_idx, n_groups=n_groups, n_experts=n_experts,
        g_mix=g_mix[l][None, :],
        g_qd=jnp.tile(g_q_diff[l], w_diff // dh)[None, :], g_kd=jnp.tile(g_k_diff[l], w_diff // dh)[None, :],
        g_qs=jnp.tile(g_q_dsa[l], w_dsa // dh)[None, :], g_ks=jnp.tile(g_k_dsa[l], w_dsa // dh)[None, :],
        bd=(blk == blk_c).astype(BF16), g_ffn=g_ffn[l][None, :])
    for name, w in (("w_qkv", w_l[:, :n_qkv]),
                    ("w_small", jnp.pad(w_l[:, n_qkv:n_qkv + dh + h_idx], ((0, 0), (0, LANES - dh - h_idx)))),
                    ("w_gate", w_l[:, n_qkv + dh + h_idx:]),
                    ("w_od", w_o_diff[l]), ("w_os", w_o_dsa[l]), ("w_out", w_out[l]), ("w_rt", rt)):
        hi = lax.reduce_precision(w, exponent_bits=8, mantissa_bits=7)
        prm[name] = hi.astype(BF16)
        prm[name + "_lo"] = (w - hi).astype(BF16)
    prm["w_rt_hi"] = prm["w_rt"]
    lam_vecs = jnp.stack([lam_q1[l], lam_k1[l], lam_q2[l], lam_k2[l]])
    g_sub = g_subln[l][None, :]
    moe_w = (w_gate[l], w_up[l], w_down[l])

    n_p = bsz * seq
    tm = _tile_rows(seq, ROW_TILE)
    kc = _tile_rows(seq, KEY_CHUNK)
    cos_p, sin_p = _rope_tables(jnp.arange(seq, dtype=I32))
    x_p = x_prompt.reshape(n_p, d_model)
    o = _inproj(x_p, cos_p, sin_p, bsz, tm, kc, prm, sample=False)
    r3 = lambda a: a.reshape(bsz, seq, a.shape[1])
    od = _diff_attn(r3(o["qd"]), o["kdT_b"], r3(o["vd_b"]), lam_vecs, g_sub, lam_init, tm)
    os_ = _dsa_attn(r3(o["qi"]), o["ki2T_b"], r3(o["wi"]), r3(o["qs"]), o["ksT_b"], r3(o["vs_b"]),
                    k_sel_p, h_idx, _tile_rows(seq, DSA_Q_TILE))
    x1, h2, _, route, counts = _outproj(x_p, od.reshape(n_p, w_diff), os_.reshape(n_p, w_dsa), o["gates"], prm, tm,
                                        sample=False)
    y_p = _routed_moe(x1, h2, route, counts, *moe_w, n_experts, tm, MOE_ROW_TILE)
    outs_p = _key_outputs(o, bsz, seq, h_diff, h_dsa, dh)

    cos_s, sin_s = _rope_tables(jnp.full((db,), past_len, I32))
    x_s = x_sample.reshape(db, d_model)
    o = _inproj(x_s, cos_s, sin_s, 1, db, db, prm, sample=True)
    pad = 2 * LANES
    assert pad >= k_sel_s
    scores = _idx_scores(page_table, o["qi"].reshape(db, h_idx, dh), o["wi"][:, dh:dh + h_idx].reshape(db, h_idx, 1),
                         o["ki_b"][:, :dh].reshape(db, 1, dh), cache_idx_k[l].transpose(0, 2, 1), pad)
    bias = _select(scores, k_sel_s, past_len, pad)
    caches = (cache_diff_k[l].transpose(0, 2, 3, 4, 1).reshape(n_pool, w_diff, page),
              cache_diff_v[l].reshape(n_pool, page * h_diff, 2 * dh),
              cache_dsa_k[l].transpose(0, 2, 3, 1).reshape(n_pool, w_dsa, page),
              cache_dsa_v[l].transpose(0, 2, 3, 1).reshape(n_pool, w_dsa, page))
    od, os_ = _decode_attn(page_table, lam_vecs, g_sub,
                           o["qd"].reshape(db, 2 * h_diff, dh), o["kd_b"].reshape(db, 2 * h_diff, dh),
                           o["vd_b"].reshape(db, 1, w_diff),
                           o["qs"].reshape(db, h_dsa, dh), o["ks_b"].reshape(db, h_dsa, dh),
                           o["vs_b"].reshape(db, 1, w_dsa),
                           bias.reshape(db, 1, bias.shape[1]), caches, lam_init)
    x1, h2, comb, _, _ = _outproj(x_s, od, os_, o["gates"], prm, db, sample=True)
    y_s = _moe(x1, h2, comb, *moe_w, db)
    outs_s = tuple(a.reshape((db, dec_seq) + a.shape[2:]) for a in _key_outputs(o, 1, db, h_diff, h_dsa, dh))

    return ((y_p.reshape(bsz, seq, d_model), y_s.reshape(db, dec_seq, d_model))
            + tuple(a[None] for a in outs_p) + tuple(a[None] for a in outs_s))
```

```python
import functools
import math

import jax
import jax.numpy as jnp
from jax import lax
from jax.experimental import pallas as pl
from jax.experimental.pallas import tpu as pltpu

F32 = jnp.float32
BF16 = jnp.bfloat16
I32 = jnp.int32

EPS = 1e-6
ROPE_THETA = 10000.0
TOPK_MAX = 256
LANES = 128
HEAD_DIM = 64
MASK_NEG = -1e30
INT_MIN = -(2 ** 31)
KEY_NEG_INF = -2139095041
VMEM_LIMIT = 56 * 1024 * 1024
ROW_TILE = 256
KEY_CHUNK = 512
DSA_Q_TILE = 128
MOE_ROW_TILE = 256
LOG2E = 1.4426950408889634


def _params(*sem, **kw):
    return pltpu.CompilerParams(dimension_semantics=sem, vmem_limit_bytes=VMEM_LIMIT, **kw)


def _dot(a, b):
    return jnp.dot(a, b, preferred_element_type=F32)


def _dot_nt(a, b):
    return lax.dot_general(a, b, (((1,), (1,)), ((), ())), preferred_element_type=F32)


def _split_bf16(a):
    hi = a.astype(BF16)
    lo = (a - hi.astype(F32)).astype(BF16)
    return hi, lo


_INPROJ_OUTS = ("qd", "kdT_f", "kdT_b", "vd_f", "vd_b", "qs", "ksT_f", "ksT_b", "vsT_f", "vs_b",
                "qi", "kiT_f", "ki2T_b", "wi", "gates")
_INPROJ_NATURAL = ("kd_b", "ks_b", "ki_b")


def _mm(a, w_ref, wlo_ref, cols=slice(None)):
    if wlo_ref is None:
        return _dot(a.astype(BF16), w_ref[:, cols])
    hi, lo = _split_bf16(a)
    return _dot(hi, w_ref[:, cols]) + (_dot(hi, wlo_ref[:, cols]) + _dot(lo, w_ref[:, cols]))


def _dot3(a, b, nt=False):
    dot = _dot_nt if nt else _dot
    a_hi, a_lo = _split_bf16(a)
    b_hi, b_lo = _split_bf16(b)
    return dot(a_hi, b_hi) + (dot(a_hi, b_lo) + dot(a_lo, b_hi))


def _inproj_kernel(x_ref, cos_ref, sin_ref, gmix_ref, wqkv_ref, wsm_ref, wgate_ref,
                   gqd_ref, gkd_ref, gqs_ref, gks_ref, bd_ref, *refs,
                   names, precise, w_diff, w_dsa, w_idx, h_idx, qk_scale):
    wqkv_lo, wsm_lo, wgate_lo = refs[:3] if precise else (None, None, None)
    o = dict(zip(names, refs[3:] if precise else refs))
    act = F32 if precise else BF16
    x = x_ref[...]
    tm = x.shape[0]
    ms = jnp.mean(x * x, axis=-1, keepdims=True)
    h = (x * lax.rsqrt(ms + EPS)) * gmix_ref[...]
    if not precise:
        h = h.astype(BF16)

    cos = cos_ref[...]
    sin = sin_ref[...]
    lane = lax.broadcasted_iota(I32, (tm, LANES), 1)
    low_half = (lane & (HEAD_DIM // 2)) == 0

    def rope(y):
        partner = jnp.where(low_half, pltpu.roll(y, LANES - HEAD_DIM // 2, 1), pltpu.roll(y, HEAD_DIM // 2, 1))
        return y * cos + partner * sin

    bd = bd_ref[...]

    def head_norm(p, g_ref):
        hi, lo = _split_bf16(p * p)
        width = bd.shape[0]
        parts = []
        for c in range(0, p.shape[1], width):
            parts.append(_dot(hi[:, c:c + width], bd) + _dot(lo[:, c:c + width], bd))
        ss = jnp.concatenate(parts, axis=1) if len(parts) > 1 else parts[0]
        return (p * lax.rsqrt(ss * (1.0 / HEAD_DIM) + EPS)) * g_ref[...]

    def seg(c0, width):
        return _mm(h, wqkv_ref, wqkv_lo, slice(c0, c0 + width))

    def rope_all(y, scale=None):
        outs = []
        for c in range(0, y.shape[1], LANES):
            r = rope(y[:, c:c + LANES])
            outs.append(r if scale is None else r * scale)
        return jnp.concatenate(outs, axis=1) if len(outs) > 1 else outs[0]

    c0 = 0
    o["qd"][...] = rope_all(head_norm(seg(c0, w_diff), gqd_ref), qk_scale * LOG2E).astype(act)
    c0 += w_diff
    kd = rope_all(head_norm(seg(c0, w_diff), gkd_ref))
    kd_t = kd.T
    o["kdT_f"][...] = kd_t
    o["kdT_b"][...] = kd_t.astype(BF16)
    if "kd_b" in o:
        o["kd_b"][...] = kd.astype(act)
    c0 += w_diff
    vd = seg(c0, w_diff)
    o["vd_f"][...] = vd
    o["vd_b"][...] = vd.astype(act)
    c0 += w_diff
    o["qs"][...] = rope_all(head_norm(seg(c0, w_dsa), gqs_ref), qk_scale * LOG2E).astype(act)
    c0 += w_dsa
    ks = rope_all(head_norm(seg(c0, w_dsa), gks_ref))
    ks_t = ks.T
    o["ksT_f"][...] = ks_t
    o["ksT_b"][...] = ks_t.astype(BF16)
    if "ks_b" in o:
        o["ks_b"][...] = ks.astype(act)
    c0 += w_dsa
    vs = seg(c0, w_dsa)
    o["vsT_f"][...] = vs.T
    o["vs_b"][...] = vs.astype(act)
    c0 += w_dsa
    o["qi"][...] = rope_all(seg(c0, w_idx), qk_scale).astype(act)

    sm = _mm(h, wsm_ref, wsm_lo)
    ki = rope(sm)
    ki2 = jnp.where(lane < HEAD_DIM, ki, pltpu.roll(ki, HEAD_DIM, 1))
    ki2_t = ki2.T
    o["kiT_f"][...] = ki2_t[:HEAD_DIM]
    o["ki2T_b"][...] = ki2_t.astype(BF16)
    if "ki_b" in o:
        o["ki_b"][...] = ki2.astype(act)
    o["wi"][...] = sm * (h_idx ** -0.5)

    g = _mm(h, wgate_ref, wgate_lo)
    o["gates"][...] = (1.0 / (1.0 + jnp.exp(-g))).astype(act)


def _inproj(x, cos_t, sin_t, bsz, tm, kc, prm, sample):
    n, d = x.shape
    t = n // bsz
    tiles = t // tm
    per_chunk = kc // tm
    w_diff, w_dsa, w_idx, h_idx = prm["w_diff"], prm["w_dsa"], prm["w_idx"], prm["h_idx"]
    wg = prm["w_gate"].shape[1]
    act = F32 if sample else BF16
    row = lambda i: (i, 0)
    const = lambda i: (0, 0)
    pos = lambda i: (i % tiles, 0)
    full = lambda a: pl.BlockSpec(a.shape, const, pipeline_mode=pl.Buffered(1))
    weights = [prm[k] for k in ("g_mix", "w_qkv", "w_small", "w_gate", "g_qd", "g_kd", "g_qs", "g_ks", "bd")]
    if sample:
        weights += [prm["w_qkv_lo"], prm["w_small_lo"], prm["w_gate_lo"]]
    in_specs = ([pl.BlockSpec((tm, d), row), pl.BlockSpec((tm, LANES), pos), pl.BlockSpec((tm, LANES), pos)]
                + [full(w) for w in weights])

    def nat(w, dt):
        return jax.ShapeDtypeStruct((n, w), dt), pl.BlockSpec((tm, w), row)

    def tr_f(w):
        return (jax.ShapeDtypeStruct((bsz, w, t), F32),
                pl.BlockSpec((None, w, tm), lambda i: (i // tiles, 0, i % tiles)))

    def tr_b(w):
        return (jax.ShapeDtypeStruct((bsz, t // kc, w, kc), BF16),
                pl.BlockSpec((None, None, w, tm),
                             lambda i: (i // tiles, (i % tiles) // per_chunk, 0, (i % tiles) % per_chunk)))

    table = dict(qd=nat(w_diff, act), kdT_f=tr_f(w_diff), kdT_b=tr_b(w_diff), vd_f=nat(w_diff, F32),
                 vd_b=nat(w_diff, act), qs=nat(w_dsa, act), ksT_f=tr_f(w_dsa), ksT_b=tr_b(w_dsa),
                 vsT_f=tr_f(w_dsa), vs_b=nat(w_dsa, act), qi=nat(w_idx, act), kiT_f=tr_f(HEAD_DIM),
                 ki2T_b=tr_b(LANES), wi=nat(LANES, F32), gates=nat(wg, act),
                 kd_b=nat(w_diff, act), ks_b=nat(w_dsa, act), ki_b=nat(LANES, act))
    names = _INPROJ_OUTS + (_INPROJ_NATURAL if sample else ())
    kern = functools.partial(_inproj_kernel, names=names, precise=sample, w_diff=w_diff, w_dsa=w_dsa, w_idx=w_idx,
                             h_idx=h_idx, qk_scale=HEAD_DIM ** -0.5)
    outs = pl.pallas_call(kern, grid=(n // tm,), in_specs=in_specs,
                          out_specs=[table[k][1] for k in names], out_shape=[table[k][0] for k in names],
                          compiler_params=_params("parallel"), name="inproj")(x, cos_t, sin_t, *weights)
    return dict(zip(names, outs))


def _stack_halves(q):
    lane = lax.broadcasted_iota(I32, q.shape, 1)
    zero = jnp.zeros_like(q)
    return jnp.concatenate([jnp.where(lane < HEAD_DIM, q, zero), jnp.where(lane >= HEAD_DIM, q, zero)], axis=0)


def _lane_fold(a):
    out = a[:, :LANES]
    for c in range(LANES, a.shape[1], LANES):
        out = out + a[:, c:c + LANES]
    return out


def _online_update(s_ref, v, m_sc, l_sc, acc_sc, fin=None):
    s = s_ref[...]
    if fin is not None:
        s = fin(s, 0)
    m_prev = m_sc[...]
    m_new = jnp.maximum(m_prev, jnp.max(s, axis=-1, keepdims=True))
    alpha = jnp.exp2(m_prev - m_new)
    p = jnp.exp2(s - jnp.tile(m_new, (1, s.shape[1] // LANES)))
    l_sc[...] = alpha * l_sc[...] + _lane_fold(p)
    acc_sc[...] = alpha * acc_sc[...] + _dot(p.astype(BF16), v)
    m_sc[...] = m_new


def _finish(l_sc, acc_sc):
    return acc_sc[...] * (1.0 / jnp.sum(l_sc[...], axis=-1, keepdims=True))


def _attend_chunks(n, score, value, s_a, s_b, m_sc, l_sc, acc_sc, last=None):
    s_a[...] = score(0)

    def pair(p, carry):
        j = 2 * p
        s_b[...] = score(j + 1)
        _online_update(s_a, value(j), m_sc, l_sc, acc_sc)
        s_a[...] = score(j + 2)
        _online_update(s_b, value(j + 1), m_sc, l_sc, acc_sc)
        return carry

    lax.fori_loop(0, (n - 1) // 2, pair, 0)

    @pl.when(n % 2 == 1)
    def _():
        _online_update(s_a, value(n - 1), m_sc, l_sc, acc_sc, last)

    @pl.when(n % 2 == 0)
    def _():
        s_b[...] = score(n - 1)
        _online_update(s_a, value(n - 2), m_sc, l_sc, acc_sc)
        _online_update(s_b, value(n - 1), m_sc, l_sc, acc_sc, last)


def _lambda(lam_ref, lam_init):
    a = jnp.sum(lam_ref[0:1, :] * lam_ref[1:2, :], axis=-1, keepdims=True)
    b = jnp.sum(lam_ref[2:3, :] * lam_ref[3:4, :], axis=-1, keepdims=True)
    return jnp.exp(a) - jnp.exp(b) + lam_init


def _attn_scratch(rows, kc):
    return [pltpu.VMEM((rows, LANES), F32)] * 3 + [pltpu.VMEM((rows, kc), F32)] * 2


def _diff_attn_kernel(lam_ref, gsub_ref, q_ref, kt_ref, v_ref, o_ref, m_sc, l_sc, acc_sc, s_a, s_b, *,
                      tq, kc, lam_init):
    q0 = pl.program_id(2) * tq
    n_c = q0 // kc + 1
    q2 = _stack_halves(q_ref[...])
    m_sc[...] = jnp.full(m_sc.shape, -jnp.inf, F32)
    l_sc[...] = jnp.zeros(l_sc.shape, F32)
    acc_sc[...] = jnp.zeros(acc_sc.shape, F32)

    def causal(s, row0):
        r = lax.broadcasted_iota(I32, s.shape, 0) + row0
        r = jnp.where(r >= tq, r - tq, r) + q0
        c = lax.broadcasted_iota(I32, s.shape, 1) + (n_c - 1) * kc
        return jnp.where(c <= r, s, -jnp.inf)

    _attend_chunks(n_c, lambda j: _dot(q2, kt_ref[j]),
                   lambda j: v_ref[pl.ds(pl.multiple_of(j * kc, kc), kc), :],
                   s_a, s_b, m_sc, l_sc, acc_sc, last=causal)

    a = _finish(l_sc, acc_sc)
    o = a[:tq] - _lambda(lam_ref, lam_init) * a[tq:]
    ms = jnp.mean(o * o, axis=-1, keepdims=True)
    on = ((o * lax.rsqrt(ms + EPS)) * gsub_ref[...]) * (1.0 - lam_init)
    o_ref[...] = on.astype(BF16)


def _diff_attn(q, kt, v, lam_vecs, g_sub, lam_init, tq):
    b, t, w = q.shape
    nc, kc = kt.shape[1], kt.shape[3]
    assert kc % tq == 0
    grid = (b, w // LANES, t // tq)
    qmap = lambda bi, hi, qi: (bi, qi, hi)
    const = lambda bi, hi, qi: (0, 0)
    kern = functools.partial(_diff_attn_kernel, tq=tq, kc=kc, lam_init=lam_init)
    return pl.pallas_call(
        kern, grid=grid,
        in_specs=[pl.BlockSpec(lam_vecs.shape, const), pl.BlockSpec(g_sub.shape, const),
                  pl.BlockSpec((None, tq, LANES), qmap),
                  pl.BlockSpec((None, nc, LANES, kc), lambda bi, hi, qi: (bi, 0, hi, 0)),
                  pl.BlockSpec((None, t, LANES), lambda bi, hi, qi: (bi, 0, hi))],
        out_specs=pl.BlockSpec((None, tq, LANES), qmap),
        out_shape=jax.ShapeDtypeStruct((b, t, w), BF16),
        scratch_shapes=_attn_scratch(2 * tq, kc),
        compiler_params=_params("parallel", "parallel", "arbitrary"), name="diff_attn")(lam_vecs, g_sub, q, kt, v)


def _key_to_f32(k):
    b = k ^ (lax.shift_right_arithmetic(k, 31) & 0x7FFFFFFF)
    return lax.bitcast_convert_type(b, F32)


def _select_bias(score_sc, bias_sc, n_c, k_sel, row_pos, col0, t_sc=None):
    nc_max, r, kc = score_sc.shape

    def count(pred):
        def body(c, a):
            return a + _lane_fold(jnp.where(pred(score_sc[c]), 1.0, 0.0))
        a = lax.fori_loop(0, n_c, body, jnp.zeros((r, LANES), F32))
        return jnp.sum(a, axis=-1, keepdims=True)

    def search(k):
        def bit_body(i, t):
            cand = t + lax.shift_left(jnp.int32(1), 31 - i)
            cand_f = _key_to_f32(cand)
            a = jnp.zeros((r, LANES), F32)
            for c in range(k):
                a = a + _lane_fold(jnp.where(score_sc[c] >= cand_f, 1.0, 0.0))
            cnt = jnp.sum(a, axis=-1, keepdims=True)
            return jnp.where(cnt >= k_sel, cand, t)
        return lax.fori_loop(0, 32, bit_body, jnp.full((r, 1), INT_MIN, I32))

    if isinstance(n_c, int):
        t = search(n_c)
    else:
        for k in range(1, nc_max + 1):
            @pl.when(n_c == k)
            def _(k=k):
                t_sc[...] = jnp.broadcast_to(search(k), t_sc.shape)
        t = t_sc[:, :1]
    thr = _key_to_f32(jnp.maximum(t, KEY_NEG_INF))
    need = k_sel - count(lambda sc: sc > thr)

    ri = lax.broadcasted_iota(I32, (kc, kc), 0)
    ci = lax.broadcasted_iota(I32, (kc, kc), 1)
    tri = jnp.where(ri <= ci, 1.0, 0.0).astype(BF16)
    col = lax.broadcasted_iota(I32, (r, kc), 1)

    def bias_body(c, run):
        sc = score_sc[c]
        eq = sc == thr
        pre = _dot(jnp.where(eq, 1.0, 0.0).astype(BF16), tri)
        sel = (sc > thr) | (eq & ((pre + run) <= need))
        causal = (col + (col0 + c * kc)) <= row_pos
        bias_sc[c] = jnp.where(sel & causal, 0.0, MASK_NEG)
        return run + pre[:, kc - 1:kc]

    lax.fori_loop(0, n_c, bias_body, jnp.zeros((r, 1), F32))


def _dsa_attn_kernel(qi_ref, ki2t_ref, wi_ref, qs_ref, kst_ref, vs_ref, o_ref,
                     qstack_sc, score_sc, bias_sc, t_sc, q2_sc, m_sc, l_sc, acc_sc, *, tq, kc, k_sel, h_idx):
    qb = pl.program_id(1)
    q0 = qb * tq
    n_c = q0 // kc + 1
    row_pos = q0 + lax.broadcasted_iota(I32, (tq, 1), 0)

    for hh in range(h_idx):
        pair = qi_ref[:, (hh // 2) * LANES:(hh // 2 + 1) * LANES]
        lane = lax.broadcasted_iota(I32, pair.shape, 1)
        keep = (lane < HEAD_DIM) if hh % 2 == 0 else (lane >= HEAD_DIM)
        qstack_sc[hh * tq:(hh + 1) * tq, :] = jnp.where(keep, pair, jnp.zeros_like(pair))
    wi = wi_ref[...]

    def score_body(c, carry):
        d = _dot(qstack_sc[...], ki2t_ref[c])
        acc = jnp.zeros((tq, kc), F32)
        for hh in range(h_idx):
            acc = acc + jnp.maximum(d[hh * tq:(hh + 1) * tq], 0.0) * wi[:, HEAD_DIM + hh:HEAD_DIM + hh + 1]
        col = c * kc + lax.broadcasted_iota(I32, (tq, kc), 1)
        score_sc[c] = jnp.where(col <= row_pos, acc, -jnp.inf)
        return carry

    lax.fori_loop(0, n_c, score_body, 0)
    _select_bias(score_sc, bias_sc, n_c, k_sel, row_pos, 0, t_sc)

    n_hp = qs_ref.shape[1] // LANES
    for hp in range(n_hp):
        q2_sc[hp] = _stack_halves(qs_ref[:, hp * LANES:(hp + 1) * LANES])
    m_sc[...] = jnp.full(m_sc.shape, MASK_NEG, F32)
    l_sc[...] = jnp.zeros(l_sc.shape, F32)
    acc_sc[...] = jnp.zeros(acc_sc.shape, F32)

    def attn_body(c, carry):
        bias = bias_sc[c]
        bias2 = jnp.concatenate([bias, bias], axis=0)
        rows = pl.ds(pl.multiple_of(c * kc, kc), kc)
        for hp in range(n_hp):
            cols = slice(hp * LANES, (hp + 1) * LANES)
            s = _dot(q2_sc[hp], kst_ref[c, cols, :]) + bias2
            _online_update(s, vs_ref[rows, cols], m_sc.at[hp], l_sc.at[hp], acc_sc.at[hp])
        return carry

    lax.fori_loop(0, n_c, attn_body, 0)
    lane = lax.broadcasted_iota(I32, (tq, LANES), 1)
    for hp in range(n_hp):
        a = _finish(l_sc.at[hp], acc_sc.at[hp])
        o_ref[:, hp * LANES:(hp + 1) * LANES] = jnp.where(lane < HEAD_DIM, a[:tq], a[tq:]).astype(BF16)


def _dsa_attn(qi, ki2t, wi, qs, kst, vs, k_sel, h_idx, tq):
    b, t, w = qs.shape
    nc, kc = kst.shape[1], kst.shape[3]
    assert kc % tq == 0 and kc >= k_sel
    qmap = lambda bi, qb: (bi, qb, 0)
    kern = functools.partial(_dsa_attn_kernel, tq=tq, kc=kc, k_sel=k_sel, h_idx=h_idx)
    return pl.pallas_call(
        kern, grid=(b, t // tq),
        in_specs=[pl.BlockSpec((None, tq, qi.shape[2]), qmap),
                  pl.BlockSpec((None, nc, LANES, kc), lambda bi, qb: (bi, 0, 0, 0)),
                  pl.BlockSpec((None, tq, LANES), qmap), pl.BlockSpec((None, tq, w), qmap),
                  pl.BlockSpec((None, nc, w, kc), lambda bi, qb: (bi, 0, 0, 0)),
                  pl.BlockSpec((None, t, w), lambda bi, qb: (bi, 0, 0))],
        out_specs=pl.BlockSpec((None, tq, w), qmap),
        out_shape=jax.ShapeDtypeStruct((b, t, w), BF16),
        scratch_shapes=[pltpu.VMEM((h_idx * tq, LANES), BF16), pltpu.VMEM((nc, tq, kc), F32),
                        pltpu.VMEM((nc, tq, kc), F32), pltpu.VMEM((tq, LANES), I32),
                        pltpu.VMEM((w // LANES, 2 * tq, LANES), BF16)]
                       + [pltpu.VMEM((w // LANES, 2 * tq, LANES), F32)] * 3,
        compiler_params=_params("parallel", "arbitrary"), name="dsa_attn")(qi, ki2t, wi, qs, kst, vs)


def _page_specs(n_pages, block):
    def spec(j):
        return pl.BlockSpec((None,) + block, lambda b, pt, j=j: (pt[b, j],) + (0,) * len(block))
    return [spec(j) for j in range(n_pages)]


def _idx_score_kernel(pt_ref, q_ref, w_ref, knew_ref, *refs, n_pages, page, pad):
    page_refs, o_ref = refs[:n_pages], refs[n_pages]
    q = q_ref[...]
    w = w_ref[...]
    for j in range(n_pages):
        d = _dot3(q, page_refs[j][...])
        o_ref[:, j * page:(j + 1) * page] = jnp.sum(jnp.maximum(d, 0.0) * w, axis=0, keepdims=True)
    d_new = jnp.sum(q * knew_ref[...], axis=-1, keepdims=True)
    s_new = jnp.sum(jnp.maximum(d_new, 0.0) * w, axis=0, keepdims=True)
    lane = lax.broadcasted_iota(I32, (1, pad), 1)
    o_ref[:, n_pages * page:] = jnp.where(lane == 0, s_new, -jnp.inf)


def _idx_scores(page_table, q_i, w_i, k_new, cache_t, pad):
    db, n_pages = page_table.shape
    _, dh, page = cache_t.shape
    h_idx = q_i.shape[1]
    tk = n_pages * page + pad
    seq = lambda b, pt: (b, 0, 0)
    kern = functools.partial(_idx_score_kernel, n_pages=n_pages, page=page, pad=pad)
    grid_spec = pltpu.PrefetchScalarGridSpec(
        num_scalar_prefetch=1, grid=(db,),
        in_specs=[pl.BlockSpec((None, h_idx, dh), seq), pl.BlockSpec((None, h_idx, 1), seq),
                  pl.BlockSpec((None, 1, dh), seq)] + _page_specs(n_pages, (dh, page)),
        out_specs=pl.BlockSpec((None, 1, tk), seq))
    out = pl.pallas_call(kern, grid_spec=grid_spec, out_shape=jax.ShapeDtypeStruct((db, 1, tk), F32),
                         compiler_params=_params("parallel"), name="idx_scores")(
        page_table, q_i, w_i, k_new, *([cache_t] * n_pages))
    return out.reshape(db, tk)


def _select_kernel(s_ref, o_ref, score_sc, bias_sc, *, kc, k_sel, q_pos):
    r = s_ref.shape[0]
    nc = s_ref.shape[1] // kc
    for c in range(nc):
        score_sc[c] = s_ref[:, c * kc:(c + 1) * kc]
    row_pos = jnp.full((r, 1), q_pos, I32)
    _select_bias(score_sc, bias_sc, nc, k_sel, row_pos, 0)
    for c in range(nc):
        o_ref[:, c * kc:(c + 1) * kc] = bias_sc[c]


def _select(scores, k_sel, q_pos, kc):
    r, tk = scores.shape
    nc = tk // kc
    kern = functools.partial(_select_kernel, kc=kc, k_sel=k_sel, q_pos=q_pos)
    return pl.pallas_call(
        kern, grid=(1,), in_specs=[pl.BlockSpec((r, tk), lambda i: (0, 0))],
        out_specs=pl.BlockSpec((r, tk), lambda i: (0, 0)), out_shape=jax.ShapeDtypeStruct((r, tk), F32),
        scratch_shapes=[pltpu.VMEM((nc, r, kc), F32), pltpu.VMEM((nc, r, kc), F32)],
        compiler_params=_params("arbitrary"), name="select")(scores)


def _block_diag_rows(q, width):
    t = jnp.concatenate([q] * (width // HEAD_DIM), axis=1)
    row = lax.broadcasted_iota(I32, t.shape, 0)
    col = lax.broadcasted_iota(I32, t.shape, 1)
    return jnp.where((col >= row * HEAD_DIM) & (col < (row + 1) * HEAD_DIM), t, jnp.zeros_like(t))


def _decode_attn_kernel(pt_ref, lam_ref, gsub_ref, qd_ref, kdn_ref, vdn_ref, qs_ref, ksn_ref, vsn_ref, bias_ref,
                        *refs, n_pages, page, lam_init):
    kd_pages = refs[:n_pages]
    vd_pages = refs[n_pages:2 * n_pages]
    ks_pages = refs[2 * n_pages:3 * n_pages]
    vs_pages = refs[3 * n_pages:4 * n_pages]
    od_ref, os_ref = refs[4 * n_pages], refs[4 * n_pages + 1]
    h_diff = vd_pages[0].shape[0] // page

    def probs(q_ref, knew_ref, k_pages, bias):
        q = q_ref[...]
        qbd = _block_diag_rows(q, k_pages[0].shape[0])
        s = [_dot3(qbd, k_pages[j][...]) for j in range(n_pages)]
        s_new = jnp.sum(q * knew_ref[...], axis=-1, keepdims=True)
        if bias is not None:
            s = [s[j] + bias[:, j * page:(j + 1) * page] for j in range(n_pages)]
            s_new = s_new + bias[:, n_pages * page:n_pages * page + 1]
        m = s_new
        for j in range(n_pages):
            m = jnp.maximum(m, jnp.max(s[j], axis=-1, keepdims=True))
        p_new = jnp.exp2(s_new - m)
        p = [jnp.exp2(s[j] - m) for j in range(n_pages)]
        l = p_new
        for j in range(n_pages):
            l = l + jnp.sum(p[j], axis=-1, keepdims=True)
        return p, p_new, 1.0 / l

    p, p_new, inv_l = probs(qd_ref, kdn_ref, kd_pages, None)
    acc = p_new * vdn_ref[...]
    for j in range(n_pages):
        v = jnp.concatenate([vd_pages[j][pl.ds(hh, page, stride=h_diff), :] for hh in range(h_diff)], axis=1)
        acc = acc + _dot3(p[j], v)
    a = acc * inv_l
    row = lax.broadcasted_iota(I32, a.shape, 0)
    col = lax.broadcasted_iota(I32, a.shape, 1)
    own = (col // LANES) == (row // 2)
    coef = jnp.where(row % 2 == 0, 1.0, -_lambda(lam_ref, lam_init))
    o = jnp.sum(jnp.where(own, a * coef, 0.0), axis=0, keepdims=True)
    parts = []
    for c in range(0, o.shape[1], LANES):
        oc = o[:, c:c + LANES]
        ms = jnp.mean(oc * oc, axis=-1, keepdims=True)
        parts.append(((oc * lax.rsqrt(ms + EPS)) * gsub_ref[...]) * (1.0 - lam_init))
    od_ref[...] = jnp.concatenate(parts, axis=1)

    p, p_new, inv_l = probs(qs_ref, ksn_ref, ks_pages, bias_ref[...])
    acc = p_new * vsn_ref[...]
    for j in range(n_pages):
        acc = acc + _dot3(p[j], vs_pages[j][...], nt=True)
    a = acc * inv_l
    row = lax.broadcasted_iota(I32, a.shape, 0)
    col = lax.broadcasted_iota(I32, a.shape, 1)
    os_ref[...] = jnp.sum(jnp.where((col // HEAD_DIM) == row, a, 0.0), axis=0, keepdims=True)


def _decode_attn(page_table, lam_vecs, g_sub, qd, kdn, vdn, qs, ksn, vsn, bias, caches, lam_init):
    db, n_pages = page_table.shape
    c_kd, c_vd, c_ks, c_vs = caches
    w_diff, page = c_kd.shape[1:]
    w_dsa = c_ks.shape[1]
    seq = lambda b, pt: (b, 0, 0)
    const = lambda b, pt: (0, 0)
    in_specs = [pl.BlockSpec(lam_vecs.shape, const), pl.BlockSpec(g_sub.shape, const),
                pl.BlockSpec((None,) + qd.shape[1:], seq), pl.BlockSpec((None,) + kdn.shape[1:], seq),
                pl.BlockSpec((None,) + vdn.shape[1:], seq),
                pl.BlockSpec((None,) + qs.shape[1:], seq), pl.BlockSpec((None,) + ksn.shape[1:], seq),
                pl.BlockSpec((None,) + vsn.shape[1:], seq), pl.BlockSpec((None,) + bias.shape[1:], seq)]
    for cache in caches:
        in_specs += _page_specs(n_pages, tuple(cache.shape[1:]))
    kern = functools.partial(_decode_attn_kernel, n_pages=n_pages, page=page, lam_init=lam_init)
    grid_spec = pltpu.PrefetchScalarGridSpec(
        num_scalar_prefetch=1, grid=(db,), in_specs=in_specs,
        out_specs=[pl.BlockSpec((None, 1, w_diff), seq), pl.BlockSpec((None, 1, w_dsa), seq)])
    pages = []
    for cache in caches:
        pages += [cache] * n_pages
    od, os_ = pl.pallas_call(
        kern, grid_spec=grid_spec,
        out_shape=[jax.ShapeDtypeStruct((db, 1, w_diff), F32), jax.ShapeDtypeStruct((db, 1, w_dsa), F32)],
        compiler_params=_params("parallel"), name="decode_attn")(
        page_table, lam_vecs, g_sub, qd, kdn, vdn, qs, ksn, vsn, bias, *pages)
    return od.reshape(db, w_diff), os_.reshape(db, w_dsa)


ROUTE_E1, ROUTE_E2, ROUTE_R1, ROUTE_R2, ROUTE_W1, ROUTE_W2 = range(6)


def _outproj_kernel(x_ref, od_ref, os_ref, gate_ref, wod_ref, wos_ref, wout_ref, gffn_ref, wrh_ref, wrl_ref, *refs,
                    precise, n_groups, n_experts):
    wod_lo, wos_lo, wout_lo = refs[:3] if precise else (None, None, None)
    x1_ref, h2_ref, comb_ref, route_ref, count_ref = refs[3:] if precise else refs
    d = x_ref.shape[1]

    @pl.when(pl.program_id(0) == 0)
    def _():
        count_ref[...] = jnp.zeros(count_ref.shape, F32)

    yd = _mm(od_ref[...], wod_ref, wod_lo)
    ys = _mm(os_ref[...], wos_ref, wos_lo)
    merged = gate_ref[:, :d].astype(F32) * yd + gate_ref[:, d:].astype(F32) * ys
    x1 = x_ref[...] + _mm(merged, wout_ref, wout_lo)
    x1_ref[...] = x1

    ms = jnp.mean(x1 * x1, axis=-1, keepdims=True)
    h2 = (x1 * lax.rsqrt(ms + EPS)) * gffn_ref[...]
    h2_ref[...] = h2

    hi, lo = _split_bf16(h2)
    lg = _dot(hi, wrh_ref[...]) + (_dot(hi, wrl_ref[...]) + _dot(lo, wrh_ref[...]))
    lane = lax.broadcasted_iota(I32, lg.shape, 1)
    lanef = lane.astype(F32)
    big = float(LANES)
    epg = n_experts // n_groups

    gmask = (lane >= n_experts) & (lane < n_experts + n_groups)
    gl = jnp.where(gmask, lg, -jnp.inf)
    gmax = jnp.max(gl, axis=-1, keepdims=True)
    gsel = jnp.min(jnp.where(gl == gmax, lanef, big), axis=-1, keepdims=True) - n_experts
    g_w = 1.0 / jnp.sum(jnp.exp(gl - gmax), axis=-1, keepdims=True)

    lo_lane = gsel * epg
    emask = (lanef >= lo_lane) & (lanef < lo_lane + epg)
    el = jnp.where(emask, lg, -jnp.inf)
    v1 = jnp.max(el, axis=-1, keepdims=True)
    i1 = jnp.min(jnp.where(el == v1, lanef, big), axis=-1, keepdims=True)
    el2 = jnp.where(lanef == i1, -jnp.inf, el)
    v2 = jnp.max(el2, axis=-1, keepdims=True)
    i2 = jnp.min(jnp.where(el2 == v2, lanef, big), axis=-1, keepdims=True)
    e21 = jnp.exp(v2 - v1)
    den = 1.0 / (1.0 + e21)
    w1 = den * g_w
    w2 = (e21 * den) * g_w
    comb_ref[...] = jnp.where(lanef == i1, w1, 0.0) + jnp.where(lanef == i2, w2, 0.0)

    tm = lg.shape[0]
    onehot = jnp.where((lanef == i1) | (lanef == i2), 1.0, 0.0)
    ri = lax.broadcasted_iota(I32, (tm, tm), 0)
    ci = lax.broadcasted_iota(I32, (tm, tm), 1)
    before = jnp.where(ci < ri, 1.0, 0.0).astype(BF16)
    rank_e = count_ref[...] + _dot(before, onehot.astype(BF16))
    r1 = jnp.sum(jnp.where(lanef == i1, rank_e, 0.0), axis=-1, keepdims=True)
    r2 = jnp.sum(jnp.where(lanef == i2, rank_e, 0.0), axis=-1, keepdims=True)
    count_ref[...] += jnp.sum(onehot, axis=0, keepdims=True)
    rec = jnp.zeros(lg.shape, F32)
    for slot, val in ((ROUTE_E1, i1), (ROUTE_E2, i2), (ROUTE_W1, w1), (ROUTE_W2, w2), (ROUTE_R1, r1), (ROUTE_R2, r2)):
        rec = jnp.where(lane == slot, val, rec)
    route_ref[...] = rec


def _outproj(x, od, os_, gates, prm, tm, sample):
    n, d = x.shape
    row = lambda i: (i, 0)
    const = lambda i: (0, 0)
    full = lambda a: pl.BlockSpec(a.shape, const)
    weights = [prm[k] for k in ("w_od", "w_os", "w_out", "g_ffn", "w_rt_hi", "w_rt_lo")]
    if sample:
        weights += [prm["w_od_lo"], prm["w_os_lo"], prm["w_out_lo"]]
    kern = functools.partial(_outproj_kernel, precise=sample, n_groups=prm["n_groups"], n_experts=prm["n_experts"])
    return pl.pallas_call(
        kern, grid=(n // tm,),
        in_specs=[pl.BlockSpec((tm, d), row), pl.BlockSpec((tm, od.shape[1]), row), pl.BlockSpec((tm, os_.shape[1]), row),
                  pl.BlockSpec((tm, gates.shape[1]), row)] + [full(w) for w in weights],
        out_specs=[pl.BlockSpec((tm, d), row), pl.BlockSpec((tm, d), row), pl.BlockSpec((tm, LANES), row),
                   pl.BlockSpec((tm, LANES), row), pl.BlockSpec((1, LANES), const)],
        out_shape=[jax.ShapeDtypeStruct((n, d), F32), jax.ShapeDtypeStruct((n, d), F32),
                   jax.ShapeDtypeStruct((n, LANES), F32), jax.ShapeDtypeStruct((n, LANES), F32),
                   jax.ShapeDtypeStruct((1, LANES), F32)],
        compiler_params=_params("arbitrary"), name="outproj")(x, od, os_, gates, *weights)


def _expert_ffn(h, wg_ref, wu_ref, wd_ref):
    h = h.astype(BF16)
    a = _dot(h, wg_ref[...].astype(BF16))
    u = _dot(h, wu_ref[...].astype(BF16))
    act = (a * (1.0 / (1.0 + jnp.exp(-a)))) * u
    return _dot(act.astype(BF16), wd_ref[...].astype(BF16))


def _moe_kernel(x1_ref, h2_ref, comb_ref, wg_ref, wu_ref, wd_ref, y_ref, acc_sc):
    e = pl.program_id(1)

    @pl.when(e == 0)
    def _():
        acc_sc[...] = jnp.zeros(acc_sc.shape, F32)

    o = _expert_ffn(h2_ref[...], wg_ref, wu_ref, wd_ref)
    comb = comb_ref[...]
    lane = lax.broadcasted_iota(I32, comb.shape, 1)
    c = jnp.sum(jnp.where(lane == e, comb, 0.0), axis=-1, keepdims=True)
    acc_sc[...] += o * c

    @pl.when(e == pl.num_programs(1) - 1)
    def _():
        y_ref[...] = x1_ref[...] + acc_sc[...]


def _moe(x1, h2, comb, w_gate, w_up, w_down, tm):
    n, d = x1.shape
    ne, _, de = w_gate.shape
    row = lambda i, e: (i, 0)
    return pl.pallas_call(
        _moe_kernel, grid=(n // tm, ne),
        in_specs=[pl.BlockSpec((tm, d), row), pl.BlockSpec((tm, d), row), pl.BlockSpec((tm, LANES), row),
                  pl.BlockSpec((None, d, de), lambda i, e: (e, 0, 0)), pl.BlockSpec((None, d, de), lambda i, e: (e, 0, 0)),
                  pl.BlockSpec((None, de, d), lambda i, e: (e, 0, 0))],
        out_specs=pl.BlockSpec((tm, d), row), out_shape=jax.ShapeDtypeStruct((n, d), F32),
        scratch_shapes=[pltpu.VMEM((tm, d), F32)],
        compiler_params=_params("parallel", "arbitrary"), name="moe")(x1, h2, comb, w_gate, w_up, w_down)


def _row_copy(src, src_row, dst, dst_row, sem):
    return pltpu.make_async_copy(src.at[pl.ds(src_row, 1), :], dst.at[pl.ds(dst_row, 1), :], sem)


def _slot_rows(assign_ref, offs_ref, n, tok):
    return (offs_ref[assign_ref[tok]] + assign_ref[2 * n + tok],
            offs_ref[assign_ref[n + tok]] + assign_ref[3 * n + tok])


def _scatter_kernel(assign_ref, offs_ref, group_ref, n_tiles_ref, h2_ref, xs_hbm, zero_sc, sem, zero_sem, *,
                    n, tm, tr):
    base = pl.program_id(0) * tm
    n_experts = offs_ref.shape[0]
    total_tiles = xs_hbm.shape[0] // tr

    def zero_tile(row0):
        return pltpu.make_async_copy(zero_sc, xs_hbm.at[pl.ds(pl.multiple_of(row0, tr), tr), :], zero_sem)

    @pl.when(pl.program_id(0) == 0)
    def _():
        zero_sc[...] = jnp.zeros(zero_sc.shape, F32)
        for wait in (False, True):
            def group_tail(e, carry, wait=wait):
                @pl.when(group_ref[e] > 0)
                def _():
                    cp = zero_tile(offs_ref[e] + group_ref[e] - tr)
                    cp.wait() if wait else cp.start()
                return carry

            def unused(j, carry, wait=wait):
                cp = zero_tile(j * tr)
                cp.wait() if wait else cp.start()
                return carry

            lax.fori_loop(0, n_experts, group_tail, 0)
            lax.fori_loop(n_tiles_ref[0], total_tiles, unused, 0)

    def start(t, carry):
        p1, p2 = _slot_rows(assign_ref, offs_ref, n, base + t)
        _row_copy(h2_ref, t, xs_hbm, p1, sem).start()
        _row_copy(h2_ref, t, xs_hbm, p2, sem).start()
        return carry

    def wait(t, carry):
        p1, p2 = _slot_rows(assign_ref, offs_ref, n, base + t)
        _row_copy(h2_ref, t, xs_hbm, p1, sem).wait()
        _row_copy(h2_ref, t, xs_hbm, p2, sem).wait()
        return carry

    lax.fori_loop(0, tm, start, 0, unroll=8)
    lax.fori_loop(0, tm, wait, 0, unroll=8)


def _scatter_rows(assign, offs, group, n_tiles, h2, n_rows, tm, tr):
    n, d = h2.shape
    grid_spec = pltpu.PrefetchScalarGridSpec(
        num_scalar_prefetch=4, grid=(n // tm,),
        in_specs=[pl.BlockSpec((tm, d), lambda i, *_: (i, 0))], out_specs=pl.BlockSpec(memory_space=pl.ANY),
        scratch_shapes=[pltpu.VMEM((tr, d), F32), pltpu.SemaphoreType.DMA(()), pltpu.SemaphoreType.DMA(())])
    return pl.pallas_call(
        functools.partial(_scatter_kernel, n=n, tm=tm, tr=tr), grid_spec=grid_spec,
        out_shape=jax.ShapeDtypeStruct((n_rows, d), F32),
        compiler_params=_params("arbitrary", disable_bounds_checks=True), name="moe_scatter")(
        assign, offs, group, n_tiles, h2)


def _grouped_ffn_kernel(tile_expert_ref, n_tiles_ref, xs_ref, wg_ref, wu_ref, wd_ref, ys_ref):
    used = pl.program_id(0) < n_tiles_ref[0]

    @pl.when(used)
    def _():
        ys_ref[...] = _expert_ffn(xs_ref[...], wg_ref, wu_ref, wd_ref)

    @pl.when(jnp.logical_not(used))
    def _():
        ys_ref[...] = jnp.zeros(ys_ref.shape, F32)


def _grouped_ffn(tile_expert, n_tiles, xs, w_gate, w_up, w_down, tr):
    rows, d = xs.shape
    ne, _, de = w_gate.shape
    last = lambda i, te, nt: jnp.minimum(i, nt[0] - 1)
    xmap = lambda i, te, nt: (last(i, te, nt), 0)
    wmap = lambda i, te, nt: (te[last(i, te, nt)], 0, 0)
    grid_spec = pltpu.PrefetchScalarGridSpec(
        num_scalar_prefetch=2, grid=(rows // tr,),
        in_specs=[pl.BlockSpec((tr, d), xmap), pl.BlockSpec((None, d, de), wmap), pl.BlockSpec((None, d, de), wmap),
                  pl.BlockSpec((None, de, d), wmap)],
        out_specs=pl.BlockSpec((tr, d), lambda i, te, nt: (i, 0)))
    return pl.pallas_call(
        _grouped_ffn_kernel, grid_spec=grid_spec, out_shape=jax.ShapeDtypeStruct((rows, d), F32),
        compiler_params=_params("arbitrary"), name="moe_ffn")(tile_expert, n_tiles, xs, w_gate, w_up, w_down)


def _combine_kernel(assign_ref, offs_ref, x1_ref, route_ref, ys_hbm, y_ref, buf1, buf2, sem, *, n, tm):
    base = pl.program_id(0) * tm

    def start(t, carry):
        p1, p2 = _slot_rows(assign_ref, offs_ref, n, base + t)
        _row_copy(ys_hbm, p1, buf1, t, sem).start()
        _row_copy(ys_hbm, p2, buf2, t, sem).start()
        return carry

    def wait(t, carry):
        p1, p2 = _slot_rows(assign_ref, offs_ref, n, base + t)
        _row_copy(ys_hbm, p1, buf1, t, sem).wait()
        _row_copy(ys_hbm, p2, buf2, t, sem).wait()
        return carry

    lax.fori_loop(0, tm, start, 0, unroll=8)
    lax.fori_loop(0, tm, wait, 0, unroll=8)
    route = route_ref[...]
    w1 = route[:, ROUTE_W1:ROUTE_W1 + 1]
    w2 = route[:, ROUTE_W2:ROUTE_W2 + 1]
    y_ref[...] = (x1_ref[...] + buf1[...] * w1) + buf2[...] * w2


def _combine_rows(assign, offs, x1, route, ys, tm):
    n, d = x1.shape
    row = lambda i, *_: (i, 0)
    grid_spec = pltpu.PrefetchScalarGridSpec(
        num_scalar_prefetch=2, grid=(n // tm,),
        in_specs=[pl.BlockSpec((tm, d), row), pl.BlockSpec((tm, LANES), row), pl.BlockSpec(memory_space=pl.ANY)],
        out_specs=pl.BlockSpec((tm, d), row),
        scratch_shapes=[pltpu.VMEM((tm, d), F32), pltpu.VMEM((tm, d), F32), pltpu.SemaphoreType.DMA(())])
    return pl.pallas_call(
        functools.partial(_combine_kernel, n=n, tm=tm), grid_spec=grid_spec,
        out_shape=jax.ShapeDtypeStruct((n, d), F32),
        compiler_params=_params("arbitrary", disable_bounds_checks=True), name="moe_combine")(
        assign, offs, x1, route, ys)


def _routed_moe(x1, h2, route, counts, w_gate, w_up, w_down, n_experts, tm, tr):
    n, d = x1.shape
    cnt = counts[0, :n_experts].astype(I32)
    group = ((cnt + tr - 1) // tr) * tr
    ends = jnp.cumsum(group)
    offs = ends - group
    assign = route[:, ROUTE_E1:ROUTE_R2 + 1].astype(I32).T.reshape(-1)
    n_rows = 2 * n + n_experts * tr
    tile_start = jnp.arange(n_rows // tr, dtype=I32) * tr
    tile_expert = jnp.minimum(jnp.sum(tile_start[:, None] >= ends[None, :], axis=1), n_experts - 1).astype(I32)
    n_tiles = (ends[-1:] // tr).astype(I32)
    xs = _scatter_rows(assign, offs, group, n_tiles, h2, n_rows, tm, tr)
    ys = _grouped_ffn(tile_expert, n_tiles, xs, w_gate, w_up, w_down, tr)
    return _combine_rows(assign, offs, x1, route, ys, tm)


def _rope_tables(pos):
    half = HEAD_DIM // 2
    inv_freq = jnp.power(ROPE_THETA, -jnp.arange(half, dtype=F32) / half)
    ang = pos.astype(F32)[:, None] * inv_freq[None, :]
    cos, sin = jnp.cos(ang), jnp.sin(ang)
    reps = LANES // HEAD_DIM
    return jnp.tile(cos, (1, 2 * reps)), jnp.tile(jnp.concatenate([-sin, sin], axis=1), (1, reps))


def _tile_rows(n, target):
    t = min(n, target)
    while n % t:
        t //= 2
    return t


def _key_outputs(o, bsz, t, h_diff, h_dsa, dh):
    kd = o["kdT_f"].reshape(bsz, h_diff, 2, dh, t).transpose(0, 4, 1, 2, 3)
    vd = o["vd_f"].reshape(bsz, t, h_diff, 2 * dh)
    ks = o["ksT_f"].reshape(bsz, h_dsa, dh, t).transpose(0, 3, 1, 2)
    vs = o["vsT_f"].reshape(bsz, h_dsa, dh, t).transpose(0, 3, 1, 2)
    ki = o["kiT_f"].transpose(0, 2, 1)
    return kd, vd, ks, vs, ki


def kernel(x_prompt, x_sample, cache_diff_k, cache_diff_v, cache_dsa_k, cache_dsa_v, cache_idx_k, page_table, g_mix, w_in, g_q_diff, g_k_diff, lam_q1, lam_k1, lam_q2, lam_k2, g_subln, g_q_dsa, g_k_dsa, w_o_diff, w_o_dsa, w_out, g_ffn, w_group, w_router, w_gate, w_up, w_down):
    bsz, seq, d_model = x_prompt.shape
    db, dec_seq, _ = x_sample.shape
    depth, n_pool, page, h_diff, _, dh = cache_diff_k.shape
    h_dsa = cache_dsa_k.shape[3]
    n_pages = page_table.shape[1]
    past_len = n_pages * page
    n_groups = w_group.shape[2]
    n_experts = w_router.shape[2]
    w_diff = h_diff * 2 * dh
    w_dsa = h_dsa * dh
    h_idx = (w_in.shape[2] - (3 * w_diff + 3 * w_dsa + dh + 2 * d_model)) // (dh + 1)
    w_idx = h_idx * dh
    assert dh == HEAD_DIM and cache_dsa_k.shape[4] == HEAD_DIM and cache_idx_k.shape[3] == HEAD_DIM
    assert dec_seq == 1 and depth == 1
    assert 3 * w_diff + 3 * w_dsa + w_idx + dh + h_idx + 2 * d_model == w_in.shape[2]
    k_sel_p = min(TOPK_MAX, seq // 4)
    k_sel_s = min(TOPK_MAX, (past_len + dec_seq) // 4)

    l = 0
    lam_init = 0.8 - 0.6 * math.exp(-0.3 * l)
    n_qkv = 3 * w_diff + 3 * w_dsa + w_idx
    w_l = w_in[l]
    rt = jnp.concatenate([w_router[l], w_group[l]], axis=1)
    rt = jnp.pad(rt, ((0, 0), (0, LANES - rt.shape[1])))
    blk = lax.broadcasted_iota(I32, (2 * LANES, 2 * LANES), 0) // HEAD_DIM
    blk_c = lax.broadcasted_iota(I32, (2 * LANES, 2 * LANES), 1) // HEAD_DIM
    prm = dict(
        w_diff=w_diff, w_dsa=w_dsa, w_idx=w_idx, h_idx=h_idx, n_groups=n_groups, n_experts=n_experts,
        g_mix=g_mix[l][None, :],
        g_qd=jnp.tile(g_q_diff[l], w_diff // dh)[None, :], g_kd=jnp.tile(g_k_diff[l], w_diff // dh)[None, :],
        g_qs=jnp.tile(g_q_dsa[l], w_dsa // dh)[None, :], g_ks=jnp.tile(g_k_dsa[l], w_dsa // dh)[None, :],
        bd=(blk == blk_c).astype(BF16), g_ffn=g_ffn[l][None, :])
    for name, w in (("w_qkv", w_l[:, :n_qkv]),
                    ("w_small", jnp.pad(w_l[:, n_qkv:n_qkv + dh + h_idx], ((0, 0), (0, LANES - dh - h_idx)))),
                    ("w_gate", w_l[:, n_qkv + dh + h_idx:]),
                    ("w_od", w_o_diff[l]), ("w_os", w_o_dsa[l]), ("w_out", w_out[l]), ("w_rt", rt)):
        hi = lax.reduce_precision(w, exponent_bits=8, mantissa_bits=7)
        prm[name] = hi.astype(BF16)
        prm[name + "_lo"] = (w - hi).astype(BF16)
    prm["w_rt_hi"] = prm["w_rt"]
    lam_vecs = jnp.stack([lam_q1[l], lam_k1[l], lam_q2[l], lam_k2[l]])
    g_sub = g_subln[l][None, :]
    moe_w = (w_gate[l], w_up[l], w_down[l])

    n_p = bsz * seq
    tm = _tile_rows(seq, ROW_TILE)
    kc = _tile_rows(seq, KEY_CHUNK)
    cos_p, sin_p = _rope_tables(jnp.arange(seq, dtype=I32))
    x_p = x_prompt.reshape(n_p, d_model)
    o = _inproj(x_p, cos_p, sin_p, bsz, tm, kc, prm, sample=False)
    r3 = lambda a: a.reshape(bsz, seq, a.shape[1])
    od = _diff_attn(r3(o["qd"]), o["kdT_b"], r3(o["vd_b"]), lam_vecs, g_sub, lam_init, tm)
    os_ = _dsa_attn(r3(o["qi"]), o["ki2T_b"], r3(o["wi"]), r3(o["qs"]), o["ksT_b"], r3(o["vs_b"]),
                    k_sel_p, h_idx, _tile_rows(seq, DSA_Q_TILE))
    x1, h2, _, route, counts = _outproj(x_p, od.reshape(n_p, w_diff), os_.reshape(n_p, w_dsa), o["gates"], prm, tm,
                                        sample=False)
    y_p = _routed_moe(x1, h2, route, counts, *moe_w, n_experts, tm, MOE_ROW_TILE)
    outs_p = _key_outputs(o, bsz, seq, h_diff, h_dsa, dh)

    cos_s, sin_s = _rope_tables(jnp.full((db,), past_len, I32))
    x_s = x_sample.reshape(db, d_model)
    o = _inproj(x_s, cos_s, sin_s, 1, db, db, prm, sample=True)
    pad = 2 * LANES
    assert pad >= k_sel_s
    scores = _idx_scores(page_table, o["qi"].reshape(db, h_idx, dh), o["wi"][:, dh:dh + h_idx].reshape(db, h_idx, 1),
                         o["ki_b"][:, :dh].reshape(db, 1, dh), cache_idx_k[l].transpose(0, 2, 1), pad)
    bias = _select(scores, k_sel_s, past_len, pad)
    caches = (cache_diff_k[l].transpose(0, 2, 3, 4, 1).reshape(n_pool, w_diff, page),
              cache_diff_v[l].reshape(n_pool, page * h_diff, 2 * dh),
              cache_dsa_k[l].transpose(0, 2, 3, 1).reshape(n_pool, w_dsa, page),
              cache_dsa_v[l].transpose(0, 2, 3, 1).reshape(n_pool, w_dsa, page))
    od, os_ = _decode_attn(page_table, lam_vecs, g_sub,
                           o["qd"].reshape(db, 2 * h_diff, dh), o["kd_b"].reshape(db, 2 * h_diff, dh),
                           o["vd_b"].reshape(db, 1, w_diff),
                           o["qs"].reshape(db, h_dsa, dh), o["ks_b"].reshape(db, h_dsa, dh),
                           o["vs_b"].reshape(db, 1, w_dsa),
                           bias.reshape(db, 1, bias.shape[1]), caches, lam_init)
    x1, h2, comb, _, _ = _outproj(x_s, od, os_, o["gates"], prm, db, sample=True)
    y_s = _moe(x1, h2, comb, *moe_w, db)
    outs_s = tuple(a.reshape((db, dec_seq) + a.shape[2:]) for a in _key_outputs(o, 1, db, h_diff, h_dsa, dh))

    return ((y_p.reshape(bsz, seq, d_model), y_s.reshape(db, dec_seq, d_model))
            + tuple(a[None] for a in outs_p) + tuple(a[None] for a in outs_s))
```

```python
import functools
import math

import jax
import jax.numpy as jnp
from jax import lax
from jax.experimental import pallas as pl
from jax.experimental.pallas import tpu as pltpu

F32 = jnp.float32
BF16 = jnp.bfloat16
I32 = jnp.int32

EPS = 1e-6
ROPE_THETA = 10000.0
TOPK_MAX = 256
LANES = 128
HEAD_DIM = 64
MASK_NEG = -1e30
INT_MIN = -(2 ** 31)
KEY_NEG_INF = -2139095041
VMEM_LIMIT = 56 * 1024 * 1024
ROW_TILE = 256
KEY_CHUNK = 512
DSA_Q_TILE = 128
MOE_ROW_TILE = 256
LOG2E = 1.4426950408889634


def _params(*sem, **kw):
    return pltpu.CompilerParams(dimension_semantics=sem, vmem_limit_bytes=VMEM_LIMIT, **kw)


def _dot(a, b):
    return jnp.dot(a, b, preferred_element_type=F32)


def _dot_nt(a, b):
    return lax.dot_general(a, b, (((1,), (1,)), ((), ())), preferred_element_type=F32)


def _split_bf16(a):
    hi = a.astype(BF16)
    lo = (a - hi.astype(F32)).astype(BF16)
    return hi, lo


_INPROJ_OUTS = ("qd", "kdT_f", "kdT_b", "vd_f", "vd_b", "qs", "ksT_f", "ksT_b", "vsT_f", "vs_b",
                "qi", "kiT_f", "ki2T_b", "wi", "gates")
_INPROJ_NATURAL = ("kd_b", "ks_b", "ki_b")


def _mm(a, w_ref, precise, cols=slice(None)):
    w = w_ref[:, cols]
    return _dot3(a, w) if precise else _dot(a.astype(BF16), w)


def _dot3(a, b, nt=False):
    dot = _dot_nt if nt else _dot
    a_hi, a_lo = _split_bf16(a)
    b_hi, b_lo = _split_bf16(b)
    return dot(a_hi, b_hi) + (dot(a_hi, b_lo) + dot(a_lo, b_hi))


def _inproj_kernel(x_ref, cos_ref, sin_ref, gmix_ref, wqkv_ref, wsm_ref, wgate_ref,
                   gqd_ref, gkd_ref, gqs_ref, gks_ref, bd_ref, *out_refs,
                   names, precise, w_diff, w_dsa, w_idx, h_idx, qk_scale):
    o = dict(zip(names, out_refs))
    act = F32 if precise else BF16
    x = x_ref[...]
    tm = x.shape[0]
    ms = jnp.mean(x * x, axis=-1, keepdims=True)
    h = (x * lax.rsqrt(ms + EPS)) * gmix_ref[...]
    if not precise:
        h = h.astype(BF16)

    cos = cos_ref[...]
    sin = sin_ref[...]
    lane = lax.broadcasted_iota(I32, (tm, LANES), 1)
    low_half = (lane & (HEAD_DIM // 2)) == 0

    def rope(y):
        partner = jnp.where(low_half, pltpu.roll(y, LANES - HEAD_DIM // 2, 1), pltpu.roll(y, HEAD_DIM // 2, 1))
        return y * cos + partner * sin

    bd = bd_ref[...]

    def head_norm(p, g_ref):
        hi, lo = _split_bf16(p * p)
        width = bd.shape[0]
        parts = []
        for c in range(0, p.shape[1], width):
            parts.append(_dot(hi[:, c:c + width], bd) + _dot(lo[:, c:c + width], bd))
        ss = jnp.concatenate(parts, axis=1) if len(parts) > 1 else parts[0]
        return (p * lax.rsqrt(ss * (1.0 / HEAD_DIM) + EPS)) * g_ref[...]

    def seg(c0, width):
        return _mm(h, wqkv_ref, precise, slice(c0, c0 + width))

    def rope_all(y, scale=None):
        outs = []
        for c in range(0, y.shape[1], LANES):
            r = rope(y[:, c:c + LANES])
            outs.append(r if scale is None else r * scale)
        return jnp.concatenate(outs, axis=1) if len(outs) > 1 else outs[0]

    c0 = 0
    o["qd"][...] = rope_all(head_norm(seg(c0, w_diff), gqd_ref), qk_scale * LOG2E).astype(act)
    c0 += w_diff
    kd = rope_all(head_norm(seg(c0, w_diff), gkd_ref))
    kd_t = kd.T
    o["kdT_f"][...] = kd_t
    o["kdT_b"][...] = kd_t.astype(BF16)
    if "kd_b" in o:
        o["kd_b"][...] = kd.astype(act)
    c0 += w_diff
    vd = seg(c0, w_diff)
    o["vd_f"][...] = vd
    o["vd_b"][...] = vd.astype(act)
    c0 += w_diff
    o["qs"][...] = rope_all(head_norm(seg(c0, w_dsa), gqs_ref), qk_scale * LOG2E).astype(act)
    c0 += w_dsa
    ks = rope_all(head_norm(seg(c0, w_dsa), gks_ref))
    ks_t = ks.T
    o["ksT_f"][...] = ks_t
    o["ksT_b"][...] = ks_t.astype(BF16)
    if "ks_b" in o:
        o["ks_b"][...] = ks.astype(act)
    c0 += w_dsa
    vs = seg(c0, w_dsa)
    o["vsT_f"][...] = vs.T
    o["vs_b"][...] = vs.astype(act)
    c0 += w_dsa
    o["qi"][...] = rope_all(seg(c0, w_idx), qk_scale).astype(act)

    sm = _mm(h, wsm_ref, precise)
    ki = rope(sm)
    ki2 = jnp.where(lane < HEAD_DIM, ki, pltpu.roll(ki, HEAD_DIM, 1))
    ki2_t = ki2.T
    o["kiT_f"][...] = ki2_t[:HEAD_DIM]
    o["ki2T_b"][...] = ki2_t.astype(BF16)
    if "ki_b" in o:
        o["ki_b"][...] = ki2.astype(act)
    o["wi"][...] = sm * (h_idx ** -0.5)

    g = _mm(h, wgate_ref, precise)
    o["gates"][...] = (1.0 / (1.0 + jnp.exp(-g))).astype(act)


def _inproj(x, cos_t, sin_t, bsz, tm, kc, prm, sample):
    n, d = x.shape
    t = n // bsz
    tiles = t // tm
    per_chunk = kc // tm
    w_diff, w_dsa, w_idx, h_idx = prm["w_diff"], prm["w_dsa"], prm["w_idx"], prm["h_idx"]
    wg = prm["w_gate"].shape[1]
    act = F32 if sample else BF16
    row = lambda i: (i, 0)
    const = lambda i: (0, 0)
    pos = lambda i: (i % tiles, 0)
    full = lambda a: pl.BlockSpec(a.shape, const, pipeline_mode=pl.Buffered(1))
    proj = ("w_qkv_f", "w_small_f", "w_gate_f") if sample else ("w_qkv", "w_small", "w_gate")
    weights = [prm[k] for k in ("g_mix",) + proj + ("g_qd", "g_kd", "g_qs", "g_ks", "bd")]
    in_specs = ([pl.BlockSpec((tm, d), row), pl.BlockSpec((tm, LANES), pos), pl.BlockSpec((tm, LANES), pos)]
                + [full(w) for w in weights])

    def nat(w, dt):
        return jax.ShapeDtypeStruct((n, w), dt), pl.BlockSpec((tm, w), row)

    def tr_f(w):
        return (jax.ShapeDtypeStruct((bsz, w, t), F32),
                pl.BlockSpec((None, w, tm), lambda i: (i // tiles, 0, i % tiles)))

    def tr_b(w):
        return (jax.ShapeDtypeStruct((bsz, t // kc, w, kc), BF16),
                pl.BlockSpec((None, None, w, tm),
                             lambda i: (i // tiles, (i % tiles) // per_chunk, 0, (i % tiles) % per_chunk)))

    table = dict(qd=nat(w_diff, act), kdT_f=tr_f(w_diff), kdT_b=tr_b(w_diff), vd_f=nat(w_diff, F32),
                 vd_b=nat(w_diff, act), qs=nat(w_dsa, act), ksT_f=tr_f(w_dsa), ksT_b=tr_b(w_dsa),
                 vsT_f=tr_f(w_dsa), vs_b=nat(w_dsa, act), qi=nat(w_idx, act), kiT_f=tr_f(HEAD_DIM),
                 ki2T_b=tr_b(LANES), wi=nat(LANES, F32), gates=nat(wg, act),
                 kd_b=nat(w_diff, act), ks_b=nat(w_dsa, act), ki_b=nat(LANES, act))
    names = _INPROJ_OUTS + (_INPROJ_NATURAL if sample else ())
    kern = functools.partial(_inproj_kernel, names=names, precise=sample, w_diff=w_diff, w_dsa=w_dsa, w_idx=w_idx,
                             h_idx=h_idx, qk_scale=HEAD_DIM ** -0.5)
    outs = pl.pallas_call(kern, grid=(n // tm,), in_specs=in_specs,
                          out_specs=[table[k][1] for k in names], out_shape=[table[k][0] for k in names],
                          compiler_params=_params("parallel"), name="inproj")(x, cos_t, sin_t, *weights)
    return dict(zip(names, outs))


def _stack_halves(q):
    lane = lax.broadcasted_iota(I32, q.shape, 1)
    zero = jnp.zeros_like(q)
    return jnp.concatenate([jnp.where(lane < HEAD_DIM, q, zero), jnp.where(lane >= HEAD_DIM, q, zero)], axis=0)


def _lane_fold(a):
    out = a[:, :LANES]
    for c in range(LANES, a.shape[1], LANES):
        out = out + a[:, c:c + LANES]
    return out


def _online_update(s_ref, v, m_sc, l_sc, acc_sc, fin=None):
    s = s_ref[...]
    if fin is not None:
        s = fin(s, 0)
    m_prev = m_sc[...]
    m_new = jnp.maximum(m_prev, jnp.max(s, axis=-1, keepdims=True))
    alpha = jnp.exp2(m_prev - m_new)
    p = jnp.exp2(s - jnp.tile(m_new, (1, s.shape[1] // LANES)))
    l_sc[...] = alpha * l_sc[...] + _lane_fold(p)
    acc_sc[...] = alpha * acc_sc[...] + _dot(p.astype(BF16), v)
    m_sc[...] = m_new


def _finish(l_sc, acc_sc):
    return acc_sc[...] * (1.0 / jnp.sum(l_sc[...], axis=-1, keepdims=True))


def _attend_chunks(n, score, value, s_a, s_b, m_sc, l_sc, acc_sc, last=None):
    s_a[...] = score(0)

    def pair(p, carry):
        j = 2 * p
        s_b[...] = score(j + 1)
        _online_update(s_a, value(j), m_sc, l_sc, acc_sc)
        s_a[...] = score(j + 2)
        _online_update(s_b, value(j + 1), m_sc, l_sc, acc_sc)
        return carry

    lax.fori_loop(0, (n - 1) // 2, pair, 0)

    @pl.when(n % 2 == 1)
    def _():
        _online_update(s_a, value(n - 1), m_sc, l_sc, acc_sc, last)

    @pl.when(n % 2 == 0)
    def _():
        s_b[...] = score(n - 1)
        _online_update(s_a, value(n - 2), m_sc, l_sc, acc_sc)
        _online_update(s_b, value(n - 1), m_sc, l_sc, acc_sc, last)


def _lambda(lam_ref, lam_init):
    a = jnp.sum(lam_ref[0:1, :] * lam_ref[1:2, :], axis=-1, keepdims=True)
    b = jnp.sum(lam_ref[2:3, :] * lam_ref[3:4, :], axis=-1, keepdims=True)
    return jnp.exp(a) - jnp.exp(b) + lam_init


def _attn_scratch(rows, kc):
    return [pltpu.VMEM((rows, LANES), F32)] * 3 + [pltpu.VMEM((rows, kc), F32)] * 2


def _diff_attn_kernel(lam_ref, gsub_ref, q_ref, kt_ref, v_ref, o_ref, m_sc, l_sc, acc_sc, s_a, s_b, *,
                      tq, kc, lam_init):
    q0 = pl.program_id(2) * tq
    n_c = q0 // kc + 1
    q2 = _stack_halves(q_ref[...])
    m_sc[...] = jnp.full(m_sc.shape, -jnp.inf, F32)
    l_sc[...] = jnp.zeros(l_sc.shape, F32)
    acc_sc[...] = jnp.zeros(acc_sc.shape, F32)

    def causal(s, row0):
        r = lax.broadcasted_iota(I32, s.shape, 0) + row0
        r = jnp.where(r >= tq, r - tq, r) + q0
        c = lax.broadcasted_iota(I32, s.shape, 1) + (n_c - 1) * kc
        return jnp.where(c <= r, s, -jnp.inf)

    _attend_chunks(n_c, lambda j: _dot(q2, kt_ref[j]),
                   lambda j: v_ref[pl.ds(pl.multiple_of(j * kc, kc), kc), :],
                   s_a, s_b, m_sc, l_sc, acc_sc, last=causal)

    a = _finish(l_sc, acc_sc)
    o = a[:tq] - _lambda(lam_ref, lam_init) * a[tq:]
    ms = jnp.mean(o * o, axis=-1, keepdims=True)
    on = ((o * lax.rsqrt(ms + EPS)) * gsub_ref[...]) * (1.0 - lam_init)
    o_ref[...] = on.astype(BF16)


def _diff_attn(q, kt, v, lam_vecs, g_sub, lam_init, tq):
    b, t, w = q.shape
    nc, kc = kt.shape[1], kt.shape[3]
    assert kc % tq == 0
    grid = (b, w // LANES, t // tq)
    qmap = lambda bi, hi, qi: (bi, qi, hi)
    const = lambda bi, hi, qi: (0, 0)
    kern = functools.partial(_diff_attn_kernel, tq=tq, kc=kc, lam_init=lam_init)
    return pl.pallas_call(
        kern, grid=grid,
        in_specs=[pl.BlockSpec(lam_vecs.shape, const), pl.BlockSpec(g_sub.shape, const),
                  pl.BlockSpec((None, tq, LANES), qmap),
                  pl.BlockSpec((None, nc, LANES, kc), lambda bi, hi, qi: (bi, 0, hi, 0)),
                  pl.BlockSpec((None, t, LANES), lambda bi, hi, qi: (bi, 0, hi))],
        out_specs=pl.BlockSpec((None, tq, LANES), qmap),
        out_shape=jax.ShapeDtypeStruct((b, t, w), BF16),
        scratch_shapes=_attn_scratch(2 * tq, kc),
        compiler_params=_params("parallel", "parallel", "arbitrary"), name="diff_attn")(lam_vecs, g_sub, q, kt, v)


def _key_to_f32(k):
    b = k ^ (lax.shift_right_arithmetic(k, 31) & 0x7FFFFFFF)
    return lax.bitcast_convert_type(b, F32)


def _select_bias(score_sc, bias_sc, n_c, k_sel, row_pos, col0, t_sc=None):
    nc_max, r, kc = score_sc.shape

    def count(pred):
        def body(c, a):
            return a + _lane_fold(jnp.where(pred(score_sc[c]), 1.0, 0.0))
        a = lax.fori_loop(0, n_c, body, jnp.zeros((r, LANES), F32))
        return jnp.sum(a, axis=-1, keepdims=True)

    def search(k):
        def bit_body(i, t):
            cand = t + lax.shift_left(jnp.int32(1), 31 - i)
            cand_f = _key_to_f32(cand)
            a = jnp.zeros((r, LANES), F32)
            for c in range(k):
                a = a + _lane_fold(jnp.where(score_sc[c] >= cand_f, 1.0, 0.0))
            cnt = jnp.sum(a, axis=-1, keepdims=True)
            return jnp.where(cnt >= k_sel, cand, t)
        return lax.fori_loop(0, 32, bit_body, jnp.full((r, 1), INT_MIN, I32))

    if isinstance(n_c, int):
        t = search(n_c)
    else:
        for k in range(1, nc_max + 1):
            @pl.when(n_c == k)
            def _(k=k):
                t_sc[...] = jnp.broadcast_to(search(k), t_sc.shape)
        t = t_sc[:, :1]
    thr = _key_to_f32(jnp.maximum(t, KEY_NEG_INF))
    need = k_sel - count(lambda sc: sc > thr)

    ri = lax.broadcasted_iota(I32, (kc, kc), 0)
    ci = lax.broadcasted_iota(I32, (kc, kc), 1)
    tri = jnp.where(ri <= ci, 1.0, 0.0).astype(BF16)
    col = lax.broadcasted_iota(I32, (r, kc), 1)

    def bias_body(c, run):
        sc = score_sc[c]
        eq = sc == thr
        pre = _dot(jnp.where(eq, 1.0, 0.0).astype(BF16), tri)
        sel = (sc > thr) | (eq & ((pre + run) <= need))
        causal = (col + (col0 + c * kc)) <= row_pos
        bias_sc[c] = jnp.where(sel & causal, 0.0, MASK_NEG)
        return run + pre[:, kc - 1:kc]

    lax.fori_loop(0, n_c, bias_body, jnp.zeros((r, 1), F32))


def _dsa_attn_kernel(qi_ref, ki2t_ref, wi_ref, qs_ref, kst_ref, vs_ref, o_ref,
                     qstack_sc, score_sc, bias_sc, t_sc, q2_sc, m_sc, l_sc, acc_sc, *, tq, kc, k_sel, h_idx):
    qb = pl.program_id(1)
    q0 = qb * tq
    n_c = q0 // kc + 1
    row_pos = q0 + lax.broadcasted_iota(I32, (tq, 1), 0)

    for hh in range(h_idx):
        pair = qi_ref[:, (hh // 2) * LANES:(hh // 2 + 1) * LANES]
        lane = lax.broadcasted_iota(I32, pair.shape, 1)
        keep = (lane < HEAD_DIM) if hh % 2 == 0 else (lane >= HEAD_DIM)
        qstack_sc[hh * tq:(hh + 1) * tq, :] = jnp.where(keep, pair, jnp.zeros_like(pair))
    wi = wi_ref[...]

    def score_body(c, carry):
        d = _dot(qstack_sc[...], ki2t_ref[c])
        acc = jnp.zeros((tq, kc), F32)
        for hh in range(h_idx):
            acc = acc + jnp.maximum(d[hh * tq:(hh + 1) * tq], 0.0) * wi[:, HEAD_DIM + hh:HEAD_DIM + hh + 1]
        col = c * kc + lax.broadcasted_iota(I32, (tq, kc), 1)
        score_sc[c] = jnp.where(col <= row_pos, acc, -jnp.inf)
        return carry

    lax.fori_loop(0, n_c, score_body, 0)
    _select_bias(score_sc, bias_sc, n_c, k_sel, row_pos, 0, t_sc)

    n_hp = qs_ref.shape[1] // LANES
    for hp in range(n_hp):
        q2_sc[hp] = _stack_halves(qs_ref[:, hp * LANES:(hp + 1) * LANES])
    m_sc[...] = jnp.full(m_sc.shape, MASK_NEG, F32)
    l_sc[...] = jnp.zeros(l_sc.shape, F32)
    acc_sc[...] = jnp.zeros(acc_sc.shape, F32)

    def attn_body(c, carry):
        bias = bias_sc[c]
        bias2 = jnp.concatenate([bias, bias], axis=0)
        rows = pl.ds(pl.multiple_of(c * kc, kc), kc)
        for hp in range(n_hp):
            cols = slice(hp * LANES, (hp + 1) * LANES)
            s = _dot(q2_sc[hp], kst_ref[c, cols, :]) + bias2
            _online_update(s, vs_ref[rows, cols], m_sc.at[hp], l_sc.at[hp], acc_sc.at[hp])
        return carry

    lax.fori_loop(0, n_c, attn_body, 0)
    lane = lax.broadcasted_iota(I32, (tq, LANES), 1)
    for hp in range(n_hp):
        a = _finish(l_sc.at[hp], acc_sc.at[hp])
        o_ref[:, hp * LANES:(hp + 1) * LANES] = jnp.where(lane < HEAD_DIM, a[:tq], a[tq:]).astype(BF16)


def _dsa_attn(qi, ki2t, wi, qs, kst, vs, k_sel, h_idx, tq):
    b, t, w = qs.shape
    nc, kc = kst.shape[1], kst.shape[3]
    assert kc % tq == 0 and kc >= k_sel
    qmap = lambda bi, qb: (bi, qb, 0)
    kern = functools.partial(_dsa_attn_kernel, tq=tq, kc=kc, k_sel=k_sel, h_idx=h_idx)
    return pl.pallas_call(
        kern, grid=(b, t // tq),
        in_specs=[pl.BlockSpec((None, tq, qi.shape[2]), qmap),
                  pl.BlockSpec((None, nc, LANES, kc), lambda bi, qb: (bi, 0, 0, 0)),
                  pl.BlockSpec((None, tq, LANES), qmap), pl.BlockSpec((None, tq, w), qmap),
                  pl.BlockSpec((None, nc, w, kc), lambda bi, qb: (bi, 0, 0, 0)),
                  pl.BlockSpec((None, t, w), lambda bi, qb: (bi, 0, 0))],
        out_specs=pl.BlockSpec((None, tq, w), qmap),
        out_shape=jax.ShapeDtypeStruct((b, t, w), BF16),
        scratch_shapes=[pltpu.VMEM((h_idx * tq, LANES), BF16), pltpu.VMEM((nc, tq, kc), F32),
                        pltpu.VMEM((nc, tq, kc), F32), pltpu.VMEM((tq, LANES), I32),
                        pltpu.VMEM((w // LANES, 2 * tq, LANES), BF16)]
                       + [pltpu.VMEM((w // LANES, 2 * tq, LANES), F32)] * 3,
        compiler_params=_params("parallel", "arbitrary"), name="dsa_attn")(qi, ki2t, wi, qs, kst, vs)


def _page_specs(n_pages, block):
    def spec(j):
        return pl.BlockSpec((None,) + block, lambda b, pt, j=j: (pt[b, j],) + (0,) * len(block))
    return [spec(j) for j in range(n_pages)]


def _idx_score_kernel(pt_ref, q_ref, w_ref, knew_ref, *refs, n_pages, page, pad):
    page_refs, o_ref = refs[:n_pages], refs[n_pages]
    q = q_ref[...]
    w = w_ref[...]
    for j in range(n_pages):
        d = _dot3(q, page_refs[j][...])
        o_ref[:, j * page:(j + 1) * page] = jnp.sum(jnp.maximum(d, 0.0) * w, axis=0, keepdims=True)
    d_new = jnp.sum(q * knew_ref[...], axis=-1, keepdims=True)
    s_new = jnp.sum(jnp.maximum(d_new, 0.0) * w, axis=0, keepdims=True)
    lane = lax.broadcasted_iota(I32, (1, pad), 1)
    o_ref[:, n_pages * page:] = jnp.where(lane == 0, s_new, -jnp.inf)


def _idx_scores(page_table, q_i, w_i, k_new, cache_t, pad):
    db, n_pages = page_table.shape
    _, dh, page = cache_t.shape
    h_idx = q_i.shape[1]
    tk = n_pages * page + pad
    seq = lambda b, pt: (b, 0, 0)
    kern = functools.partial(_idx_score_kernel, n_pages=n_pages, page=page, pad=pad)
    grid_spec = pltpu.PrefetchScalarGridSpec(
        num_scalar_prefetch=1, grid=(db,),
        in_specs=[pl.BlockSpec((None, h_idx, dh), seq), pl.BlockSpec((None, h_idx, 1), seq),
                  pl.BlockSpec((None, 1, dh), seq)] + _page_specs(n_pages, (dh, page)),
        out_specs=pl.BlockSpec((None, 1, tk), seq))
    out = pl.pallas_call(kern, grid_spec=grid_spec, out_shape=jax.ShapeDtypeStruct((db, 1, tk), F32),
                         compiler_params=_params("parallel"), name="idx_scores")(
        page_table, q_i, w_i, k_new, *([cache_t] * n_pages))
    return out.reshape(db, tk)


def _select_kernel(s_ref, o_ref, score_sc, bias_sc, *, kc, k_sel, q_pos):
    r = s_ref.shape[0]
    nc = s_ref.shape[1] // kc
    for c in range(nc):
        score_sc[c] = s_ref[:, c * kc:(c + 1) * kc]
    row_pos = jnp.full((r, 1), q_pos, I32)
    _select_bias(score_sc, bias_sc, nc, k_sel, row_pos, 0)
    for c in range(nc):
        o_ref[:, c * kc:(c + 1) * kc] = bias_sc[c]


def _select(scores, k_sel, q_pos, kc):
    r, tk = scores.shape
    nc = tk // kc
    kern = functools.partial(_select_kernel, kc=kc, k_sel=k_sel, q_pos=q_pos)
    return pl.pallas_call(
        kern, grid=(1,), in_specs=[pl.BlockSpec((r, tk), lambda i: (0, 0))],
        out_specs=pl.BlockSpec((r, tk), lambda i: (0, 0)), out_shape=jax.ShapeDtypeStruct((r, tk), F32),
        scratch_shapes=[pltpu.VMEM((nc, r, kc), F32), pltpu.VMEM((nc, r, kc), F32)],
        compiler_params=_params("arbitrary"), name="select")(scores)


def _block_diag_rows(q, width):
    t = jnp.concatenate([q] * (width // HEAD_DIM), axis=1)
    row = lax.broadcasted_iota(I32, t.shape, 0)
    col = lax.broadcasted_iota(I32, t.shape, 1)
    return jnp.where((col >= row * HEAD_DIM) & (col < (row + 1) * HEAD_DIM), t, jnp.zeros_like(t))


def _decode_attn_kernel(pt_ref, lam_ref, gsub_ref, qd_ref, kdn_ref, vdn_ref, qs_ref, ksn_ref, vsn_ref, bias_ref,
                        *refs, n_pages, page, lam_init):
    kd_pages = refs[:n_pages]
    vd_pages = refs[n_pages:2 * n_pages]
    ks_pages = refs[2 * n_pages:3 * n_pages]
    vs_pages = refs[3 * n_pages:4 * n_pages]
    od_ref, os_ref = refs[4 * n_pages], refs[4 * n_pages + 1]
    h_diff = vd_pages[0].shape[0] // page

    def probs(q_ref, knew_ref, k_pages, bias):
        q = q_ref[...]
        qbd = _block_diag_rows(q, k_pages[0].shape[0])
        s = [_dot3(qbd, k_pages[j][...]) for j in range(n_pages)]
        s_new = jnp.sum(q * knew_ref[...], axis=-1, keepdims=True)
        if bias is not None:
            s = [s[j] + bias[:, j * page:(j + 1) * page] for j in range(n_pages)]
            s_new = s_new + bias[:, n_pages * page:n_pages * page + 1]
        m = s_new
        for j in range(n_pages):
            m = jnp.maximum(m, jnp.max(s[j], axis=-1, keepdims=True))
        p_new = jnp.exp2(s_new - m)
        p = [jnp.exp2(s[j] - m) for j in range(n_pages)]
        l = p_new
        for j in range(n_pages):
            l = l + jnp.sum(p[j], axis=-1, keepdims=True)
        return p, p_new, 1.0 / l

    p, p_new, inv_l = probs(qd_ref, kdn_ref, kd_pages, None)
    acc = p_new * vdn_ref[...]
    for j in range(n_pages):
        v = jnp.concatenate([vd_pages[j][pl.ds(hh, page, stride=h_diff), :] for hh in range(h_diff)], axis=1)
        acc = acc + _dot3(p[j], v)
    a = acc * inv_l
    row = lax.broadcasted_iota(I32, a.shape, 0)
    col = lax.broadcasted_iota(I32, a.shape, 1)
    own = (col // LANES) == (row // 2)
    coef = jnp.where(row % 2 == 0, 1.0, -_lambda(lam_ref, lam_init))
    o = jnp.sum(jnp.where(own, a * coef, 0.0), axis=0, keepdims=True)
    parts = []
    for c in range(0, o.shape[1], LANES):
        oc = o[:, c:c + LANES]
        ms = jnp.mean(oc * oc, axis=-1, keepdims=True)
        parts.append(((oc * lax.rsqrt(ms + EPS)) * gsub_ref[...]) * (1.0 - lam_init))
    od_ref[...] = jnp.concatenate(parts, axis=1)

    p, p_new, inv_l = probs(qs_ref, ksn_ref, ks_pages, bias_ref[...])
    acc = p_new * vsn_ref[...]
    for j in range(n_pages):
        acc = acc + _dot3(p[j], vs_pages[j][...], nt=True)
    a = acc * inv_l
    row = lax.broadcasted_iota(I32, a.shape, 0)
    col = lax.broadcasted_iota(I32, a.shape, 1)
    os_ref[...] = jnp.sum(jnp.where((col // HEAD_DIM) == row, a, 0.0), axis=0, keepdims=True)


def _decode_attn(page_table, lam_vecs, g_sub, qd, kdn, vdn, qs, ksn, vsn, bias, caches, lam_init):
    db, n_pages = page_table.shape
    c_kd, c_vd, c_ks, c_vs = caches
    w_diff, page = c_kd.shape[1:]
    w_dsa = c_ks.shape[1]
    seq = lambda b, pt: (b, 0, 0)
    const = lambda b, pt: (0, 0)
    in_specs = [pl.BlockSpec(lam_vecs.shape, const), pl.BlockSpec(g_sub.shape, const),
                pl.BlockSpec((None,) + qd.shape[1:], seq), pl.BlockSpec((None,) + kdn.shape[1:], seq),
                pl.BlockSpec((None,) + vdn.shape[1:], seq),
                pl.BlockSpec((None,) + qs.shape[1:], seq), pl.BlockSpec((None,) + ksn.shape[1:], seq),
                pl.BlockSpec((None,) + vsn.shape[1:], seq), pl.BlockSpec((None,) + bias.shape[1:], seq)]
    for cache in caches:
        in_specs += _page_specs(n_pages, tuple(cache.shape[1:]))
    kern = functools.partial(_decode_attn_kernel, n_pages=n_pages, page=page, lam_init=lam_init)
    grid_spec = pltpu.PrefetchScalarGridSpec(
        num_scalar_prefetch=1, grid=(db,), in_specs=in_specs,
        out_specs=[pl.BlockSpec((None, 1, w_diff), seq), pl.BlockSpec((None, 1, w_dsa), seq)])
    pages = []
    for cache in caches:
        pages += [cache] * n_pages
    od, os_ = pl.pallas_call(
        kern, grid_spec=grid_spec,
        out_shape=[jax.ShapeDtypeStruct((db, 1, w_diff), F32), jax.ShapeDtypeStruct((db, 1, w_dsa), F32)],
        compiler_params=_params("parallel"), name="decode_attn")(
        page_table, lam_vecs, g_sub, qd, kdn, vdn, qs, ksn, vsn, bias, *pages)
    return od.reshape(db, w_diff), os_.reshape(db, w_dsa)


ROUTE_E1, ROUTE_E2, ROUTE_R1, ROUTE_R2, ROUTE_W1, ROUTE_W2 = range(6)


def _outproj_kernel(x_ref, od_ref, os_ref, gate_ref, wod_ref, wos_ref, wout_ref, gffn_ref, wrt_ref,
                    x1_ref, h2_ref, comb_ref, route_ref, count_ref, *, precise, n_groups, n_experts):
    d = x_ref.shape[1]

    @pl.when(pl.program_id(0) == 0)
    def _():
        count_ref[...] = jnp.zeros(count_ref.shape, F32)

    yd = _mm(od_ref[...], wod_ref, precise)
    ys = _mm(os_ref[...], wos_ref, precise)
    merged = gate_ref[:, :d].astype(F32) * yd + gate_ref[:, d:].astype(F32) * ys
    x1 = x_ref[...] + _mm(merged, wout_ref, precise)
    x1_ref[...] = x1

    ms = jnp.mean(x1 * x1, axis=-1, keepdims=True)
    h2 = (x1 * lax.rsqrt(ms + EPS)) * gffn_ref[...]
    h2_ref[...] = h2

    lg = _dot3(h2, wrt_ref[...])
    lane = lax.broadcasted_iota(I32, lg.shape, 1)
    lanef = lane.astype(F32)
    big = float(LANES)
    epg = n_experts // n_groups

    gmask = (lane >= n_experts) & (lane < n_experts + n_groups)
    gl = jnp.where(gmask, lg, -jnp.inf)
    gmax = jnp.max(gl, axis=-1, keepdims=True)
    gsel = jnp.min(jnp.where(gl == gmax, lanef, big), axis=-1, keepdims=True) - n_experts
    g_w = 1.0 / jnp.sum(jnp.exp(gl - gmax), axis=-1, keepdims=True)

    lo_lane = gsel * epg
    emask = (lanef >= lo_lane) & (lanef < lo_lane + epg)
    el = jnp.where(emask, lg, -jnp.inf)
    v1 = jnp.max(el, axis=-1, keepdims=True)
    i1 = jnp.min(jnp.where(el == v1, lanef, big), axis=-1, keepdims=True)
    el2 = jnp.where(lanef == i1, -jnp.inf, el)
    v2 = jnp.max(el2, axis=-1, keepdims=True)
    i2 = jnp.min(jnp.where(el2 == v2, lanef, big), axis=-1, keepdims=True)
    e21 = jnp.exp(v2 - v1)
    den = 1.0 / (1.0 + e21)
    w1 = den * g_w
    w2 = (e21 * den) * g_w
    comb_ref[...] = jnp.where(lanef == i1, w1, 0.0) + jnp.where(lanef == i2, w2, 0.0)

    tm = lg.shape[0]
    onehot = jnp.where((lanef == i1) | (lanef == i2), 1.0, 0.0)
    ri = lax.broadcasted_iota(I32, (tm, tm), 0)
    ci = lax.broadcasted_iota(I32, (tm, tm), 1)
    before = jnp.where(ci < ri, 1.0, 0.0).astype(BF16)
    rank_e = count_ref[...] + _dot(before, onehot.astype(BF16))
    r1 = jnp.sum(jnp.where(lanef == i1, rank_e, 0.0), axis=-1, keepdims=True)
    r2 = jnp.sum(jnp.where(lanef == i2, rank_e, 0.0), axis=-1, keepdims=True)
    count_ref[...] += jnp.sum(onehot, axis=0, keepdims=True)
    rec = jnp.zeros(lg.shape, F32)
    for slot, val in ((ROUTE_E1, i1), (ROUTE_E2, i2), (ROUTE_W1, w1), (ROUTE_W2, w2), (ROUTE_R1, r1), (ROUTE_R2, r2)):
        rec = jnp.where(lane == slot, val, rec)
    route_ref[...] = rec


def _outproj(x, od, os_, gates, prm, tm, sample):
    n, d = x.shape
    row = lambda i: (i, 0)
    const = lambda i: (0, 0)
    full = lambda a: pl.BlockSpec(a.shape, const)
    proj = ("w_od_f", "w_os_f", "w_out_f") if sample else ("w_od", "w_os", "w_out")
    weights = [prm[k] for k in proj + ("g_ffn", "w_rt_f")]
    kern = functools.partial(_outproj_kernel, precise=sample, n_groups=prm["n_groups"], n_experts=prm["n_experts"])
    return pl.pallas_call(
        kern, grid=(n // tm,),
        in_specs=[pl.BlockSpec((tm, d), row), pl.BlockSpec((tm, od.shape[1]), row), pl.BlockSpec((tm, os_.shape[1]), row),
                  pl.BlockSpec((tm, gates.shape[1]), row)] + [full(w) for w in weights],
        out_specs=[pl.BlockSpec((tm, d), row), pl.BlockSpec((tm, d), row), pl.BlockSpec((tm, LANES), row),
                   pl.BlockSpec((tm, LANES), row), pl.BlockSpec((1, LANES), const)],
        out_shape=[jax.ShapeDtypeStruct((n, d), F32), jax.ShapeDtypeStruct((n, d), F32),
                   jax.ShapeDtypeStruct((n, LANES), F32), jax.ShapeDtypeStruct((n, LANES), F32),
                   jax.ShapeDtypeStruct((1, LANES), F32)],
        compiler_params=_params("arbitrary"), name="outproj")(x, od, os_, gates, *weights)


def _expert_ffn(h, wg_ref, wu_ref, wd_ref):
    h = h.astype(BF16)
    a = _dot(h, wg_ref[...].astype(BF16))
    u = _dot(h, wu_ref[...].astype(BF16))
    act = (a * (1.0 / (1.0 + jnp.exp(-a)))) * u
    return _dot(act.astype(BF16), wd_ref[...].astype(BF16))


def _moe_kernel(x1_ref, h2_ref, comb_ref, wg_ref, wu_ref, wd_ref, y_ref, acc_sc):
    e = pl.program_id(1)

    @pl.when(e == 0)
    def _():
        acc_sc[...] = jnp.zeros(acc_sc.shape, F32)

    o = _expert_ffn(h2_ref[...], wg_ref, wu_ref, wd_ref)
    comb = comb_ref[...]
    lane = lax.broadcasted_iota(I32, comb.shape, 1)
    c = jnp.sum(jnp.where(lane == e, comb, 0.0), axis=-1, keepdims=True)
    acc_sc[...] += o * c

    @pl.when(e == pl.num_programs(1) - 1)
    def _():
        y_ref[...] = x1_ref[...] + acc_sc[...]


def _moe(x1, h2, comb, w_gate, w_up, w_down, tm):
    n, d = x1.shape
    ne, _, de = w_gate.shape
    row = lambda i, e: (i, 0)
    return pl.pallas_call(
        _moe_kernel, grid=(n // tm, ne),
        in_specs=[pl.BlockSpec((tm, d), row), pl.BlockSpec((tm, d), row), pl.BlockSpec((tm, LANES), row),
                  pl.BlockSpec((None, d, de), lambda i, e: (e, 0, 0)), pl.BlockSpec((None, d, de), lambda i, e: (e, 0, 0)),
                  pl.BlockSpec((None, de, d), lambda i, e: (e, 0, 0))],
        out_specs=pl.BlockSpec((tm, d), row), out_shape=jax.ShapeDtypeStruct((n, d), F32),
        scratch_shapes=[pltpu.VMEM((tm, d), F32)],
        compiler_params=_params("parallel", "arbitrary"), name="moe")(x1, h2, comb, w_gate, w_up, w_down)


def _row_copy(src, src_row, dst, dst_row, sem):
    return pltpu.make_async_copy(src.at[pl.ds(src_row, 1), :], dst.at[pl.ds(dst_row, 1), :], sem)


def _slot_rows(assign_ref, offs_ref, n, tok):
    return (offs_ref[assign_ref[tok]] + assign_ref[2 * n + tok],
            offs_ref[assign_ref[n + tok]] + assign_ref[3 * n + tok])


def _scatter_kernel(assign_ref, offs_ref, group_ref, n_tiles_ref, h2_ref, xs_hbm, zero_sc, sem, zero_sem, *,
                    n, tm, tr):
    base = pl.program_id(0) * tm
    n_experts = offs_ref.shape[0]
    total_tiles = xs_hbm.shape[0] // tr

    def zero_tile(row0):
        return pltpu.make_async_copy(zero_sc, xs_hbm.at[pl.ds(pl.multiple_of(row0, tr), tr), :], zero_sem)

    @pl.when(pl.program_id(0) == 0)
    def _():
        zero_sc[...] = jnp.zeros(zero_sc.shape, F32)
        for wait in (False, True):
            def group_tail(e, carry, wait=wait):
                @pl.when(group_ref[e] > 0)
                def _():
                    cp = zero_tile(offs_ref[e] + group_ref[e] - tr)
                    cp.wait() if wait else cp.start()
                return carry

            def unused(j, carry, wait=wait):
                cp = zero_tile(j * tr)
                cp.wait() if wait else cp.start()
                return carry

            lax.fori_loop(0, n_experts, group_tail, 0)
            lax.fori_loop(n_tiles_ref[0], total_tiles, unused, 0)

    def start(t, carry):
        p1, p2 = _slot_rows(assign_ref, offs_ref, n, base + t)
        _row_copy(h2_ref, t, xs_hbm, p1, sem).start()
        _row_copy(h2_ref, t, xs_hbm, p2, sem).start()
        return carry

    def wait(t, carry):
        p1, p2 = _slot_rows(assign_ref, offs_ref, n, base + t)
        _row_copy(h2_ref, t, xs_hbm, p1, sem).wait()
        _row_copy(h2_ref, t, xs_hbm, p2, sem).wait()
        return carry

    lax.fori_loop(0, tm, start, 0, unroll=8)
    lax.fori_loop(0, tm, wait, 0, unroll=8)


def _scatter_rows(assign, offs, group, n_tiles, h2, n_rows, tm, tr):
    n, d = h2.shape
    grid_spec = pltpu.PrefetchScalarGridSpec(
        num_scalar_prefetch=4, grid=(n // tm,),
        in_specs=[pl.BlockSpec((tm, d), lambda i, *_: (i, 0))], out_specs=pl.BlockSpec(memory_space=pl.ANY),
        scratch_shapes=[pltpu.VMEM((tr, d), F32), pltpu.SemaphoreType.DMA(()), pltpu.SemaphoreType.DMA(())])
    return pl.pallas_call(
        functools.partial(_scatter_kernel, n=n, tm=tm, tr=tr), grid_spec=grid_spec,
        out_shape=jax.ShapeDtypeStruct((n_rows, d), F32),
        compiler_params=_params("arbitrary", disable_bounds_checks=True), name="moe_scatter")(
        assign, offs, group, n_tiles, h2)


def _grouped_ffn_kernel(tile_expert_ref, n_tiles_ref, xs_ref, wg_ref, wu_ref, wd_ref, ys_ref):
    used = pl.program_id(0) < n_tiles_ref[0]

    @pl.when(used)
    def _():
        ys_ref[...] = _expert_ffn(xs_ref[...], wg_ref, wu_ref, wd_ref)

    @pl.when(jnp.logical_not(used))
    def _():
        ys_ref[...] = jnp.zeros(ys_ref.shape, F32)


def _grouped_ffn(tile_expert, n_tiles, xs, w_gate, w_up, w_down, tr):
    rows, d = xs.shape
    ne, _, de = w_gate.shape
    last = lambda i, te, nt: jnp.minimum(i, nt[0] - 1)
    xmap = lambda i, te, nt: (last(i, te, nt), 0)
    wmap = lambda i, te, nt: (te[last(i, te, nt)], 0, 0)
    grid_spec = pltpu.PrefetchScalarGridSpec(
        num_scalar_prefetch=2, grid=(rows // tr,),
        in_specs=[pl.BlockSpec((tr, d), xmap), pl.BlockSpec((None, d, de), wmap), pl.BlockSpec((None, d, de), wmap),
                  pl.BlockSpec((None, de, d), wmap)],
        out_specs=pl.BlockSpec((tr, d), lambda i, te, nt: (i, 0)))
    return pl.pallas_call(
        _grouped_ffn_kernel, grid_spec=grid_spec, out_shape=jax.ShapeDtypeStruct((rows, d), F32),
        compiler_params=_params("arbitrary"), name="moe_ffn")(tile_expert, n_tiles, xs, w_gate, w_up, w_down)


def _combine_kernel(assign_ref, offs_ref, x1_ref, route_ref, ys_hbm, y_ref, buf1, buf2, sem, *, n, tm):
    base = pl.program_id(0) * tm

    def start(t, carry):
        p1, p2 = _slot_rows(assign_ref, offs_ref, n, base + t)
        _row_copy(ys_hbm, p1, buf1, t, sem).start()
        _row_copy(ys_hbm, p2, buf2, t, sem).start()
        return carry

    def wait(t, carry):
        p1, p2 = _slot_rows(assign_ref, offs_ref, n, base + t)
        _row_copy(ys_hbm, p1, buf1, t, sem).wait()
        _row_copy(ys_hbm, p2, buf2, t, sem).wait()
        return carry

    lax.fori_loop(0, tm, start, 0, unroll=8)
    lax.fori_loop(0, tm, wait, 0, unroll=8)
    route = route_ref[...]
    w1 = route[:, ROUTE_W1:ROUTE_W1 + 1]
    w2 = route[:, ROUTE_W2:ROUTE_W2 + 1]
    y_ref[...] = (x1_ref[...] + buf1[...] * w1) + buf2[...] * w2


def _combine_rows(assign, offs, x1, route, ys, tm):
    n, d = x1.shape
    row = lambda i, *_: (i, 0)
    grid_spec = pltpu.PrefetchScalarGridSpec(
        num_scalar_prefetch=2, grid=(n // tm,),
        in_specs=[pl.BlockSpec((tm, d), row), pl.BlockSpec((tm, LANES), row), pl.BlockSpec(memory_space=pl.ANY)],
        out_specs=pl.BlockSpec((tm, d), row),
        scratch_shapes=[pltpu.VMEM((tm, d), F32), pltpu.VMEM((tm, d), F32), pltpu.SemaphoreType.DMA(())])
    return pl.pallas_call(
        functools.partial(_combine_kernel, n=n, tm=tm), grid_spec=grid_spec,
        out_shape=jax.ShapeDtypeStruct((n, d), F32),
        compiler_params=_params("arbitrary", disable_bounds_checks=True), name="moe_combine")(
        assign, offs, x1, route, ys)


def _routed_moe(x1, h2, route, counts, w_gate, w_up, w_down, n_experts, tm, tr):
    n, d = x1.shape
    cnt = counts[0, :n_experts].astype(I32)
    group = ((cnt + tr - 1) // tr) * tr
    ends = jnp.cumsum(group)
    offs = ends - group
    assign = route[:, ROUTE_E1:ROUTE_R2 + 1].astype(I32).T.reshape(-1)
    n_rows = 2 * n + n_experts * tr
    tile_start = jnp.arange(n_rows // tr, dtype=I32) * tr
    tile_expert = jnp.minimum(jnp.sum(tile_start[:, None] >= ends[None, :], axis=1), n_experts - 1).astype(I32)
    n_tiles = (ends[-1:] // tr).astype(I32)
    xs = _scatter_rows(assign, offs, group, n_tiles, h2, n_rows, tm, tr)
    ys = _grouped_ffn(tile_expert, n_tiles, xs, w_gate, w_up, w_down, tr)
    return _combine_rows(assign, offs, x1, route, ys, tm)


def _rope_tables(pos):
    half = HEAD_DIM // 2
    inv_freq = jnp.power(ROPE_THETA, -jnp.arange(half, dtype=F32) / half)
    ang = pos.astype(F32)[:, None] * inv_freq[None, :]
    cos, sin = jnp.cos(ang), jnp.sin(ang)
    reps = LANES // HEAD_DIM
    return jnp.tile(cos, (1, 2 * reps)), jnp.tile(jnp.concatenate([-sin, sin], axis=1), (1, reps))


def _tile_rows(n, target):
    t = min(n, target)
    while n % t:
        t //= 2
    return t


def _key_outputs(o, bsz, t, h_diff, h_dsa, dh):
    kd = o["kdT_f"].reshape(bsz, h_diff, 2, dh, t).transpose(0, 4, 1, 2, 3)
    vd = o["vd_f"].reshape(bsz, t, h_diff, 2 * dh)
    ks = o["ksT_f"].reshape(bsz, h_dsa, dh, t).transpose(0, 3, 1, 2)
    vs = o["vsT_f"].reshape(bsz, h_dsa, dh, t).transpose(0, 3, 1, 2)
    ki = o["kiT_f"].transpose(0, 2, 1)
    return kd, vd, ks, vs, ki


def kernel(x_prompt, x_sample, cache_diff_k, cache_diff_v, cache_dsa_k, cache_dsa_v, cache_idx_k, page_table, g_mix, w_in, g_q_diff, g_k_diff, lam_q1, lam_k1, lam_q2, lam_k2, g_subln, g_q_dsa, g_k_dsa, w_o_diff, w_o_dsa, w_out, g_ffn, w_group, w_router, w_gate, w_up, w_down):
    bsz, seq, d_model = x_prompt.shape
    db, dec_seq, _ = x_sample.shape
    depth, n_pool, page, h_diff, _, dh = cache_diff_k.shape
    h_dsa = cache_dsa_k.shape[3]
    n_pages = page_table.shape[1]
    past_len = n_pages * page
    n_groups = w_group.shape[2]
    n_experts = w_router.shape[2]
    w_diff = h_diff * 2 * dh
    w_dsa = h_dsa * dh
    h_idx = (w_in.shape[2] - (3 * w_diff + 3 * w_dsa + dh + 2 * d_model)) // (dh + 1)
    w_idx = h_idx * dh
    assert dh == HEAD_DIM and cache_dsa_k.shape[4] == HEAD_DIM and cache_idx_k.shape[3] == HEAD_DIM
    assert dec_seq == 1 and depth == 1
    assert 3 * w_diff + 3 * w_dsa + w_idx + dh + h_idx + 2 * d_model == w_in.shape[2]
    k_sel_p = min(TOPK_MAX, seq // 4)
    k_sel_s = min(TOPK_MAX, (past_len + dec_seq) // 4)

    l = 0
    lam_init = 0.8 - 0.6 * math.exp(-0.3 * l)
    n_qkv = 3 * w_diff + 3 * w_dsa + w_idx
    w_l = w_in[l]
    rt = jnp.concatenate([w_router[l], w_group[l]], axis=1)
    rt = jnp.pad(rt, ((0, 0), (0, LANES - rt.shape[1])))
    blk = lax.broadcasted_iota(I32, (2 * LANES, 2 * LANES), 0) // HEAD_DIM
    blk_c = lax.broadcasted_iota(I32, (2 * LANES, 2 * LANES), 1) // HEAD_DIM
    prm = dict(
        w_diff=w_diff, w_dsa=w_dsa, w_idx=w_idx, h_idx=h_idx, n_groups=n_groups, n_experts=n_experts,
        g_mix=g_mix[l][None, :],
        g_qd=jnp.tile(g_q_diff[l], w_diff // dh)[None, :], g_kd=jnp.tile(g_k_diff[l], w_diff // dh)[None, :],
        g_qs=jnp.tile(g_q_dsa[l], w_dsa // dh)[None, :], g_ks=jnp.tile(g_k_dsa[l], w_dsa // dh)[None, :],
        bd=(blk == blk_c).astype(BF16), g_ffn=g_ffn[l][None, :])
    for name, w in (("w_qkv", w_l[:, :n_qkv]),
                    ("w_small", jnp.pad(w_l[:, n_qkv:n_qkv + dh + h_idx], ((0, 0), (0, LANES - dh - h_idx)))),
                    ("w_gate", w_l[:, n_qkv + dh + h_idx:]),
                    ("w_od", w_o_diff[l]), ("w_os", w_o_dsa[l]), ("w_out", w_out[l]), ("w_rt", rt)):
        prm[name] = w.astype(BF16)
        prm[name + "_f"] = w
    lam_vecs = jnp.stack([lam_q1[l], lam_k1[l], lam_q2[l], lam_k2[l]])
    g_sub = g_subln[l][None, :]
    moe_w = (w_gate[l], w_up[l], w_down[l])

    n_p = bsz * seq
    tm = _tile_rows(seq, ROW_TILE)
    kc = _tile_rows(seq, KEY_CHUNK)
    cos_p, sin_p = _rope_tables(jnp.arange(seq, dtype=I32))
    x_p = x_prompt.reshape(n_p, d_model)
    o = _inproj(x_p, cos_p, sin_p, bsz, tm, kc, prm, sample=False)
    r3 = lambda a: a.reshape(bsz, seq, a.shape[1])
    od = _diff_attn(r3(o["qd"]), o["kdT_b"], r3(o["vd_b"]), lam_vecs, g_sub, lam_init, tm)
    os_ = _dsa_attn(r3(o["qi"]), o["ki2T_b"], r3(o["wi"]), r3(o["qs"]), o["ksT_b"], r3(o["vs_b"]),
                    k_sel_p, h_idx, _tile_rows(seq, DSA_Q_TILE))
    x1, h2, _, route, counts = _outproj(x_p, od.reshape(n_p, w_diff), os_.reshape(n_p, w_dsa), o["gates"], prm, tm,
                                        sample=False)
    y_p = _routed_moe(x1, h2, route, counts, *moe_w, n_experts, tm, MOE_ROW_TILE)
    outs_p = _key_outputs(o, bsz, seq, h_diff, h_dsa, dh)

    cos_s, sin_s = _rope_tables(jnp.full((db,), past_len, I32))
    x_s = x_sample.reshape(db, d_model)
    o = _inproj(x_s, cos_s, sin_s, 1, db, db, prm, sample=True)
    pad = 2 * LANES
    assert pad >= k_sel_s
    scores = _idx_scores(page_table, o["qi"].reshape(db, h_idx, dh), o["wi"][:, dh:dh + h_idx].reshape(db, h_idx, 1),
                         o["ki_b"][:, :dh].reshape(db, 1, dh), cache_idx_k[l].transpose(0, 2, 1), pad)
    bias = _select(scores, k_sel_s, past_len, pad)
    caches = (cache_diff_k[l].transpose(0, 2, 3, 4, 1).reshape(n_pool, w_diff, page),
              cache_diff_v[l].reshape(n_pool, page * h_diff, 2 * dh),
              cache_dsa_k[l].transpose(0, 2, 3, 1).reshape(n_pool, w_dsa, page),
              cache_dsa_v[l].transpose(0, 2, 3, 1).reshape(n_pool, w_dsa, page))
    od, os_ = _decode_attn(page_table, lam_vecs, g_sub,
                           o["qd"].reshape(db, 2 * h_diff, dh), o["kd_b"].reshape(db, 2 * h_diff, dh),
                           o["vd_b"].reshape(db, 1, w_diff),
                           o["qs"].reshape(db, h_dsa, dh), o["ks_b"].reshape(db, h_dsa, dh),
                           o["vs_b"].reshape(db, 1, w_dsa),
                           bias.reshape(db, 1, bias.shape[1]), caches, lam_init)
    x1, h2, comb, _, _ = _outproj(x_s, od, os_, o["gates"], prm, db, sample=True)
    y_s = _moe(x1, h2, comb, *moe_w, db)
    outs_s = tuple(a.reshape((db, dec_seq) + a.shape[2:]) for a in _key_outputs(o, 1, db, h_diff, h_dsa, dh))

    return ((y_p.reshape(bsz, seq, d_model), y_s.reshape(db, dec_seq, d_model))
            + tuple(a[None] for a in outs_p) + tuple(a[None] for a in outs_s))
```

```python
import functools
import math

import jax
import jax.numpy as jnp
from jax import lax
from jax.experimental import pallas as pl
from jax.experimental.pallas import tpu as pltpu

F32 = jnp.float32
BF16 = jnp.bfloat16
I32 = jnp.int32

EPS = 1e-6
ROPE_THETA = 10000.0
TOPK_MAX = 256
LANES = 128
HEAD_DIM = 64
MASK_NEG = -1e30
INT_MIN = -(2 ** 31)
KEY_NEG_INF = -2139095041
VMEM_LIMIT = 56 * 1024 * 1024
ROW_TILE = 256
KEY_CHUNK = 512
DSA_Q_TILE = 256
MOE_ROW_TILE = 256
LOG2E = 1.4426950408889634


def _params(*sem, **kw):
    return pltpu.CompilerParams(dimension_semantics=sem, vmem_limit_bytes=VMEM_LIMIT, **kw)


def _dot(a, b):
    return jnp.dot(a, b, preferred_element_type=F32)


def _dot_nt(a, b):
    return lax.dot_general(a, b, (((1,), (1,)), ((), ())), preferred_element_type=F32)


def _split_bf16(a):
    hi = a.astype(BF16)
    lo = (a - hi.astype(F32)).astype(BF16)
    return hi, lo


_INPROJ_OUTS = ("qd", "kdT_f", "kdT_b", "vd_f", "vd_b", "qs", "ksT_f", "ksT_b", "vsT_f", "vs_b",
                "qi", "kiT_f", "ki2T_b", "wi", "gates")
_INPROJ_NATURAL = ("kd_b", "ks_b", "ki_b")


def _mm(a, w_ref, precise, cols=slice(None)):
    w = w_ref[:, cols]
    return _dot3(a, w) if precise else _dot(a.astype(BF16), w)


def _dot3(a, b, nt=False):
    dot = _dot_nt if nt else _dot
    a_hi, a_lo = _split_bf16(a)
    b_hi, b_lo = _split_bf16(b)
    return dot(a_hi, b_hi) + (dot(a_hi, b_lo) + dot(a_lo, b_hi))


def _inproj_kernel(x_ref, cos_ref, sin_ref, gmix_ref, wqkv_ref, wsm_ref, wgate_ref,
                   gqd_ref, gkd_ref, gqs_ref, gks_ref, bd_ref, *out_refs,
                   names, precise, w_diff, w_dsa, w_idx, h_idx, qk_scale):
    o = dict(zip(names, out_refs))
    act = F32 if precise else BF16
    x = x_ref[...]
    tm = x.shape[0]
    ms = jnp.mean(x * x, axis=-1, keepdims=True)
    h = (x * lax.rsqrt(ms + EPS)) * gmix_ref[...]
    if not precise:
        h = h.astype(BF16)

    cos = cos_ref[...]
    sin = sin_ref[...]
    lane = lax.broadcasted_iota(I32, (tm, LANES), 1)
    low_half = (lane & (HEAD_DIM // 2)) == 0

    def rope(y):
        partner = jnp.where(low_half, pltpu.roll(y, LANES - HEAD_DIM // 2, 1), pltpu.roll(y, HEAD_DIM // 2, 1))
        return y * cos + partner * sin

    bd = bd_ref[...]

    def head_norm(p, g_ref):
        hi, lo = _split_bf16(p * p)
        width = bd.shape[0]
        parts = []
        for c in range(0, p.shape[1], width):
            parts.append(_dot(hi[:, c:c + width], bd) + _dot(lo[:, c:c + width], bd))
        ss = jnp.concatenate(parts, axis=1) if len(parts) > 1 else parts[0]
        return (p * lax.rsqrt(ss * (1.0 / HEAD_DIM) + EPS)) * g_ref[...]

    def seg(c0, width):
        return _mm(h, wqkv_ref, precise, slice(c0, c0 + width))

    def rope_all(y, scale=None):
        outs = []
        for c in range(0, y.shape[1], LANES):
            r = rope(y[:, c:c + LANES])
            outs.append(r if scale is None else r * scale)
        return jnp.concatenate(outs, axis=1) if len(outs) > 1 else outs[0]

    c0 = 0
    o["qd"][...] = rope_all(head_norm(seg(c0, w_diff), gqd_ref), qk_scale * LOG2E).astype(act)
    c0 += w_diff
    kd = rope_all(head_norm(seg(c0, w_diff), gkd_ref))
    kd_t = kd.T
    o["kdT_f"][...] = kd_t
    o["kdT_b"][...] = kd_t.astype(BF16)
    if "kd_b" in o:
        o["kd_b"][...] = kd.astype(act)
    c0 += w_diff
    vd = seg(c0, w_diff)
    o["vd_f"][...] = vd
    o["vd_b"][...] = vd.astype(act)
    c0 += w_diff
    o["qs"][...] = rope_all(head_norm(seg(c0, w_dsa), gqs_ref), qk_scale * LOG2E).astype(act)
    c0 += w_dsa
    ks = rope_all(head_norm(seg(c0, w_dsa), gks_ref))
    ks_t = ks.T
    o["ksT_f"][...] = ks_t
    o["ksT_b"][...] = ks_t.astype(BF16)
    if "ks_b" in o:
        o["ks_b"][...] = ks.astype(act)
    c0 += w_dsa
    vs = seg(c0, w_dsa)
    o["vsT_f"][...] = vs.T
    o["vs_b"][...] = vs.astype(act)
    c0 += w_dsa
    o["qi"][...] = rope_all(seg(c0, w_idx), qk_scale).astype(act)

    sm = _mm(h, wsm_ref, precise)
    ki = rope(sm)
    ki2 = jnp.where(lane < HEAD_DIM, ki, pltpu.roll(ki, HEAD_DIM, 1))
    ki2_t = ki2.T
    o["kiT_f"][...] = ki2_t[:HEAD_DIM]
    o["ki2T_b"][...] = ki2_t.astype(BF16)
    if "ki_b" in o:
        o["ki_b"][...] = ki2.astype(act)
    o["wi"][...] = sm * (h_idx ** -0.5)

    g = _mm(h, wgate_ref, precise)
    o["gates"][...] = (1.0 / (1.0 + jnp.exp(-g))).astype(act)


def _inproj(x, cos_t, sin_t, bsz, tm, kc, prm, sample):
    n, d = x.shape
    t = n // bsz
    tiles = t // tm
    per_chunk = kc // tm
    w_diff, w_dsa, w_idx, h_idx = prm["w_diff"], prm["w_dsa"], prm["w_idx"], prm["h_idx"]
    wg = prm["w_gate"].shape[1]
    act = F32 if sample else BF16
    row = lambda i: (i, 0)
    const = lambda i: (0, 0)
    pos = lambda i: (i % tiles, 0)
    full = lambda a: pl.BlockSpec(a.shape, const, pipeline_mode=pl.Buffered(1))
    proj = ("w_qkv_f", "w_small_f", "w_gate_f") if sample else ("w_qkv", "w_small", "w_gate")
    weights = [prm[k] for k in ("g_mix",) + proj + ("g_qd", "g_kd", "g_qs", "g_ks", "bd")]
    in_specs = ([pl.BlockSpec((tm, d), row), pl.BlockSpec((tm, LANES), pos), pl.BlockSpec((tm, LANES), pos)]
                + [full(w) for w in weights])

    def nat(w, dt):
        return jax.ShapeDtypeStruct((n, w), dt), pl.BlockSpec((tm, w), row)

    def tr_f(w):
        return (jax.ShapeDtypeStruct((bsz, w, t), F32),
                pl.BlockSpec((None, w, tm), lambda i: (i // tiles, 0, i % tiles)))

    def tr_b(w):
        return (jax.ShapeDtypeStruct((bsz, t // kc, w, kc), BF16),
                pl.BlockSpec((None, None, w, tm),
                             lambda i: (i // tiles, (i % tiles) // per_chunk, 0, (i % tiles) % per_chunk)))

    table = dict(qd=nat(w_diff, act), kdT_f=tr_f(w_diff), kdT_b=tr_b(w_diff), vd_f=nat(w_diff, F32),
                 vd_b=nat(w_diff, act), qs=nat(w_dsa, act), ksT_f=tr_f(w_dsa), ksT_b=tr_b(w_dsa),
                 vsT_f=tr_f(w_dsa), vs_b=nat(w_dsa, act), qi=nat(w_idx, act), kiT_f=tr_f(HEAD_DIM),
                 ki2T_b=tr_b(LANES), wi=nat(LANES, F32), gates=nat(wg, act),
                 kd_b=nat(w_diff, act), ks_b=nat(w_dsa, act), ki_b=nat(LANES, act))
    names = _INPROJ_OUTS + (_INPROJ_NATURAL if sample else ())
    kern = functools.partial(_inproj_kernel, names=names, precise=sample, w_diff=w_diff, w_dsa=w_dsa, w_idx=w_idx,
                             h_idx=h_idx, qk_scale=HEAD_DIM ** -0.5)
    outs = pl.pallas_call(kern, grid=(n // tm,), in_specs=in_specs,
                          out_specs=[table[k][1] for k in names], out_shape=[table[k][0] for k in names],
                          compiler_params=_params("parallel"), name="inproj")(x, cos_t, sin_t, *weights)
    return dict(zip(names, outs))


def _stack_halves(q):
    lane = lax.broadcasted_iota(I32, q.shape, 1)
    zero = jnp.zeros_like(q)
    return jnp.concatenate([jnp.where(lane < HEAD_DIM, q, zero), jnp.where(lane >= HEAD_DIM, q, zero)], axis=0)


def _lane_fold(a):
    out = a[:, :LANES]
    for c in range(LANES, a.shape[1], LANES):
        out = out + a[:, c:c + LANES]
    return out


def _online_update(s_ref, v, m_sc, l_sc, acc_sc, fin=None):
    s = s_ref[...]
    if fin is not None:
        s = fin(s, 0)
    m_prev = m_sc[...]
    m_new = jnp.maximum(m_prev, jnp.max(s, axis=-1, keepdims=True))
    alpha = jnp.exp2(m_prev - m_new)
    p = jnp.exp2(s - jnp.tile(m_new, (1, s.shape[1] // LANES)))
    l_sc[...] = alpha * l_sc[...] + _lane_fold(p)
    acc_sc[...] = alpha * acc_sc[...] + _dot(p.astype(BF16), v)
    m_sc[...] = m_new


def _finish(l_sc, acc_sc):
    return acc_sc[...] * (1.0 / jnp.sum(l_sc[...], axis=-1, keepdims=True))


def _attend_chunks(n, score, value, s_a, s_b, m_sc, l_sc, acc_sc, last=None):
    s_a[...] = score(0)

    def pair(p, carry):
        j = 2 * p
        s_b[...] = score(j + 1)
        _online_update(s_a, value(j), m_sc, l_sc, acc_sc)
        s_a[...] = score(j + 2)
        _online_update(s_b, value(j + 1), m_sc, l_sc, acc_sc)
        return carry

    lax.fori_loop(0, (n - 1) // 2, pair, 0)

    @pl.when(n % 2 == 1)
    def _():
        _online_update(s_a, value(n - 1), m_sc, l_sc, acc_sc, last)

    @pl.when(n % 2 == 0)
    def _():
        s_b[...] = score(n - 1)
        _online_update(s_a, value(n - 2), m_sc, l_sc, acc_sc)
        _online_update(s_b, value(n - 1), m_sc, l_sc, acc_sc, last)


def _lambda(lam_ref, lam_init):
    a = jnp.sum(lam_ref[0:1, :] * lam_ref[1:2, :], axis=-1, keepdims=True)
    b = jnp.sum(lam_ref[2:3, :] * lam_ref[3:4, :], axis=-1, keepdims=True)
    return jnp.exp(a) - jnp.exp(b) + lam_init


def _attn_scratch(rows, kc):
    return [pltpu.VMEM((rows, LANES), F32)] * 3 + [pltpu.VMEM((rows, kc), F32)] * 2


def _diff_attn_kernel(lam_ref, gsub_ref, q_ref, kt_ref, v_ref, o_ref, m_sc, l_sc, acc_sc, s_a, s_b, *,
                      tq, kc, lam_init):
    q0 = pl.program_id(2) * tq
    n_c = q0 // kc + 1
    q2 = _stack_halves(q_ref[...])
    m_sc[...] = jnp.full(m_sc.shape, -jnp.inf, F32)
    l_sc[...] = jnp.zeros(l_sc.shape, F32)
    acc_sc[...] = jnp.zeros(acc_sc.shape, F32)

    def causal(s, row0):
        r = lax.broadcasted_iota(I32, s.shape, 0) + row0
        r = jnp.where(r >= tq, r - tq, r) + q0
        c = lax.broadcasted_iota(I32, s.shape, 1) + (n_c - 1) * kc
        return jnp.where(c <= r, s, -jnp.inf)

    _attend_chunks(n_c, lambda j: _dot(q2, kt_ref[j]),
                   lambda j: v_ref[pl.ds(pl.multiple_of(j * kc, kc), kc), :],
                   s_a, s_b, m_sc, l_sc, acc_sc, last=causal)

    a = _finish(l_sc, acc_sc)
    o = a[:tq] - _lambda(lam_ref, lam_init) * a[tq:]
    ms = jnp.mean(o * o, axis=-1, keepdims=True)
    on = ((o * lax.rsqrt(ms + EPS)) * gsub_ref[...]) * (1.0 - lam_init)
    o_ref[...] = on.astype(BF16)


def _diff_attn(q, kt, v, lam_vecs, g_sub, lam_init, tq):
    b, t, w = q.shape
    nc, kc = kt.shape[1], kt.shape[3]
    assert kc % tq == 0
    grid = (b, w // LANES, t // tq)
    qmap = lambda bi, hi, qi: (bi, qi, hi)
    const = lambda bi, hi, qi: (0, 0)
    kern = functools.partial(_diff_attn_kernel, tq=tq, kc=kc, lam_init=lam_init)
    return pl.pallas_call(
        kern, grid=grid,
        in_specs=[pl.BlockSpec(lam_vecs.shape, const), pl.BlockSpec(g_sub.shape, const),
                  pl.BlockSpec((None, tq, LANES), qmap),
                  pl.BlockSpec((None, nc, LANES, kc), lambda bi, hi, qi: (bi, 0, hi, 0)),
                  pl.BlockSpec((None, t, LANES), lambda bi, hi, qi: (bi, 0, hi))],
        out_specs=pl.BlockSpec((None, tq, LANES), qmap),
        out_shape=jax.ShapeDtypeStruct((b, t, w), BF16),
        scratch_shapes=_attn_scratch(2 * tq, kc),
        compiler_params=_params("parallel", "parallel", "arbitrary"), name="diff_attn")(lam_vecs, g_sub, q, kt, v)


def _key_to_f32(k):
    b = k ^ (lax.shift_right_arithmetic(k, 31) & 0x7FFFFFFF)
    return lax.bitcast_convert_type(b, F32)


def _select_bias(score_sc, bias_sc, n_c, k_sel, row_pos, col0, t_sc=None):
    nc_max, r, kc = score_sc.shape

    def count(pred):
        def body(c, a):
            return a + _lane_fold(jnp.where(pred(score_sc[c]), 1.0, 0.0))
        a = lax.fori_loop(0, n_c, body, jnp.zeros((r, LANES), F32))
        return jnp.sum(a, axis=-1, keepdims=True)

    def search(k):
        def bit_body(i, t):
            cand = t + lax.shift_left(jnp.int32(1), 31 - i)
            cand_f = _key_to_f32(cand)
            a = jnp.zeros((r, LANES), F32)
            for c in range(k):
                a = a + _lane_fold(jnp.where(score_sc[c] >= cand_f, 1.0, 0.0))
            cnt = jnp.sum(a, axis=-1, keepdims=True)
            return jnp.where(cnt >= k_sel, cand, t)
        return lax.fori_loop(0, 32, bit_body, jnp.full((r, 1), INT_MIN, I32))

    if isinstance(n_c, int):
        t = search(n_c)
    else:
        for k in range(1, nc_max + 1):
            @pl.when(n_c == k)
            def _(k=k):
                t_sc[...] = jnp.broadcast_to(search(k), t_sc.shape)
        t = t_sc[:, :1]
    thr = _key_to_f32(jnp.maximum(t, KEY_NEG_INF))
    need = k_sel - count(lambda sc: sc > thr)

    ri = lax.broadcasted_iota(I32, (kc, kc), 0)
    ci = lax.broadcasted_iota(I32, (kc, kc), 1)
    tri = jnp.where(ri <= ci, 1.0, 0.0).astype(BF16)
    col = lax.broadcasted_iota(I32, (r, kc), 1)

    def bias_body(c, run):
        sc = score_sc[c]
        eq = sc == thr
        pre = _dot(jnp.where(eq, 1.0, 0.0).astype(BF16), tri)
        sel = (sc > thr) | (eq & ((pre + run) <= need))
        causal = (col + (col0 + c * kc)) <= row_pos
        bias_sc[c] = jnp.where(sel & causal, 0.0, MASK_NEG)
        return run + pre[:, kc - 1:kc]

    lax.fori_loop(0, n_c, bias_body, jnp.zeros((r, 1), F32))


def _dsa_attn_kernel(qi_ref, ki2t_ref, wi_ref, qs_ref, kst_ref, vs_ref, o_ref,
                     qstack_sc, score_sc, bias_sc, t_sc, q2_sc, m_sc, l_sc, acc_sc, *, tq, kc, k_sel, h_idx):
    qb = pl.program_id(1)
    q0 = qb * tq
    n_c = q0 // kc + 1
    row_pos = q0 + lax.broadcasted_iota(I32, (tq, 1), 0)

    for hh in range(h_idx):
        pair = qi_ref[:, (hh // 2) * LANES:(hh // 2 + 1) * LANES]
        lane = lax.broadcasted_iota(I32, pair.shape, 1)
        keep = (lane < HEAD_DIM) if hh % 2 == 0 else (lane >= HEAD_DIM)
        qstack_sc[hh * tq:(hh + 1) * tq, :] = jnp.where(keep, pair, jnp.zeros_like(pair))
    wi = wi_ref[...]

    def score_body(c, carry):
        d = _dot(qstack_sc[...], ki2t_ref[c])
        acc = jnp.zeros((tq, kc), F32)
        for hh in range(h_idx):
            acc = acc + jnp.maximum(d[hh * tq:(hh + 1) * tq], 0.0) * wi[:, HEAD_DIM + hh:HEAD_DIM + hh + 1]
        col = c * kc + lax.broadcasted_iota(I32, (tq, kc), 1)
        score_sc[c] = jnp.where(col <= row_pos, acc, -jnp.inf)
        return carry

    lax.fori_loop(0, n_c, score_body, 0)
    _select_bias(score_sc, bias_sc, n_c, k_sel, row_pos, 0, t_sc)

    n_hp = qs_ref.shape[1] // LANES
    for hp in range(n_hp):
        q2_sc[hp] = _stack_halves(qs_ref[:, hp * LANES:(hp + 1) * LANES])
    m_sc[...] = jnp.full(m_sc.shape, MASK_NEG, F32)
    l_sc[...] = jnp.zeros(l_sc.shape, F32)
    acc_sc[...] = jnp.zeros(acc_sc.shape, F32)

    def attn_body(c, carry):
        bias = bias_sc[c]
        bias2 = jnp.concatenate([bias, bias], axis=0)
        rows = pl.ds(pl.multiple_of(c * kc, kc), kc)
        for hp in range(n_hp):
            cols = slice(hp * LANES, (hp + 1) * LANES)
            s = _dot(q2_sc[hp], kst_ref[c, cols, :]) + bias2
            _online_update(s, vs_ref[rows, cols], m_sc.at[hp], l_sc.at[hp], acc_sc.at[hp])
        return carry

    lax.fori_loop(0, n_c, attn_body, 0)
    lane = lax.broadcasted_iota(I32, (tq, LANES), 1)
    for hp in range(n_hp):
        a = _finish(l_sc.at[hp], acc_sc.at[hp])
        o_ref[:, hp * LANES:(hp + 1) * LANES] = jnp.where(lane < HEAD_DIM, a[:tq], a[tq:]).astype(BF16)


def _dsa_attn(qi, ki2t, wi, qs, kst, vs, k_sel, h_idx, tq):
    b, t, w = qs.shape
    nc, kc = kst.shape[1], kst.shape[3]
    assert kc % tq == 0 and kc >= k_sel
    qmap = lambda bi, qb: (bi, qb, 0)
    kern = functools.partial(_dsa_attn_kernel, tq=tq, kc=kc, k_sel=k_sel, h_idx=h_idx)
    return pl.pallas_call(
        kern, grid=(b, t // tq),
        in_specs=[pl.BlockSpec((None, tq, qi.shape[2]), qmap),
                  pl.BlockSpec((None, nc, LANES, kc), lambda bi, qb: (bi, 0, 0, 0)),
                  pl.BlockSpec((None, tq, LANES), qmap), pl.BlockSpec((None, tq, w), qmap),
                  pl.BlockSpec((None, nc, w, kc), lambda bi, qb: (bi, 0, 0, 0)),
                  pl.BlockSpec((None, t, w), lambda bi, qb: (bi, 0, 0))],
        out_specs=pl.BlockSpec((None, tq, w), qmap),
        out_shape=jax.ShapeDtypeStruct((b, t, w), BF16),
        scratch_shapes=[pltpu.VMEM((h_idx * tq, LANES), BF16), pltpu.VMEM((nc, tq, kc), F32),
                        pltpu.VMEM((nc, tq, kc), F32), pltpu.VMEM((tq, LANES), I32),
                        pltpu.VMEM((w // LANES, 2 * tq, LANES), BF16)]
                       + [pltpu.VMEM((w // LANES, 2 * tq, LANES), F32)] * 3,
        compiler_params=_params("parallel", "arbitrary"), name="dsa_attn")(qi, ki2t, wi, qs, kst, vs)


def _page_specs(n_pages, block):
    def spec(j):
        return pl.BlockSpec((None,) + block, lambda b, pt, j=j: (pt[b, j],) + (0,) * len(block))
    return [spec(j) for j in range(n_pages)]


def _idx_score_kernel(pt_ref, q_ref, w_ref, knew_ref, *refs, n_pages, page, pad):
    page_refs, o_ref = refs[:n_pages], refs[n_pages]
    q = q_ref[...]
    w = w_ref[...]
    for j in range(n_pages):
        d = _dot3(q, page_refs[j][...])
        o_ref[:, j * page:(j + 1) * page] = jnp.sum(jnp.maximum(d, 0.0) * w, axis=0, keepdims=True)
    d_new = jnp.sum(q * knew_ref[...], axis=-1, keepdims=True)
    s_new = jnp.sum(jnp.maximum(d_new, 0.0) * w, axis=0, keepdims=True)
    lane = lax.broadcasted_iota(I32, (1, pad), 1)
    o_ref[:, n_pages * page:] = jnp.where(lane == 0, s_new, -jnp.inf)


def _idx_scores(page_table, q_i, w_i, k_new, cache_t, pad):
    db, n_pages = page_table.shape
    _, dh, page = cache_t.shape
    h_idx = q_i.shape[1]
    tk = n_pages * page + pad
    seq = lambda b, pt: (b, 0, 0)
    kern = functools.partial(_idx_score_kernel, n_pages=n_pages, page=page, pad=pad)
    grid_spec = pltpu.PrefetchScalarGridSpec(
        num_scalar_prefetch=1, grid=(db,),
        in_specs=[pl.BlockSpec((None, h_idx, dh), seq), pl.BlockSpec((None, h_idx, 1), seq),
                  pl.BlockSpec((None, 1, dh), seq)] + _page_specs(n_pages, (dh, page)),
        out_specs=pl.BlockSpec((None, 1, tk), seq))
    out = pl.pallas_call(kern, grid_spec=grid_spec, out_shape=jax.ShapeDtypeStruct((db, 1, tk), F32),
                         compiler_params=_params("parallel"), name="idx_scores")(
        page_table, q_i, w_i, k_new, *([cache_t] * n_pages))
    return out.reshape(db, tk)


def _select_kernel(s_ref, o_ref, score_sc, bias_sc, *, kc, k_sel, q_pos):
    r = s_ref.shape[0]
    nc = s_ref.shape[1] // kc
    for c in range(nc):
        score_sc[c] = s_ref[:, c * kc:(c + 1) * kc]
    row_pos = jnp.full((r, 1), q_pos, I32)
    _select_bias(score_sc, bias_sc, nc, k_sel, row_pos, 0)
    for c in range(nc):
        o_ref[:, c * kc:(c + 1) * kc] = bias_sc[c]


def _select(scores, k_sel, q_pos, kc):
    r, tk = scores.shape
    nc = tk // kc
    kern = functools.partial(_select_kernel, kc=kc, k_sel=k_sel, q_pos=q_pos)
    return pl.pallas_call(
        kern, grid=(1,), in_specs=[pl.BlockSpec((r, tk), lambda i: (0, 0))],
        out_specs=pl.BlockSpec((r, tk), lambda i: (0, 0)), out_shape=jax.ShapeDtypeStruct((r, tk), F32),
        scratch_shapes=[pltpu.VMEM((nc, r, kc), F32), pltpu.VMEM((nc, r, kc), F32)],
        compiler_params=_params("arbitrary"), name="select")(scores)


def _block_diag_rows(q, width):
    t = jnp.concatenate([q] * (width // HEAD_DIM), axis=1)
    row = lax.broadcasted_iota(I32, t.shape, 0)
    col = lax.broadcasted_iota(I32, t.shape, 1)
    return jnp.where((col >= row * HEAD_DIM) & (col < (row + 1) * HEAD_DIM), t, jnp.zeros_like(t))


def _decode_attn_kernel(pt_ref, lam_ref, gsub_ref, qd_ref, kdn_ref, vdn_ref, qs_ref, ksn_ref, vsn_ref, bias_ref,
                        *refs, n_pages, page, lam_init):
    kd_pages = refs[:n_pages]
    vd_pages = refs[n_pages:2 * n_pages]
    ks_pages = refs[2 * n_pages:3 * n_pages]
    vs_pages = refs[3 * n_pages:4 * n_pages]
    od_ref, os_ref = refs[4 * n_pages], refs[4 * n_pages + 1]
    h_diff = vd_pages[0].shape[0] // page

    def probs(q_ref, knew_ref, k_pages, bias):
        q = q_ref[...]
        qbd = _block_diag_rows(q, k_pages[0].shape[0])
        s = [_dot3(qbd, k_pages[j][...]) for j in range(n_pages)]
        s_new = jnp.sum(q * knew_ref[...], axis=-1, keepdims=True)
        if bias is not None:
            s = [s[j] + bias[:, j * page:(j + 1) * page] for j in range(n_pages)]
            s_new = s_new + bias[:, n_pages * page:n_pages * page + 1]
        m = s_new
        for j in range(n_pages):
            m = jnp.maximum(m, jnp.max(s[j], axis=-1, keepdims=True))
        p_new = jnp.exp2(s_new - m)
        p = [jnp.exp2(s[j] - m) for j in range(n_pages)]
        l = p_new
        for j in range(n_pages):
            l = l + jnp.sum(p[j], axis=-1, keepdims=True)
        return p, p_new, 1.0 / l

    p, p_new, inv_l = probs(qd_ref, kdn_ref, kd_pages, None)
    acc = p_new * vdn_ref[...]
    for j in range(n_pages):
        v = jnp.concatenate([vd_pages[j][pl.ds(hh, page, stride=h_diff), :] for hh in range(h_diff)], axis=1)
        acc = acc + _dot3(p[j], v)
    a = acc * inv_l
    row = lax.broadcasted_iota(I32, a.shape, 0)
    col = lax.broadcasted_iota(I32, a.shape, 1)
    own = (col // LANES) == (row // 2)
    coef = jnp.where(row % 2 == 0, 1.0, -_lambda(lam_ref, lam_init))
    o = jnp.sum(jnp.where(own, a * coef, 0.0), axis=0, keepdims=True)
    parts = []
    for c in range(0, o.shape[1], LANES):
        oc = o[:, c:c + LANES]
        ms = jnp.mean(oc * oc, axis=-1, keepdims=True)
        parts.append(((oc * lax.rsqrt(ms + EPS)) * gsub_ref[...]) * (1.0 - lam_init))
    od_ref[...] = jnp.concatenate(parts, axis=1)

    p, p_new, inv_l = probs(qs_ref, ksn_ref, ks_pages, bias_ref[...])
    acc = p_new * vsn_ref[...]
    for j in range(n_pages):
        acc = acc + _dot3(p[j], vs_pages[j][...], nt=True)
    a = acc * inv_l
    row = lax.broadcasted_iota(I32, a.shape, 0)
    col = lax.broadcasted_iota(I32, a.shape, 1)
    os_ref[...] = jnp.sum(jnp.where((col // HEAD_DIM) == row, a, 0.0), axis=0, keepdims=True)


def _decode_attn(page_table, lam_vecs, g_sub, qd, kdn, vdn, qs, ksn, vsn, bias, caches, lam_init):
    db, n_pages = page_table.shape
    c_kd, c_vd, c_ks, c_vs = caches
    w_diff, page = c_kd.shape[1:]
    w_dsa = c_ks.shape[1]
    seq = lambda b, pt: (b, 0, 0)
    const = lambda b, pt: (0, 0)
    in_specs = [pl.BlockSpec(lam_vecs.shape, const), pl.BlockSpec(g_sub.shape, const),
                pl.BlockSpec((None,) + qd.shape[1:], seq), pl.BlockSpec((None,) + kdn.shape[1:], seq),
                pl.BlockSpec((None,) + vdn.shape[1:], seq),
                pl.BlockSpec((None,) + qs.shape[1:], seq), pl.BlockSpec((None,) + ksn.shape[1:], seq),
                pl.BlockSpec((None,) + vsn.shape[1:], seq), pl.BlockSpec((None,) + bias.shape[1:], seq)]
    for cache in caches:
        in_specs += _page_specs(n_pages, tuple(cache.shape[1:]))
    kern = functools.partial(_decode_attn_kernel, n_pages=n_pages, page=page, lam_init=lam_init)
    grid_spec = pltpu.PrefetchScalarGridSpec(
        num_scalar_prefetch=1, grid=(db,), in_specs=in_specs,
        out_specs=[pl.BlockSpec((None, 1, w_diff), seq), pl.BlockSpec((None, 1, w_dsa), seq)])
    pages = []
    for cache in caches:
        pages += [cache] * n_pages
    od, os_ = pl.pallas_call(
        kern, grid_spec=grid_spec,
        out_shape=[jax.ShapeDtypeStruct((db, 1, w_diff), F32), jax.ShapeDtypeStruct((db, 1, w_dsa), F32)],
        compiler_params=_params("parallel"), name="decode_attn")(
        page_table, lam_vecs, g_sub, qd, kdn, vdn, qs, ksn, vsn, bias, *pages)
    return od.reshape(db, w_diff), os_.reshape(db, w_dsa)


ROUTE_E1, ROUTE_E2, ROUTE_R1, ROUTE_R2, ROUTE_W1, ROUTE_W2 = range(6)


def _outproj_kernel(x_ref, od_ref, os_ref, gate_ref, wod_ref, wos_ref, wout_ref, gffn_ref, wrt_ref,
                    x1_ref, h2_ref, comb_ref, route_ref, count_ref, *, precise, n_groups, n_experts):
    d = x_ref.shape[1]

    @pl.when(pl.program_id(0) == 0)
    def _():
        count_ref[...] = jnp.zeros(count_ref.shape, F32)

    yd = _mm(od_ref[...], wod_ref, precise)
    ys = _mm(os_ref[...], wos_ref, precise)
    merged = gate_ref[:, :d].astype(F32) * yd + gate_ref[:, d:].astype(F32) * ys
    x1 = x_ref[...] + _mm(merged, wout_ref, precise)
    x1_ref[...] = x1

    ms = jnp.mean(x1 * x1, axis=-1, keepdims=True)
    h2 = (x1 * lax.rsqrt(ms + EPS)) * gffn_ref[...]
    h2_ref[...] = h2

    lg = _dot3(h2, wrt_ref[...])
    lane = lax.broadcasted_iota(I32, lg.shape, 1)
    lanef = lane.astype(F32)
    big = float(LANES)
    epg = n_experts // n_groups

    gmask = (lane >= n_experts) & (lane < n_experts + n_groups)
    gl = jnp.where(gmask, lg, -jnp.inf)
    gmax = jnp.max(gl, axis=-1, keepdims=True)
    gsel = jnp.min(jnp.where(gl == gmax, lanef, big), axis=-1, keepdims=True) - n_experts
    g_w = 1.0 / jnp.sum(jnp.exp(gl - gmax), axis=-1, keepdims=True)

    lo_lane = gsel * epg
    emask = (lanef >= lo_lane) & (lanef < lo_lane + epg)
    el = jnp.where(emask, lg, -jnp.inf)
    v1 = jnp.max(el, axis=-1, keepdims=True)
    i1 = jnp.min(jnp.where(el == v1, lanef, big), axis=-1, keepdims=True)
    el2 = jnp.where(lanef == i1, -jnp.inf, el)
    v2 = jnp.max(el2, axis=-1, keepdims=True)
    i2 = jnp.min(jnp.where(el2 == v2, lanef, big), axis=-1, keepdims=True)
    e21 = jnp.exp(v2 - v1)
    den = 1.0 / (1.0 + e21)
    w1 = den * g_w
    w2 = (e21 * den) * g_w
    comb_ref[...] = jnp.where(lanef == i1, w1, 0.0) + jnp.where(lanef == i2, w2, 0.0)

    tm = lg.shape[0]
    onehot = jnp.where((lanef == i1) | (lanef == i2), 1.0, 0.0)
    ri = lax.broadcasted_iota(I32, (tm, tm), 0)
    ci = lax.broadcasted_iota(I32, (tm, tm), 1)
    before = jnp.where(ci < ri, 1.0, 0.0).astype(BF16)
    rank_e = count_ref[...] + _dot(before, onehot.astype(BF16))
    r1 = jnp.sum(jnp.where(lanef == i1, rank_e, 0.0), axis=-1, keepdims=True)
    r2 = jnp.sum(jnp.where(lanef == i2, rank_e, 0.0), axis=-1, keepdims=True)
    count_ref[...] += jnp.sum(onehot, axis=0, keepdims=True)
    rec = jnp.zeros(lg.shape, F32)
    for slot, val in ((ROUTE_E1, i1), (ROUTE_E2, i2), (ROUTE_W1, w1), (ROUTE_W2, w2), (ROUTE_R1, r1), (ROUTE_R2, r2)):
        rec = jnp.where(lane == slot, val, rec)
    route_ref[...] = rec


def _outproj(x, od, os_, gates, prm, tm, sample):
    n, d = x.shape
    row = lambda i: (i, 0)
    const = lambda i: (0, 0)
    full = lambda a: pl.BlockSpec(a.shape, const)
    proj = ("w_od_f", "w_os_f", "w_out_f") if sample else ("w_od", "w_os", "w_out")
    weights = [prm[k] for k in proj + ("g_ffn", "w_rt_f")]
    kern = functools.partial(_outproj_kernel, precise=sample, n_groups=prm["n_groups"], n_experts=prm["n_experts"])
    return pl.pallas_call(
        kern, grid=(n // tm,),
        in_specs=[pl.BlockSpec((tm, d), row), pl.BlockSpec((tm, od.shape[1]), row), pl.BlockSpec((tm, os_.shape[1]), row),
                  pl.BlockSpec((tm, gates.shape[1]), row)] + [full(w) for w in weights],
        out_specs=[pl.BlockSpec((tm, d), row), pl.BlockSpec((tm, d), row), pl.BlockSpec((tm, LANES), row),
                   pl.BlockSpec((tm, LANES), row), pl.BlockSpec((1, LANES), const)],
        out_shape=[jax.ShapeDtypeStruct((n, d), F32), jax.ShapeDtypeStruct((n, d), F32),
                   jax.ShapeDtypeStruct((n, LANES), F32), jax.ShapeDtypeStruct((n, LANES), F32),
                   jax.ShapeDtypeStruct((1, LANES), F32)],
        compiler_params=_params("arbitrary"), name="outproj")(x, od, os_, gates, *weights)


def _expert_ffn(h, wg_ref, wu_ref, wd_ref):
    h = h.astype(BF16)
    a = _dot(h, wg_ref[...].astype(BF16))
    u = _dot(h, wu_ref[...].astype(BF16))
    act = (a * (1.0 / (1.0 + jnp.exp(-a)))) * u
    return _dot(act.astype(BF16), wd_ref[...].astype(BF16))


def _moe_kernel(x1_ref, h2_ref, comb_ref, wg_ref, wu_ref, wd_ref, y_ref, acc_sc):
    e = pl.program_id(1)

    @pl.when(e == 0)
    def _():
        acc_sc[...] = jnp.zeros(acc_sc.shape, F32)

    o = _expert_ffn(h2_ref[...], wg_ref, wu_ref, wd_ref)
    comb = comb_ref[...]
    lane = lax.broadcasted_iota(I32, comb.shape, 1)
    c = jnp.sum(jnp.where(lane == e, comb, 0.0), axis=-1, keepdims=True)
    acc_sc[...] += o * c

    @pl.when(e == pl.num_programs(1) - 1)
    def _():
        y_ref[...] = x1_ref[...] + acc_sc[...]


def _moe(x1, h2, comb, w_gate, w_up, w_down, tm):
    n, d = x1.shape
    ne, _, de = w_gate.shape
    row = lambda i, e: (i, 0)
    return pl.pallas_call(
        _moe_kernel, grid=(n // tm, ne),
        in_specs=[pl.BlockSpec((tm, d), row), pl.BlockSpec((tm, d), row), pl.BlockSpec((tm, LANES), row),
                  pl.BlockSpec((None, d, de), lambda i, e: (e, 0, 0)), pl.BlockSpec((None, d, de), lambda i, e: (e, 0, 0)),
                  pl.BlockSpec((None, de, d), lambda i, e: (e, 0, 0))],
        out_specs=pl.BlockSpec((tm, d), row), out_shape=jax.ShapeDtypeStruct((n, d), F32),
        scratch_shapes=[pltpu.VMEM((tm, d), F32)],
        compiler_params=_params("parallel", "arbitrary"), name="moe")(x1, h2, comb, w_gate, w_up, w_down)


def _row_copy(src, src_row, dst, dst_row, sem):
    return pltpu.make_async_copy(src.at[pl.ds(src_row, 1), :], dst.at[pl.ds(dst_row, 1), :], sem)


def _slot_rows(assign_ref, offs_ref, n, tok):
    return (offs_ref[assign_ref[tok]] + assign_ref[2 * n + tok],
            offs_ref[assign_ref[n + tok]] + assign_ref[3 * n + tok])


def _scatter_kernel(assign_ref, offs_ref, group_ref, n_tiles_ref, h2_ref, xs_hbm, zero_sc, sem, zero_sem, *,
                    n, tm, tr):
    base = pl.program_id(0) * tm
    n_experts = offs_ref.shape[0]
    total_tiles = xs_hbm.shape[0] // tr

    def zero_tile(row0):
        return pltpu.make_async_copy(zero_sc, xs_hbm.at[pl.ds(pl.multiple_of(row0, tr), tr), :], zero_sem)

    @pl.when(pl.program_id(0) == 0)
    def _():
        zero_sc[...] = jnp.zeros(zero_sc.shape, F32)
        for wait in (False, True):
            def group_tail(e, carry, wait=wait):
                @pl.when(group_ref[e] > 0)
                def _():
                    cp = zero_tile(offs_ref[e] + group_ref[e] - tr)
                    cp.wait() if wait else cp.start()
                return carry

            def unused(j, carry, wait=wait):
                cp = zero_tile(j * tr)
                cp.wait() if wait else cp.start()
                return carry

            lax.fori_loop(0, n_experts, group_tail, 0)
            lax.fori_loop(n_tiles_ref[0], total_tiles, unused, 0)

    def start(t, carry):
        p1, p2 = _slot_rows(assign_ref, offs_ref, n, base + t)
        _row_copy(h2_ref, t, xs_hbm, p1, sem).start()
        _row_copy(h2_ref, t, xs_hbm, p2, sem).start()
        return carry

    def wait(t, carry):
        p1, p2 = _slot_rows(assign_ref, offs_ref, n, base + t)
        _row_copy(h2_ref, t, xs_hbm, p1, sem).wait()
        _row_copy(h2_ref, t, xs_hbm, p2, sem).wait()
        return carry

    lax.fori_loop(0, tm, start, 0, unroll=8)
    lax.fori_loop(0, tm, wait, 0, unroll=8)


def _scatter_rows(assign, offs, group, n_tiles, h2, n_rows, tm, tr):
    n, d = h2.shape
    grid_spec = pltpu.PrefetchScalarGridSpec(
        num_scalar_prefetch=4, grid=(n // tm,),
        in_specs=[pl.BlockSpec((tm, d), lambda i, *_: (i, 0))], out_specs=pl.BlockSpec(memory_space=pl.ANY),
        scratch_shapes=[pltpu.VMEM((tr, d), F32), pltpu.SemaphoreType.DMA(()), pltpu.SemaphoreType.DMA(())])
    return pl.pallas_call(
        functools.partial(_scatter_kernel, n=n, tm=tm, tr=tr), grid_spec=grid_spec,
        out_shape=jax.ShapeDtypeStruct((n_rows, d), F32),
        compiler_params=_params("arbitrary", disable_bounds_checks=True), name="moe_scatter")(
        assign, offs, group, n_tiles, h2)


def _grouped_ffn_kernel(tile_expert_ref, n_tiles_ref, xs_ref, wg_ref, wu_ref, wd_ref, ys_ref):
    used = pl.program_id(0) < n_tiles_ref[0]

    @pl.when(used)
    def _():
        ys_ref[...] = _expert_ffn(xs_ref[...], wg_ref, wu_ref, wd_ref)

    @pl.when(jnp.logical_not(used))
    def _():
        ys_ref[...] = jnp.zeros(ys_ref.shape, F32)


def _grouped_ffn(tile_expert, n_tiles, xs, w_gate, w_up, w_down, tr):
    rows, d = xs.shape
    ne, _, de = w_gate.shape
    last = lambda i, te, nt: jnp.minimum(i, nt[0] - 1)
    xmap = lambda i, te, nt: (last(i, te, nt), 0)
    wmap = lambda i, te, nt: (te[last(i, te, nt)], 0, 0)
    grid_spec = pltpu.PrefetchScalarGridSpec(
        num_scalar_prefetch=2, grid=(rows // tr,),
        in_specs=[pl.BlockSpec((tr, d), xmap), pl.BlockSpec((None, d, de), wmap), pl.BlockSpec((None, d, de), wmap),
                  pl.BlockSpec((None, de, d), wmap)],
        out_specs=pl.BlockSpec((tr, d), lambda i, te, nt: (i, 0)))
    return pl.pallas_call(
        _grouped_ffn_kernel, grid_spec=grid_spec, out_shape=jax.ShapeDtypeStruct((rows, d), F32),
        compiler_params=_params("arbitrary"), name="moe_ffn")(tile_expert, n_tiles, xs, w_gate, w_up, w_down)


def _combine_kernel(assign_ref, offs_ref, x1_ref, route_ref, ys_hbm, y_ref, buf1, buf2, sem, *, n, tm):
    base = pl.program_id(0) * tm

    def start(t, carry):
        p1, p2 = _slot_rows(assign_ref, offs_ref, n, base + t)
        _row_copy(ys_hbm, p1, buf1, t, sem).start()
        _row_copy(ys_hbm, p2, buf2, t, sem).start()
        return carry

    def wait(t, carry):
        p1, p2 = _slot_rows(assign_ref, offs_ref, n, base + t)
        _row_copy(ys_hbm, p1, buf1, t, sem).wait()
        _row_copy(ys_hbm, p2, buf2, t, sem).wait()
        return carry

    lax.fori_loop(0, tm, start, 0, unroll=8)
    lax.fori_loop(0, tm, wait, 0, unroll=8)
    route = route_ref[...]
    w1 = route[:, ROUTE_W1:ROUTE_W1 + 1]
    w2 = route[:, ROUTE_W2:ROUTE_W2 + 1]
    y_ref[...] = (x1_ref[...] + buf1[...] * w1) + buf2[...] * w2


def _combine_rows(assign, offs, x1, route, ys, tm):
    n, d = x1.shape
    row = lambda i, *_: (i, 0)
    grid_spec = pltpu.PrefetchScalarGridSpec(
        num_scalar_prefetch=2, grid=(n // tm,),
        in_specs=[pl.BlockSpec((tm, d), row), pl.BlockSpec((tm, LANES), row), pl.BlockSpec(memory_space=pl.ANY)],
        out_specs=pl.BlockSpec((tm, d), row),
        scratch_shapes=[pltpu.VMEM((tm, d), F32), pltpu.VMEM((tm, d), F32), pltpu.SemaphoreType.DMA(())])
    return pl.pallas_call(
        functools.partial(_combine_kernel, n=n, tm=tm), grid_spec=grid_spec,
        out_shape=jax.ShapeDtypeStruct((n, d), F32),
        compiler_params=_params("arbitrary", disable_bounds_checks=True), name="moe_combine")(
        assign, offs, x1, route, ys)


def _routed_moe(x1, h2, route, counts, w_gate, w_up, w_down, n_experts, tm, tr):
    n, d = x1.shape
    cnt = counts[0, :n_experts].astype(I32)
    group = ((cnt + tr - 1) // tr) * tr
    ends = jnp.cumsum(group)
    offs = ends - group
    assign = route[:, ROUTE_E1:ROUTE_R2 + 1].astype(I32).T.reshape(-1)
    n_rows = 2 * n + n_experts * tr
    tile_start = jnp.arange(n_rows // tr, dtype=I32) * tr
    tile_expert = jnp.minimum(jnp.sum(tile_start[:, None] >= ends[None, :], axis=1), n_experts - 1).astype(I32)
    n_tiles = (ends[-1:] // tr).astype(I32)
    xs = _scatter_rows(assign, offs, group, n_tiles, h2, n_rows, tm, tr)
    ys = _grouped_ffn(tile_expert, n_tiles, xs, w_gate, w_up, w_down, tr)
    return _combine_rows(assign, offs, x1, route, ys, tm)


def _rope_tables(pos):
    half = HEAD_DIM // 2
    inv_freq = jnp.power(ROPE_THETA, -jnp.arange(half, dtype=F32) / half)
    ang = pos.astype(F32)[:, None] * inv_freq[None, :]
    cos, sin = jnp.cos(ang), jnp.sin(ang)
    reps = LANES // HEAD_DIM
    return jnp.tile(cos, (1, 2 * reps)), jnp.tile(jnp.concatenate([-sin, sin], axis=1), (1, reps))


def _tile_rows(n, target):
    t = min(n, target)
    while n % t:
        t //= 2
    return t


def _key_outputs(o, bsz, t, h_diff, h_dsa, dh):
    kd = o["kdT_f"].reshape(bsz, h_diff, 2, dh, t).transpose(0, 4, 1, 2, 3)
    vd = o["vd_f"].reshape(bsz, t, h_diff, 2 * dh)
    ks = o["ksT_f"].reshape(bsz, h_dsa, dh, t).transpose(0, 3, 1, 2)
    vs = o["vsT_f"].reshape(bsz, h_dsa, dh, t).transpose(0, 3, 1, 2)
    ki = o["kiT_f"].transpose(0, 2, 1)
    return kd, vd, ks, vs, ki


def kernel(x_prompt, x_sample, cache_diff_k, cache_diff_v, cache_dsa_k, cache_dsa_v, cache_idx_k, page_table, g_mix, w_in, g_q_diff, g_k_diff, lam_q1, lam_k1, lam_q2, lam_k2, g_subln, g_q_dsa, g_k_dsa, w_o_diff, w_o_dsa, w_out, g_ffn, w_group, w_router, w_gate, w_up, w_down):
    bsz, seq, d_model = x_prompt.shape
    db, dec_seq, _ = x_sample.shape
    depth, n_pool, page, h_diff, _, dh = cache_diff_k.shape
    h_dsa = cache_dsa_k.shape[3]
    n_pages = page_table.shape[1]
    past_len = n_pages * page
    n_groups = w_group.shape[2]
    n_experts = w_router.shape[2]
    w_diff = h_diff * 2 * dh
    w_dsa = h_dsa * dh
    h_idx = (w_in.shape[2] - (3 * w_diff + 3 * w_dsa + dh + 2 * d_model)) // (dh + 1)
    w_idx = h_idx * dh
    assert dh == HEAD_DIM and cache_dsa_k.shape[4] == HEAD_DIM and cache_idx_k.shape[3] == HEAD_DIM
    assert dec_seq == 1 and depth == 1
    assert 3 * w_diff + 3 * w_dsa + w_idx + dh + h_idx + 2 * d_model == w_in.shape[2]
    k_sel_p = min(TOPK_MAX, seq // 4)
    k_sel_s = min(TOPK_MAX, (past_len + dec_seq) // 4)

    l = 0
    lam_init = 0.8 - 0.6 * math.exp(-0.3 * l)
    n_qkv = 3 * w_diff + 3 * w_dsa + w_idx
    w_l = w_in[l]
    rt = jnp.concatenate([w_router[l], w_group[l]], axis=1)
    rt = jnp.pad(rt, ((0, 0), (0, LANES - rt.shape[1])))
    blk = lax.broadcasted_iota(I32, (2 * LANES, 2 * LANES), 0) // HEAD_DIM
    blk_c = lax.broadcasted_iota(I32, (2 * LANES, 2 * LANES), 1) // HEAD_DIM
    prm = dict(
        w_diff=w_diff, w_dsa=w_dsa, w_idx=w_idx, h_idx=h_idx, n_groups=n_groups, n_experts=n_experts,
        g_mix=g_mix[l][None, :],
        g_qd=jnp.tile(g_q_diff[l], w_diff // dh)[None, :], g_kd=jnp.tile(g_k_diff[l], w_diff // dh)[None, :],
        g_qs=jnp.tile(g_q_dsa[l], w_dsa // dh)[None, :], g_ks=jnp.tile(g_k_dsa[l], w_dsa // dh)[None, :],
        bd=(blk == blk_c).astype(BF16), g_ffn=g_ffn[l][None, :])
    for name, w in (("w_qkv", w_l[:, :n_qkv]),
                    ("w_small", jnp.pad(w_l[:, n_qkv:n_qkv + dh + h_idx], ((0, 0), (0, LANES - dh - h_idx)))),
                    ("w_gate", w_l[:, n_qkv + dh + h_idx:]),
                    ("w_od", w_o_diff[l]), ("w_os", w_o_dsa[l]), ("w_out", w_out[l]), ("w_rt", rt)):
        prm[name] = w.astype(BF16)
        prm[name + "_f"] = w
    lam_vecs = jnp.stack([lam_q1[l], lam_k1[l], lam_q2[l], lam_k2[l]])
    g_sub = g_subln[l][None, :]
    moe_w = (w_gate[l], w_up[l], w_down[l])

    n_p = bsz * seq
    tm = _tile_rows(seq, ROW_TILE)
    kc = _tile_rows(seq, KEY_CHUNK)
    cos_p, sin_p = _rope_tables(jnp.arange(seq, dtype=I32))
    x_p = x_prompt.reshape(n_p, d_model)
    o = _inproj(x_p, cos_p, sin_p, bsz, tm, kc, prm, sample=False)
    r3 = lambda a: a.reshape(bsz, seq, a.shape[1])
    od = _diff_attn(r3(o["qd"]), o["kdT_b"], r3(o["vd_b"]), lam_vecs, g_sub, lam_init, tm)
    os_ = _dsa_attn(r3(o["qi"]), o["ki2T_b"], r3(o["wi"]), r3(o["qs"]), o["ksT_b"], r3(o["vs_b"]),
                    k_sel_p, h_idx, _tile_rows(seq, DSA_Q_TILE))
    x1, h2, _, route, counts = _outproj(x_p, od.reshape(n_p, w_diff), os_.reshape(n_p, w_dsa), o["gates"], prm, tm,
                                        sample=False)
    y_p = _routed_moe(x1, h2, route, counts, *moe_w, n_experts, tm, MOE_ROW_TILE)
    outs_p = _key_outputs(o, bsz, seq, h_diff, h_dsa, dh)

    cos_s, sin_s = _rope_tables(jnp.full((db,), past_len, I32))
    x_s = x_sample.reshape(db, d_model)
    o = _inproj(x_s, cos_s, sin_s, 1, db, db, prm, sample=True)
    pad = 2 * LANES
    assert pad >= k_sel_s
    scores = _idx_scores(page_table, o["qi"].reshape(db, h_idx, dh), o["wi"][:, dh:dh + h_idx].reshape(db, h_idx, 1),
                         o["ki_b"][:, :dh].reshape(db, 1, dh), cache_idx_k[l].transpose(0, 2, 1), pad)
    bias = _select(scores, k_sel_s, past_len, pad)
    caches = (cache_diff_k[l].transpose(0, 2, 3, 4, 1).reshape(n_pool, w_diff, page),
              cache_diff_v[l].reshape(n_pool, page * h_diff, 2 * dh),
              cache_dsa_k[l].transpose(0, 2, 3, 1).reshape(n_pool, w_dsa, page),
              cache_dsa_v[l].transpose(0, 2, 3, 1).reshape(n_pool, w_dsa, page))
    od, os_ = _decode_attn(page_table, lam_vecs, g_sub,
                           o["qd"].reshape(db, 2 * h_diff, dh), o["kd_b"].reshape(db, 2 * h_diff, dh),
                           o["vd_b"].reshape(db, 1, w_diff),
                           o["qs"].reshape(db, h_dsa, dh), o["ks_b"].reshape(db, h_dsa, dh),
                           o["vs_b"].reshape(db, 1, w_dsa),
                           bias.reshape(db, 1, bias.shape[1]), caches, lam_init)
    x1, h2, comb, _, _ = _outproj(x_s, od, os_, o["gates"], prm, db, sample=True)
    y_s = _moe(x1, h2, comb, *moe_w, db)
    outs_s = tuple(a.reshape((db, dec_seq) + a.shape[2:]) for a in _key_outputs(o, 1, db, h_diff, h_dsa, dh))

    return ((y_p.reshape(bsz, seq, d_model), y_s.reshape(db, dec_seq, d_model))
            + tuple(a[None] for a in outs_p) + tuple(a[None] for a in outs_s))
```
